```python
import math
import jax, jax.numpy as jnp
from jax import lax
import numpy as np

D_MODEL = 2048
BATCH = 4
SEQ = 8192
DEPTH = 4

GRID_W = 64
CTX_LEN = 256
RMS_EPS = 1e-6
POS_BASE = 10000.0
ADA_CHUNKS = 6

M_D_INNER = 2 * D_MODEL
M_HEADDIM = 64
M_HEADS = M_D_INNER // M_HEADDIM
M_GROUPS = 8
M_STATE = 128
M_CONV = 5
M_CHUNK = 128
M_GN = M_GROUPS * M_STATE
M_XBC = M_D_INNER + 2 * M_GN
M_DT_MIN = 1e-3
M_DT_MAX = 1e-1

H_WIDTH = D_MODEL
H_SHORT = 3
H_EMB = 33
H_BANDS = (H_EMB - 1) // 2
H_HIDDEN = 64
H_DECAY_TARGET = 1e-2
H_FAST_PCT = 0.3
H_SLOW_PCT = 1.5

FFN_HIDDEN = -(-8 * D_MODEL // (3 * 256)) * 256

OFF_DT = M_XBC
OFF_Z = OFF_DT + 2 * M_HEADS
OFF_HY = OFF_Z + M_D_INNER
OFF_GATE = OFF_HY + 3 * H_WIDTH
IN_COLS = OFF_GATE + 2 * D_MODEL

kernel_name = 'hybrid_ssd_hyena_prefix_trunk'


def rmsnorm(x, g):
    xf = x.astype(jnp.float32)
    y = xf * lax.rsqrt(jnp.mean(xf * xf, axis=-1, keepdims=True) + RMS_EPS)
    return (y * g.astype(jnp.float32)).astype(x.dtype)


def modulated_norm(x, g, shift, scale):
    return rmsnorm(x, g) * (1.0 + scale) + shift


def centred_dwconv(u, w, b):
    k = w.shape[0]
    pad = k // 2
    L = u.shape[1]
    up = jnp.pad(u, ((0, 0), (pad, pad), (0, 0)))
    y = up[:, 0:L] * w[0]
    for j in range(1, k):
        y = y + up[:, j:j + L] * w[j]
    return y + b


def grid_pos_embed(rows, d, dtype):
    r, col = jnp.meshgrid(jnp.arange(rows), jnp.arange(GRID_W), indexing='ij')
    quarter = d // 4
    omega = 1.0 / (POS_BASE ** (jnp.arange(quarter, dtype=jnp.float32) / quarter))

    def axis_embed(pos):
        ang = pos.reshape(-1)[:, None].astype(jnp.float32) * omega[None, :]
        return jnp.concatenate([jnp.sin(ang), jnp.cos(ang)], axis=-1)

    return jnp.concatenate([axis_embed(r), axis_embed(col)], axis=-1).astype(dtype)


def ssd_scan(x, dt, a, B, C, init, with_output):
    dtype = x.dtype
    f32 = jnp.float32
    b, L, H, P = x.shape
    G, N = B.shape[2], B.shape[3]
    R = H // G
    T = M_CHUNK
    nc = L // T
    xd = (x.astype(f32) * dt[..., None]).reshape(b, nc, T, G, R, P)
    Bc = B.astype(f32).reshape(b, nc, T, G, N)
    Cc = C.astype(f32).reshape(b, nc, T, G, N)
    a_cs = jnp.cumsum((dt * a).reshape(b, nc, T, G, R).transpose(0, 3, 4, 1, 2), axis=-1)
    decay_to_end = jnp.exp(a_cs[..., -1:] - a_cs)
    chunk_states = jnp.einsum('bctgn,bgrct,bctgrp->bcgrpn', Bc, decay_to_end, xd)
    chunk_decay = jnp.exp(a_cs[..., -1])

    def step(state, inp):
        dec, st = inp
        return dec[..., None, None] * state + st, state

    final, prev = lax.scan(step, init, (jnp.moveaxis(chunk_decay, -1, 0), jnp.moveaxis(chunk_states, 1, 0)))
    if not with_output:
        return None, final
    prev = jnp.moveaxis(prev, 0, 1)
    seg = a_cs[..., :, None] - a_cs[..., None, :]
    lower = jnp.tril(jnp.ones((T, T), dtype=bool))
    decay_in = jnp.exp(jnp.where(lower, seg, -jnp.inf))
    cb = jnp.einsum('bctgn,bcsgn->bgcts', Cc, Bc)
    y_diag = jnp.einsum('bgrcts,bcsgrp->bctgrp', cb[:, :, None] * decay_in, xd)
    y_off = jnp.einsum('bctgn,bcgrpn,bgrct->bctgrp', Cc, prev, jnp.exp(a_cs))
    return (y_diag + y_off).reshape(b, L, H, P).astype(dtype), final


def mamba_inputs(xbc_raw, dt_raw, conv_w, conv_b, dt_bias, a_log):
    b, L, _ = xbc_raw.shape
    xbc = jax.nn.silu(centred_dwconv(xbc_raw, conv_w, conv_b))
    xs = xbc[..., :M_D_INNER].reshape(b, L, M_HEADS, M_HEADDIM)
    Bm = xbc[..., M_D_INNER:M_D_INNER + M_GN].reshape(b, L, M_GROUPS, M_STATE)
    Cm = xbc[..., M_D_INNER + M_GN:].reshape(b, L, M_GROUPS, M_STATE)
    dt = jax.nn.softplus(dt_raw.astype(jnp.float32).reshape(b, L, 2, M_HEADS) + dt_bias.astype(jnp.float32))
    a = -jnp.exp(a_log.astype(jnp.float32))
    return xs, Bm, Cm, dt, a


def bidir_ssd(xs, Bm, Cm, dt, a, init_f, init_b, with_output):
    flip = lambda t: jnp.flip(t, axis=1)
    y_f, fin_f = ssd_scan(xs, dt[:, :, 0], a[0], Bm, Cm, init_f, with_output)
    y_b, fin_b = ssd_scan(flip(xs), flip(dt[:, :, 1]), a[1], flip(Bm), flip(Cm), init_b, with_output)
    if not with_output:
        return None, fin_f, fin_b
    return y_f + flip(y_b), fin_f, fin_b


def mamba_output(y, xs, z, d_skip, norm_g):
    b, L = z.shape[:2]
    y = (y + xs * d_skip[:, None]).reshape(b, L, M_D_INNER) * jax.nn.silu(z)
    y = rmsnorm(y.reshape(b, L, M_GROUPS, M_D_INNER // M_GROUPS), norm_g.reshape(M_GROUPS, -1))
    return y.reshape(b, L, M_D_INNER)


def hyena_filters(L, w1, b1, w2, b2, freq, w3):
    f32 = jnp.float32
    t = jnp.linspace(0.0, 1.0, L, dtype=f32)[:, None]
    ang = (2.0 * math.pi / L) * jnp.arange(L, dtype=f32)[:, None]
    bands = jnp.linspace(1e-4, H_BANDS - 1, H_BANDS, dtype=f32)[None, :]
    feats = jnp.concatenate([t, jnp.cos(bands * ang), -jnp.sin(bands * ang)], axis=-1)
    fr = freq.astype(f32)
    hid = jnp.sin(fr * (feats @ w1.astype(f32) + b1.astype(f32)))
    hid = jnp.sin(fr * (hid @ w2.astype(f32) + b2.astype(f32)))
    filt = hid @ w3.astype(f32)
    deltas = jnp.abs(jnp.linspace(math.log(H_DECAY_TARGET) / H_SLOW_PCT,
                                  math.log(H_DECAY_TARGET) / H_FAST_PCT, H_WIDTH, dtype=f32))
    filt = filt * jnp.exp(-t * jnp.tile(deltas, 2)[None, :])
    f_fwd, f_bwd = filt[:, :H_WIDTH], filt[:, H_WIDTH:]
    full = jnp.concatenate([f_fwd, jnp.zeros((1, H_WIDTH), f32), jnp.flip(f_bwd[1:], axis=0)], axis=0)
    return full * lax.rsqrt(jnp.sum(full * full, axis=0, keepdims=True) + RMS_EPS)


def fft_long_conv(u, filt, bias):
    L = u.shape[1]
    n = 2 * L
    uf = jnp.fft.rfft(u.astype(jnp.float32), n=n, axis=1)
    ff = jnp.fft.rfft(filt, n=n, axis=0)
    y = jnp.fft.irfft(uf * ff[None], n=n, axis=1)[:, :L]
    return (y + u.astype(jnp.float32) * bias.astype(jnp.float32)).astype(u.dtype)


def hyena_branch(proj, conv_w, conv_b, w1, b1, w2, b2, freq, w3, bias):
    uc = centred_dwconv(proj, conv_w, conv_b)
    x0, x1, v = jnp.split(uc, 3, axis=-1)
    filt = hyena_filters(proj.shape[1], w1, b1, w2, b2, freq, w3)
    return x0 * fft_long_conv(x1 * v, filt, bias)


def mix_sublayer(x, mods, lp, init_f, init_b):
    h = modulated_norm(x, lp['norm1_g'], mods[0], mods[1])
    p = h @ lp['w_in']
    xs, Bm, Cm, dt, a = mamba_inputs(p[..., :OFF_DT], p[..., OFF_DT:OFF_Z], lp['m_conv_w'], lp['m_conv_b'],
                                     lp['m_dt_bias'], lp['m_a_log'])
    y_ssd, fin_f, fin_b = bidir_ssd(xs, Bm, Cm, dt, a, init_f, init_b, True)
    y_m = mamba_output(y_ssd, xs, p[..., OFF_Z:OFF_HY], lp['m_d'], lp['m_norm_g'])
    y_h = hyena_branch(p[..., OFF_HY:OFF_GATE], lp['h_conv_w'], lp['h_conv_b'], lp['hf_w1'], lp['hf_b1'],
                       lp['hf_w2'], lp['hf_b2'], lp['hf_freq'], lp['hf_w3'], lp['h_bias'])
    gates = jax.nn.sigmoid(p[..., OFF_GATE:])
    merged = gates[..., :D_MODEL] * (y_m @ lp['m_w_out']) + gates[..., D_MODEL:] * (y_h @ lp['h_w_out'])
    return x + mods[2] * (merged @ lp['w_merge_out']), fin_f, fin_b


def ctx_final_states(x, mods, lp, init_f, init_b):
    h = modulated_norm(x, lp['norm1_g'], mods[0], mods[1])
    p = h @ lp['w_in'][:, :OFF_Z]
    xs, Bm, Cm, dt, a = mamba_inputs(p[..., :OFF_DT], p[..., OFF_DT:], lp['m_conv_w'], lp['m_conv_b'],
                                     lp['m_dt_bias'], lp['m_a_log'])
    _, fin_f, fin_b = bidir_ssd(xs, Bm, Cm, dt, a, init_f, init_b, False)
    return fin_f, fin_b


def ffn_sublayer(x, mods, lp):
    h = modulated_norm(x, lp['norm2_g'], mods[3], mods[4])
    g, u = jnp.split(h @ lp['ffn_w_gu'], 2, axis=-1)
    return x + mods[5] * ((jax.nn.silu(g) * u) @ lp['ffn_w_down'])


def setup_inputs(seed: int = 0) -> dict:
    key = jax.random.key(seed)
    ks = jax.random.split(key, 32)
    f32 = jnp.float32

    def nrm(k, shape, scale):
        return jax.random.normal(k, shape, f32) * scale

    dt0 = jnp.exp(jax.random.uniform(ks[8], (DEPTH, 2, M_HEADS), f32, math.log(M_DT_MIN), math.log(M_DT_MAX)))
    return {
        'x': nrm(ks[0], (BATCH, SEQ, D_MODEL), 1.0),
        'c': nrm(ks[1], (BATCH, D_MODEL), 1.0),
        'ctx': nrm(ks[2], (BATCH, CTX_LEN, D_MODEL), 1.0),
        'c_ctx': nrm(ks[3], (D_MODEL,), 1.0),
        'ada_w': nrm(ks[4], (DEPTH, D_MODEL, ADA_CHUNKS * D_MODEL), 0.5 * D_MODEL ** -0.5),
        'ada_b': nrm(ks[5], (DEPTH, ADA_CHUNKS * D_MODEL), 0.02),
        'norm1_g': 1.0 + nrm(ks[6], (DEPTH, D_MODEL), 0.05),
        'w_in': nrm(ks[7], (DEPTH, D_MODEL, IN_COLS), D_MODEL ** -0.5),
        'm_conv_w': nrm(ks[9], (DEPTH, M_CONV, M_XBC), M_CONV ** -0.5),
        'm_conv_b': nrm(ks[10], (DEPTH, M_XBC), 0.02),
        'm_dt_bias': dt0 + jnp.log(-jnp.expm1(-dt0)),
        'm_a_log': jnp.log(jax.random.uniform(ks[11], (DEPTH, 2, M_HEADS), f32, 1.0, 16.0)),
        'm_d': 1.0 + nrm(ks[12], (DEPTH, M_HEADS), 0.1),
        'm_norm_g': 1.0 + nrm(ks[13], (DEPTH, M_D_INNER), 0.05),
        'm_w_out': nrm(ks[14], (DEPTH, M_D_INNER, D_MODEL), M_D_INNER ** -0.5),
        'h_conv_w': nrm(ks[15], (DEPTH, H_SHORT, 3 * H_WIDTH), H_SHORT ** -0.5),
        'h_conv_b': nrm(ks[16], (DEPTH, 3 * H_WIDTH), 0.02),
        'hf_w1': nrm(ks[17], (DEPTH, H_EMB, H_HIDDEN), H_EMB ** -0.5),
        'hf_b1': nrm(ks[18], (DEPTH, H_HIDDEN), 0.02),
        'hf_w2': nrm(ks[19], (DEPTH, H_HIDDEN, H_HIDDEN), H_HIDDEN ** -0.5),
        'hf_b2': nrm(ks[20], (DEPTH, H_HIDDEN), 0.02),
        'hf_freq': 1.0 + nrm(ks[21], (DEPTH, H_HIDDEN), 0.1),
        'hf_w3': nrm(ks[22], (DEPTH, H_HIDDEN, 2 * H_WIDTH), H_HIDDEN ** -0.5),
        'h_bias': nrm(ks[23], (DEPTH, H_WIDTH), 0.1),
        'h_w_out': nrm(ks[24], (DEPTH, H_WIDTH, D_MODEL), H_WIDTH ** -0.5),
        'w_merge_out': nrm(ks[25], (DEPTH, D_MODEL, D_MODEL), D_MODEL ** -0.5),
        'norm2_g': 1.0 + nrm(ks[26], (DEPTH, D_MODEL), 0.05),
        'ffn_w_gu': nrm(ks[27], (DEPTH, D_MODEL, 2 * FFN_HIDDEN), D_MODEL ** -0.5),
        'ffn_w_down': nrm(ks[28], (DEPTH, FFN_HIDDEN, D_MODEL), FFN_HIDDEN ** -0.5),
        'final_g': 1.0 + nrm(ks[29], (D_MODEL,), 0.05),
    }


def reference(x, c, ctx, c_ctx, ada_w, ada_b, norm1_g, w_in, m_conv_w, m_conv_b, m_dt_bias, m_a_log, m_d,
              m_norm_g, m_w_out, h_conv_w, h_conv_b, hf_w1, hf_b1, hf_w2, hf_b2, hf_freq, hf_w3, h_bias,
              h_w_out, w_merge_out, norm2_g, ffn_w_gu, ffn_w_down, final_g):
    b, L, _ = x.shape
    rows = L // GRID_W
    x_l = x + grid_pos_embed(rows, D_MODEL, x.dtype)[None]
    x_c = ctx
    silu_c = jax.nn.silu(c)[:, None, :]
    silu_cc = jax.nn.silu(c_ctx)
    zero_state = jnp.zeros((b, M_GROUPS, M_HEADS // M_GROUPS, M_HEADDIM, M_STATE), jnp.float32)
    for i in range(DEPTH):
        lp = dict(w_in=w_in[i], norm1_g=norm1_g[i], m_conv_w=m_conv_w[i], m_conv_b=m_conv_b[i],
                  m_dt_bias=m_dt_bias[i], m_a_log=m_a_log[i], m_d=m_d[i], m_norm_g=m_norm_g[i],
                  m_w_out=m_w_out[i], h_conv_w=h_conv_w[i], h_conv_b=h_conv_b[i], hf_w1=hf_w1[i],
                  hf_b1=hf_b1[i], hf_w2=hf_w2[i], hf_b2=hf_b2[i], hf_freq=hf_freq[i], hf_w3=hf_w3[i],
                  h_bias=h_bias[i], h_w_out=h_w_out[i], w_merge_out=w_merge_out[i], norm2_g=norm2_g[i],
                  ffn_w_gu=ffn_w_gu[i], ffn_w_down=ffn_w_down[i])
        mods_l = jnp.split(silu_c @ ada_w[i] + ada_b[i], ADA_CHUNKS, axis=-1)
        mods_c = jnp.split(silu_cc @ ada_w[i] + ada_b[i], ADA_CHUNKS, axis=-1)
        if i < DEPTH - 1:
            x_c_mixed, fin_f, fin_b = mix_sublayer(x_c, mods_c, lp, zero_state, zero_state)
        else:
            fin_f, fin_b = ctx_final_states(x_c, mods_c, lp, zero_state, zero_state)
        x_l, _, _ = mix_sublayer(x_l, mods_l, lp, fin_f, fin_b)
        x_l = ffn_sublayer(x_l, mods_l, lp)
        if i < DEPTH - 1:
            x_c = ffn_sublayer(x_c_mixed, mods_c, lp)
    return rmsnorm(x_l, final_g)
```

```python
import functools
import math

import jax
import jax.numpy as jnp
from jax import lax
from jax.experimental import pallas as pl
from jax.experimental.pallas import tpu as pltpu

F32 = jnp.float32
BF16 = jnp.bfloat16
HIGHEST = lax.Precision.HIGHEST

RMS_EPS = 1e-6
GRID_W = 64
POS_BASE = 10000.0
ADA_CHUNKS = 6
M_HEADDIM = 64
M_GROUPS = 8
M_STATE = 128
M_CONV = 5
M_CHUNK = 128
H_SHORT = 3
H_EMB = 33
H_DECAY_TARGET = 1e-2
H_FAST_PCT = 0.3
H_SLOW_PCT = 1.5

LANES = 128
SUBLANES = 8
VMEM_LIMIT_BYTES = 56 * 1024 * 1024

COL_TILE = 512
FEAT_COLS = 40


def _cparams(*sem):
    return pltpu.CompilerParams(dimension_semantics=sem, vmem_limit_bytes=VMEM_LIMIT_BYTES)


def _silu(v):
    return v * jax.nn.sigmoid(v)


def _round_up(a, m):
    return -(-a // m) * m


def _mods_kernel(c_ref, w_ref, b_ref, o_ref):
    s = _silu(c_ref[...])
    o_ref[0] = jnp.dot(s, w_ref[0], preferred_element_type=F32, precision=HIGHEST) + b_ref[0]


def _mods(cvecs, ada_w, ada_b):
    depth, d, n = ada_w.shape
    tn = min(n, 1024)
    return pl.pallas_call(
        _mods_kernel,
        grid=(depth, n // tn),
        in_specs=[pl.BlockSpec((SUBLANES, d), lambda i, j: (0, 0)),
                  pl.BlockSpec((1, d, tn), lambda i, j: (i, 0, j)),
                  pl.BlockSpec((1, 1, tn), lambda i, j: (i, 0, j))],
        out_specs=pl.BlockSpec((1, SUBLANES, tn), lambda i, j: (i, 0, j)),
        out_shape=jax.ShapeDtypeStruct((depth, SUBLANES, n), F32),
        compiler_params=_cparams("parallel", "parallel"),
        name="mods",
    )(cvecs, ada_w, ada_b.reshape(depth, 1, n))


def _add_kernel(x_ref, p_ref, o_ref):
    o_ref[0] = x_ref[0] + p_ref[...]


def _add_pos(x, pos):
    b, l, d = x.shape
    tl = min(l, 1024)
    return pl.pallas_call(
        _add_kernel,
        grid=(l // tl, b),
        in_specs=[pl.BlockSpec((1, tl, d), lambda i, bb: (bb, i, 0)),
                  pl.BlockSpec((tl, d), lambda i, bb: (i, 0))],
        out_specs=pl.BlockSpec((1, tl, d), lambda i, bb: (bb, i, 0)),
        out_shape=jax.ShapeDtypeStruct(x.shape, F32),
        compiler_params=_cparams("parallel", "parallel"),
        name="add_pos",
    )(x, pos)


def _rmsnorm_kernel(x_ref, g_ref, o_ref):
    x = x_ref[0]
    ms = jnp.mean(x * x, axis=-1, keepdims=True)
    o_ref[0] = x * lax.rsqrt(ms + RMS_EPS) * g_ref[...]


def _final_norm(x, g):
    b, l, d = x.shape
    tl = min(l, 1024)
    return pl.pallas_call(
        _rmsnorm_kernel,
        grid=(b, l // tl),
        in_specs=[pl.BlockSpec((1, tl, d), lambda bb, i: (bb, i, 0)),
                  pl.BlockSpec((1, d), lambda bb, i: (0, 0))],
        out_specs=pl.BlockSpec((1, tl, d), lambda bb, i: (bb, i, 0)),
        out_shape=jax.ShapeDtypeStruct(x.shape, F32),
        compiler_params=_cparams("parallel", "parallel"),
        name="final_norm",
    )(x, g.reshape(1, d))


def _modnorm(x, g, shift, scale):
    ms = jnp.mean(x * x, axis=-1, keepdims=True)
    y = x * lax.rsqrt(ms + RMS_EPS) * g
    return y * (1.0 + scale) + shift


def _norm_mm_kernel(x_ref, g_ref, sh_ref, sc_ref, w_ref, o_ref, h_scr):
    @pl.when(pl.program_id(2) == 0)
    def _():
        h_scr[...] = _modnorm(x_ref[0], g_ref[...], sh_ref[0], sc_ref[0]).astype(BF16)

    o_ref[0] = jnp.dot(h_scr[...], w_ref[...], preferred_element_type=F32)


def _norm_mm(x, g, shift, scale, w, col0, ncols):
    b, l, d = x.shape
    tm = min(l, 1024)
    tn = COL_TILE
    j0 = col0 // tn
    return pl.pallas_call(
        _norm_mm_kernel,
        grid=(b, l // tm, ncols // tn),
        in_specs=[pl.BlockSpec((1, tm, d), lambda bb, i, j: (bb, i, 0)),
                  pl.BlockSpec((1, d), lambda bb, i, j: (0, 0)),
                  pl.BlockSpec((1, 1, d), lambda bb, i, j: (bb, 0, 0)),
                  pl.BlockSpec((1, 1, d), lambda bb, i, j: (bb, 0, 0)),
                  pl.BlockSpec((d, tn), lambda bb, i, j: (0, j + j0))],
        out_specs=pl.BlockSpec((1, tm, tn), lambda bb, i, j: (bb, i, j)),
        out_shape=jax.ShapeDtypeStruct((b, l, ncols), F32),
        scratch_shapes=[pltpu.VMEM((tm, d), BF16)],
        compiler_params=_cparams("parallel", "parallel", "arbitrary"),
        name="norm_mm",
    )(x, g.reshape(1, d), shift, scale, w)


def _conv_taps(prev_ref, main_ref, next_ref, w_ref, b_ref, ext_scr, ktaps):
    i = pl.program_id(2)
    nblk = pl.num_programs(2)
    tl = main_ref.shape[1]
    pad = ktaps // 2
    ext_scr[0:SUBLANES, :] = jnp.where(i > 0, prev_ref[0], 0.0)
    ext_scr[SUBLANES:SUBLANES + tl, :] = main_ref[0]
    ext_scr[SUBLANES + tl:, :] = jnp.where(i < nblk - 1, next_ref[0], 0.0)
    acc = None
    for j in range(ktaps):
        term = ext_scr[pl.ds(SUBLANES + j - pad, tl), :] * w_ref[j:j + 1, :]
        acc = term if acc is None else acc + term
    return acc + b_ref[...]


def _conv_specs(tl, cw, l, col_block0):
    rpb = tl // SUBLANES
    last = l // SUBLANES - 1
    return [
        pl.BlockSpec((1, SUBLANES, cw), lambda bb, c, i: (bb, jnp.maximum(i * rpb - 1, 0), c + col_block0)),
        pl.BlockSpec((1, tl, cw), lambda bb, c, i: (bb, i, c + col_block0)),
        pl.BlockSpec((1, SUBLANES, cw), lambda bb, c, i: (bb, jnp.minimum((i + 1) * rpb, last), c + col_block0)),
    ]


def _mconv_kernel(prev_ref, main_ref, next_ref, w_ref, b_ref, o_ref, ext_scr):
    o_ref[0] = _silu(_conv_taps(prev_ref, main_ref, next_ref, w_ref, b_ref, ext_scr, M_CONV))


def _mconv(p, col0, w, bias):
    b, l, _ = p.shape
    xbc = w.shape[1]
    tl = min(l, 512)
    cw = COL_TILE
    return pl.pallas_call(
        _mconv_kernel,
        grid=(b, xbc // cw, l // tl),
        in_specs=_conv_specs(tl, cw, l, col0 // cw) + [
            pl.BlockSpec((M_CONV, cw), lambda bb, c, i: (0, c)),
            pl.BlockSpec((1, cw), lambda bb, c, i: (0, c))],
        out_specs=pl.BlockSpec((1, tl, cw), lambda bb, c, i: (bb, i, c)),
        out_shape=jax.ShapeDtypeStruct((b, l, xbc), F32),
        scratch_shapes=[pltpu.VMEM((tl + 2 * SUBLANES, cw), F32)],
        compiler_params=_cparams("parallel", "parallel", "arbitrary"),
        name="mconv",
    )(p, p, p, w, bias.reshape(1, xbc))


def _hconv_kernel(p0, m0, n0, p1, m1, n1, p2, m2, n2, w0, b0, w1, b1, w2, b2, x0_ref, u_ref, e0, e1, e2):
    x0_ref[0] = _conv_taps(p0, m0, n0, w0, b0, e0, H_SHORT)
    x1 = _conv_taps(p1, m1, n1, w1, b1, e1, H_SHORT)
    v = _conv_taps(p2, m2, n2, w2, b2, e2, H_SHORT)
    u_ref[0] = x1 * v


def _hconv(p, col0, w, bias, d):
    b, l, _ = p.shape
    tl = min(l, 512)
    cw = min(d, COL_TILE)
    nb = d // cw
    specs, wspecs = [], []
    for s in range(3):
        specs += _conv_specs(tl, cw, l, col0 // cw + s * nb)
        wspecs += [pl.BlockSpec((H_SHORT, cw), functools.partial(lambda bb, c, i, s: (0, c + s * nb), s=s)),
                   pl.BlockSpec((1, cw), functools.partial(lambda bb, c, i, s: (0, c + s * nb), s=s))]
    bias2 = bias.reshape(1, 3 * d)
    out_spec = pl.BlockSpec((1, tl, cw), lambda bb, c, i: (bb, i, c))
    return pl.pallas_call(
        _hconv_kernel,
        grid=(b, nb, l // tl),
        in_specs=specs + wspecs,
        out_specs=[out_spec, out_spec],
        out_shape=[jax.ShapeDtypeStruct((b, l, d), F32)] * 2,
        scratch_shapes=[pltpu.VMEM((tl + 2 * SUBLANES, cw), F32)] * 3,
        compiler_params=_cparams("parallel", "parallel", "arbitrary"),
        name="hconv",
    )(p, p, p, p, p, p, p, p, p, w, bias2, w, bias2, w, bias2)


def _ssd_kernel(*refs, reverse, with_output, epilogue, heads):
    if epilogue:
        (xbc_ref, dt_ref, dtb_ref, alog_ref, init_ref, yf_ref, z_ref, dsk_ref, ng_ref,
         y_ref, fin_ref, s_scr) = refs
    elif with_output:
        xbc_ref, dt_ref, dtb_ref, alog_ref, init_ref, y_ref, fin_ref, s_scr = refs
    else:
        xbc_ref, dt_ref, dtb_ref, alog_ref, init_ref, fin_ref, s_scr = refs
    t = M_CHUNK
    n = M_STATE
    di = heads * M_HEADDIM
    gw = di // M_GROUPS
    rpg = heads // M_GROUPS
    lane0 = heads if reverse else 0
    edge = 0 if reverse else t - 1
    c = pl.program_id(1)

    @pl.when(c == 0)
    def _():
        s_scr[...] = init_ref[0]

    x = dt_ref[0] + dtb_ref[...]
    dt = jnp.maximum(x, 0.0) + jnp.log1p(jnp.exp(-jnp.abs(x)))
    dta = dt * (-jnp.exp(alog_ref[...]))
    row = lax.broadcasted_iota(jnp.int32, (t, t), 0)
    col = lax.broadcasted_iota(jnp.int32, (t, t), 1)
    mask = (row <= col) if reverse else (row >= col)
    acs = jnp.dot(mask.astype(F32), dta, preferred_element_type=F32, precision=HIGHEST)
    acs_t = acs.T
    dt_t = dt.T
    eacs = jnp.exp(acs)
    w_t = jnp.exp(acs_t[:, edge:edge + 1] - acs_t) * dt_t
    dec = jnp.exp(acs[edge:edge + 1, :])
    lane = lax.broadcasted_iota(jnp.int32, (1, LANES), 1)
    lo = lane < M_HEADDIM

    for g in range(M_GROUPS):
        bm = xbc_ref[0, :, di + g * n:di + (g + 1) * n]
        cm = xbc_ref[0, :, di + M_GROUPS * n + g * n:di + M_GROUPS * n + (g + 1) * n]
        bt = bm.T
        if with_output:
            cb = jnp.dot(cm.astype(BF16), bt.astype(BF16), preferred_element_type=F32)
        y_parts = []
        for q in range(rpg // 2):
            c0 = g * gw + q * LANES
            xp = xbc_ref[0, :, c0:c0 + LANES]
            sp = s_scr[g, :, q * LANES:(q + 1) * LANES]
            xlo = jnp.where(lo, xp, 0.0).astype(BF16)
            xhi = jnp.where(lo, 0.0, xp).astype(BF16)
            lhs, btw, decs = [], [], []
            for r in range(2):
                hl = lane0 + g * rpg + 2 * q + r
                arow = acs_t[hl:hl + 1, :]
                if with_output:
                    acol = jnp.broadcast_to(acs[:, hl:hl + 1], (t, t))
                    dk = jnp.exp(jnp.where(mask, acol - arow, -jnp.inf))
                    lhs.append((cb * dk * dt_t[hl:hl + 1, :]).astype(BF16))
                    lhs.append((cm * jnp.broadcast_to(eacs[:, hl:hl + 1], (t, n))).astype(BF16))
                btw.append((bt * w_t[hl:hl + 1, :]).astype(BF16))
                decs.append(jnp.broadcast_to(dec[:, hl:hl + 1], (1, LANES)))
            if with_output:
                slo = jnp.where(lo, sp, 0.0).astype(BF16)
                shi = jnp.where(lo, 0.0, sp).astype(BF16)
                y_parts.append(jnp.dot(jnp.concatenate(lhs, axis=1), jnp.concatenate([xlo, slo, xhi, shi], axis=0),
                                       preferred_element_type=F32))
            ds = jnp.dot(jnp.concatenate(btw, axis=1), jnp.concatenate([xlo, xhi], axis=0),
                         preferred_element_type=F32)
            s_scr[g, :, q * LANES:(q + 1) * LANES] = jnp.where(lo, decs[0], decs[1]) * sp + ds
        if with_output:
            yg = jnp.concatenate(y_parts, axis=1) if len(y_parts) > 1 else y_parts[0]
            sl = slice(g * gw, (g + 1) * gw)
            if epilogue:
                yg = yg + yf_ref[0, :, sl]
                v = (yg + xbc_ref[0, :, sl] * dsk_ref[:, sl]) * _silu(z_ref[0, :, sl])
                ms = jnp.mean(v * v, axis=-1, keepdims=True)
                y_ref[0, :, sl] = (v * lax.rsqrt(ms + RMS_EPS) * ng_ref[:, sl]).astype(y_ref.dtype)
            else:
                y_ref[0, :, sl] = yg

    @pl.when(c == pl.num_programs(1) - 1)
    def _():
        fin_ref[0] = s_scr[...]


def _ssd(xbc, p, dt_col0, dtb, alog, init, heads, reverse, with_output, extra=None):
    b, l, xw = xbc.shape
    t = M_CHUNK
    nc = l // t
    di = heads * M_HEADDIM
    gw = di // M_GROUPS
    epilogue = extra is not None
    cidx = (lambda c: nc - 1 - c) if reverse else (lambda c: c)
    state_spec = pl.BlockSpec((1, M_GROUPS, M_STATE, gw), lambda bb, c: (bb, 0, 0, 0))
    in_specs = [pl.BlockSpec((1, t, xw), lambda bb, c: (bb, cidx(c), 0)),
                pl.BlockSpec((1, t, LANES), lambda bb, c: (bb, cidx(c), dt_col0 // LANES)),
                pl.BlockSpec((1, LANES), lambda bb, c: (0, 0)),
                pl.BlockSpec((1, LANES), lambda bb, c: (0, 0)),
                state_spec]
    args = [xbc, p, dtb, alog, init]
    out_specs, out_shape = [], []
    if epilogue:
        yf, z_col0, dsk, ng = extra
        in_specs += [pl.BlockSpec((1, t, di), lambda bb, c: (bb, cidx(c), 0)),
                     pl.BlockSpec((1, t, di), lambda bb, c: (bb, cidx(c), z_col0 // di)),
                     pl.BlockSpec((1, di), lambda bb, c: (0, 0)),
                     pl.BlockSpec((1, di), lambda bb, c: (0, 0))]
        args += [yf, p, dsk, ng]
    if with_output:
        out_specs.append(pl.BlockSpec((1, t, di), lambda bb, c: (bb, cidx(c), 0)))
        out_shape.append(jax.ShapeDtypeStruct((b, l, di), BF16 if epilogue else F32))
    out_specs.append(state_spec)
    out_shape.append(jax.ShapeDtypeStruct((b, M_GROUPS, M_STATE, gw), F32))
    outs = pl.pallas_call(
        functools.partial(_ssd_kernel, reverse=reverse, with_output=with_output, epilogue=epilogue, heads=heads),
        grid=(b, nc),
        in_specs=in_specs,
        out_specs=out_specs,
        out_shape=out_shape,
        scratch_shapes=[pltpu.VMEM((M_GROUPS, M_STATE, gw), F32)],
        compiler_params=_cparams("parallel", "arbitrary"),
        name="ssd_bwd" if reverse else "ssd_fwd",
    )(*args)
    return (outs[0], outs[1]) if with_output else (None, outs[0])


def _filt_kernel(f_ref, w1_ref, b1_ref, w2_ref, b2_ref, fr_ref, w3_ref, dl_ref, o_ref, ss_ref):
    f = f_ref[...]
    fr = fr_ref[...]
    h = jnp.sin(fr * (jnp.dot(f, w1_ref[...], preferred_element_type=F32, precision=HIGHEST) + b1_ref[...]))
    h = jnp.sin(fr * (jnp.dot(h, w2_ref[...], preferred_element_type=F32, precision=HIGHEST) + b2_ref[...]))
    filt = jnp.dot(h, w3_ref[...], preferred_element_type=F32, precision=HIGHEST)
    tcol = f[:, H_EMB:H_EMB + 1]
    mcol = f[:, H_EMB + 1:H_EMB + 2]
    out = filt * jnp.exp(-tcol * dl_ref[...]) * mcol
    o_ref[...] = out

    @pl.when(pl.program_id(1) == 0)
    def _():
        ss_ref[...] = jnp.zeros_like(ss_ref)

    ss_ref[...] += jnp.sum(out * out, axis=0, keepdims=True)


def _filter_features(l):
    t = jnp.linspace(0.0, 1.0, l, dtype=F32)[:, None]
    ang = (2.0 * math.pi / l) * jnp.arange(l, dtype=F32)[:, None]
    nb = (H_EMB - 1) // 2
    bands = jnp.linspace(1e-4, nb - 1, nb, dtype=F32)[None, :]
    feats = jnp.concatenate([t, jnp.cos(bands * ang), -jnp.sin(bands * ang), t, jnp.ones_like(t)], axis=-1)
    rev = jnp.concatenate([feats[0:1] * 0.0, jnp.flip(feats[1:], axis=0)], axis=0)
    full = jnp.concatenate([feats, rev], axis=0)
    return jnp.pad(full, ((0, 0), (0, FEAT_COLS - full.shape[1])))


def _hyena_filter(l, w1, b1, w2, b2, freq, w3):
    hid = w2.shape[0]
    d = w3.shape[1] // 2
    feats = _filter_features(l)
    w1p = jnp.pad(w1, ((0, FEAT_COLS - w1.shape[0]), (0, 0)))
    deltas = jnp.abs(jnp.linspace(math.log(H_DECAY_TARGET) / H_SLOW_PCT, math.log(H_DECAY_TARGET) / H_FAST_PCT,
                                  d, dtype=F32))[None, :]
    tl = min(l, 512)
    cw = min(d, COL_TILE)
    nb = d // cw
    nrow = l // tl
    small = lambda shape: pl.BlockSpec(shape, lambda c, i: (0, 0))
    return pl.pallas_call(
        _filt_kernel,
        grid=(nb, 2 * nrow),
        in_specs=[pl.BlockSpec((tl, FEAT_COLS), lambda c, i: (i, 0)),
                  small((FEAT_COLS, hid)), small((1, hid)), small((hid, hid)), small((1, hid)), small((1, hid)),
                  pl.BlockSpec((hid, cw), lambda c, i: (0, c + (i // nrow) * nb)),
                  pl.BlockSpec((1, cw), lambda c, i: (0, c))],
        out_specs=[pl.BlockSpec((tl, cw), lambda c, i: (i, c)),
                   pl.BlockSpec((1, cw), lambda c, i: (0, c))],
        out_shape=[jax.ShapeDtypeStruct((2 * l, d), F32), jax.ShapeDtypeStruct((1, d), F32)],
        compiler_params=_cparams("parallel", "arbitrary"),
        name="hyena_filter",
    )(feats, w1p, b1.reshape(1, hid), w2, b2.reshape(1, hid), freq.reshape(1, hid), w3, deltas)


def _fft_split(n):
    n1 = 1 << ((n.bit_length() - 1 + 1) // 2)
    return n1, n // n1


def _dft_consts(n1, n2):
    n = n1 * n2
    h = n1 // 2
    k1 = jnp.arange(n1, dtype=jnp.int32)
    t1 = jnp.arange(n1, dtype=jnp.int32)
    t2 = jnp.arange(n2, dtype=jnp.int32)
    idx = (k1[None, :, None] * (n2 * t1[None, None, :] + t2[:, None, None])) % n
    ang = idx.astype(F32) * (2.0 * math.pi / n)
    cs, sn = jnp.cos(ang), jnp.sin(ang)
    ch, sh = cs[..., :h], sn[..., :h]
    t_data = jnp.concatenate([jnp.concatenate([ch, sh], -1), jnp.concatenate([-sh, ch], -1)], axis=1)
    t_filt = jnp.concatenate([cs, -sn], axis=1)
    ct, st = jnp.swapaxes(ch, 1, 2), jnp.swapaxes(sh, 1, 2)
    t_inv = jnp.concatenate([jnp.concatenate([ct, -st], -1), jnp.concatenate([st, ct], -1)], axis=1) / n
    k2 = jnp.arange(n2, dtype=jnp.int32)
    ang2 = ((k2[:, None] * k2[None, :]) % n2).astype(F32) * (2.0 * math.pi / n2)
    c2, s2 = jnp.cos(ang2), jnp.sin(ang2)
    m_fwd = jnp.concatenate([jnp.concatenate([c2, s2], -1), jnp.concatenate([-s2, c2], -1)], axis=0)
    m_inv = jnp.concatenate([jnp.concatenate([c2, -s2], -1), jnp.concatenate([s2, c2], -1)], axis=0)
    return dict(t_data=t_data.astype(BF16), t_filt=t_filt.astype(BF16), t_inv=t_inv.astype(BF16),
                m_fwd=m_fwd.astype(BF16), m_inv=m_inv.astype(BF16))


def _fft1_kernel(x_ref, t_ref, o_ref):
    a = x_ref.shape[1]
    n1 = o_ref.shape[2]
    for j in range(SUBLANES):
        x = jnp.concatenate([x_ref[0, s, :, j, :] for s in range(a)], axis=0).astype(BF16)
        o = jnp.dot(t_ref[j], x, preferred_element_type=F32)
        o_ref[0, 0, :, j, :] = o[:n1]
        o_ref[0, 1, :, j, :] = o[n1:]


def _fft_stage1(xv, tmat, n1, n2, d):
    p, a, r, _, _ = xv.shape
    cw = min(d, COL_TILE)
    return pl.pallas_call(
        _fft1_kernel,
        grid=(n2 // SUBLANES, d // cw, p),
        in_specs=[pl.BlockSpec((1, a, r, SUBLANES, cw), lambda j, c, pp: (pp, 0, 0, j, c)),
                  pl.BlockSpec((SUBLANES, 2 * n1, n1), lambda j, c, pp: (j, 0, 0))],
        out_specs=pl.BlockSpec((1, 2, n1, SUBLANES, cw), lambda j, c, pp: (pp, 0, 0, j, c)),
        out_shape=jax.ShapeDtypeStruct((p, 2, n1, n2, d), F32),
        compiler_params=_cparams("parallel", "parallel", "parallel"),
        name="fft_stage1",
    )(xv, tmat)


def _fft2_filter_kernel(a_ref, m_ref, ss_ref, o_ref):
    _, _, _, n2, cw = a_ref.shape
    a = a_ref[0].reshape(2 * n2, cw).astype(BF16)
    xk = jnp.dot(m_ref[...], a, preferred_element_type=F32) * lax.rsqrt(ss_ref[...] + RMS_EPS)
    o_ref[...] = xk.reshape(o_ref.shape)


def _fft_filter_stage2(av, m_fwd, ss, n1, n2, d):
    cw = min(d, 1024)
    return pl.pallas_call(
        _fft2_filter_kernel,
        grid=(n1, d // cw),
        in_specs=[pl.BlockSpec((1, 2, 1, n2, cw), lambda k, c: (0, 0, k, 0, c)),
                  pl.BlockSpec((2 * n2, 2 * n2), lambda k, c: (0, 0)),
                  pl.BlockSpec((1, cw), lambda k, c: (0, c))],
        out_specs=pl.BlockSpec((2, 1, n2, cw), lambda k, c: (0, k, 0, c)),
        out_shape=jax.ShapeDtypeStruct((2, n1, n2, d), F32),
        compiler_params=_cparams("parallel", "parallel"),
        name="fft_filter_stage2",
    )(av, m_fwd, ss)


def _fft2_kernel(a_ref, h_ref, mf_ref, mi_ref, o_ref):
    _, _, _, n2, cw = a_ref.shape
    a = a_ref[0].reshape(2 * n2, cw).astype(BF16)
    xk = jnp.dot(mf_ref[...], a, preferred_element_type=F32)
    xr, xi = xk[:n2], xk[n2:]
    hr, hi = h_ref[0, 0], h_ref[1, 0]
    z = jnp.concatenate([xr * hr - xi * hi, xr * hi + xi * hr], axis=0).astype(BF16)
    o_ref[0] = jnp.dot(mi_ref[...], z, preferred_element_type=F32).reshape(o_ref.shape[1:])


def _fft_stage2(av, hf, m_fwd, m_inv, n1, n2, d):
    p = av.shape[0]
    cw = min(d, 1024)
    return pl.pallas_call(
        _fft2_kernel,
        grid=(n1, d // cw, p),
        in_specs=[pl.BlockSpec((1, 2, 1, n2, cw), lambda k, c, pp: (pp, 0, k, 0, c)),
                  pl.BlockSpec((2, 1, n2, cw), lambda k, c, pp: (0, k, 0, c)),
                  pl.BlockSpec((2 * n2, 2 * n2), lambda k, c, pp: (0, 0)),
                  pl.BlockSpec((2 * n2, 2 * n2), lambda k, c, pp: (0, 0))],
        out_specs=pl.BlockSpec((1, 2, 1, n2, cw), lambda k, c, pp: (pp, 0, k, 0, c)),
        out_shape=jax.ShapeDtypeStruct(av.shape, F32),
        compiler_params=_cparams("parallel", "parallel", "parallel"),
        name="fft_stage2",
    )(av, hf, m_fwd, m_inv)


def _fft3_kernel(b_ref, t_ref, u_ref, x0_ref, bias_ref, o_ref):
    n1 = b_ref.shape[2]
    h = n1 // 2
    for j in range(SUBLANES):
        bv = jnp.concatenate([b_ref[0, 0, :, j, :], b_ref[0, 1, :, j, :]], axis=0).astype(BF16)
        y = jnp.dot(t_ref[j], bv, preferred_element_type=F32)
        for s in range(2):
            conv = y[s * h:(s + 1) * h] + u_ref[0, s, :, j, :] * bias_ref[...]
            o_ref[0, s, :, j, :] = (x0_ref[0, s, :, j, :] * conv).astype(o_ref.dtype)


def _fft_stage3(bv, t_inv, uv, x0v, bias, n1, n2, d):
    p = bv.shape[0]
    h = n1 // 2
    cw = min(d, COL_TILE)
    io_spec = pl.BlockSpec((1, 2, h, SUBLANES, cw), lambda j, c, pp: (pp, 0, 0, j, c))
    return pl.pallas_call(
        _fft3_kernel,
        grid=(n2 // SUBLANES, d // cw, p),
        in_specs=[pl.BlockSpec((1, 2, n1, SUBLANES, cw), lambda j, c, pp: (pp, 0, 0, j, c)),
                  pl.BlockSpec((SUBLANES, n1, 2 * n1), lambda j, c, pp: (j, 0, 0)),
                  io_spec, io_spec,
                  pl.BlockSpec((1, cw), lambda j, c, pp: (0, c))],
        out_specs=io_spec,
        out_shape=jax.ShapeDtypeStruct(uv.shape, BF16),
        compiler_params=_cparams("parallel", "parallel", "parallel"),
        name="fft_stage3",
    )(bv, t_inv, uv, x0v, bias)


def _filter_spectrum(l, consts, lp):
    n1, n2 = _fft_split(2 * l)
    d = lp['h_bias'].shape[0]
    full, ss = _hyena_filter(l, lp['hf_w1'], lp['hf_b1'], lp['hf_w2'], lp['hf_b2'], lp['hf_freq'], lp['hf_w3'])
    a = _fft_stage1(full.reshape(1, 1, n1, n2, d), consts['t_filt'], n1, n2, d)
    return _fft_filter_stage2(a, consts['m_fwd'], ss, n1, n2, d)


def _long_conv(u, x0, hf, consts, bias):
    b, l, d = u.shape
    n1, n2 = _fft_split(2 * l)
    p = b // 2
    h = n1 // 2
    uv = u.reshape(p, 2, h, n2, d)
    a = _fft_stage1(uv, consts['t_data'], n1, n2, d)
    bv = _fft_stage2(a, hf, consts['m_fwd'], consts['m_inv'], n1, n2, d)
    y = _fft_stage3(bv, consts['t_inv'], uv, x0.reshape(p, 2, h, n2, d), bias.reshape(1, d), n1, n2, d)
    return y.reshape(b, l, d)


def _merge_kernel(ym_ref, yh_ref, ga_ref, gb_ref, wm_ref, wh_ref, o_ref):
    a = jnp.dot(ym_ref[0], wm_ref[...], preferred_element_type=F32)
    bb = jnp.dot(yh_ref[0], wh_ref[...], preferred_element_type=F32)
    o_ref[0] = (jax.nn.sigmoid(ga_ref[0]) * a + jax.nn.sigmoid(gb_ref[0]) * bb).astype(o_ref.dtype)


def _merge(ym, yh, p, gate_col0, wm, wh):
    b, l, di = ym.shape
    d = yh.shape[2]
    tm = min(l, 512)
    tn = min(d, COL_TILE)
    g0 = gate_col0 // tn
    nb = d // tn
    return pl.pallas_call(
        _merge_kernel,
        grid=(b, l // tm, nb),
        in_specs=[pl.BlockSpec((1, tm, di), lambda bb, i, j: (bb, i, 0)),
                  pl.BlockSpec((1, tm, d), lambda bb, i, j: (bb, i, 0)),
                  pl.BlockSpec((1, tm, tn), lambda bb, i, j: (bb, i, g0 + j)),
                  pl.BlockSpec((1, tm, tn), lambda bb, i, j: (bb, i, g0 + nb + j)),
                  pl.BlockSpec((di, tn), lambda bb, i, j: (0, j)),
                  pl.BlockSpec((d, tn), lambda bb, i, j: (0, j))],
        out_specs=pl.BlockSpec((1, tm, tn), lambda bb, i, j: (bb, i, j)),
        out_shape=jax.ShapeDtypeStruct((b, l, d), BF16),
        compiler_params=_cparams("parallel", "parallel", "arbitrary"),
        name="merge",
    )(ym, yh, p, p, wm, wh)


def _resid_mm_kernel(a_ref, w_ref, x_ref, g_ref, o_ref):
    o_ref[0] = x_ref[0] + g_ref[0] * jnp.dot(a_ref[0], w_ref[...], preferred_element_type=F32)


def _resid_mm(a, w, x, gate):
    b, l, k = a.shape
    d = w.shape[1]
    tm = min(l, 1024)
    tn = min(d, COL_TILE)
    return pl.pallas_call(
        _resid_mm_kernel,
        grid=(b, l // tm, d // tn),
        in_specs=[pl.BlockSpec((1, tm, k), lambda bb, i, j: (bb, i, 0)),
                  pl.BlockSpec((k, tn), lambda bb, i, j: (0, j)),
                  pl.BlockSpec((1, tm, tn), lambda bb, i, j: (bb, i, j)),
                  pl.BlockSpec((1, 1, tn), lambda bb, i, j: (bb, 0, j))],
        out_specs=pl.BlockSpec((1, tm, tn), lambda bb, i, j: (bb, i, j)),
        out_shape=jax.ShapeDtypeStruct((b, l, d), F32),
        compiler_params=_cparams("parallel", "parallel", "arbitrary"),
        name="resid_mm",
    )(a, w, x, gate)


def _ffn_kernel(x_ref, g_ref, sh_ref, sc_ref, gt_ref, wg_ref, wu_ref, wd_ref, o_ref, h_scr):
    j = pl.program_id(2)

    @pl.when(j == 0)
    def _():
        h_scr[...] = _modnorm(x_ref[0], g_ref[...], sh_ref[0], sc_ref[0]).astype(BF16)

    h = h_scr[...]
    gg = jnp.dot(h, wg_ref[...], preferred_element_type=F32)
    uu = jnp.dot(h, wu_ref[...], preferred_element_type=F32)
    part = jnp.dot((_silu(gg) * uu).astype(BF16), wd_ref[...], preferred_element_type=F32)

    @pl.when(j == 0)
    def _():
        o_ref[0] = part

    @pl.when(j > 0)
    def _():
        o_ref[0] += part

    @pl.when(j == pl.num_programs(2) - 1)
    def _():
        o_ref[0] = x_ref[0] + gt_ref[0] * o_ref[0]


def _ffn(x, g, shift, scale, gate, w_gu, w_down):
    b, l, d = x.shape
    f = w_down.shape[0]
    tm = min(l, 512)
    tf = 512 if f % 512 == 0 else 256
    nf = f // tf
    vec = pl.BlockSpec((1, 1, d), lambda bb, i, j: (bb, 0, 0))
    return pl.pallas_call(
        _ffn_kernel,
        grid=(b, l // tm, nf),
        in_specs=[pl.BlockSpec((1, tm, d), lambda bb, i, j: (bb, i, 0)),
                  pl.BlockSpec((1, d), lambda bb, i, j: (0, 0)),
                  vec, vec, vec,
                  pl.BlockSpec((d, tf), lambda bb, i, j: (0, j)),
                  pl.BlockSpec((d, tf), lambda bb, i, j: (0, j + nf)),
                  pl.BlockSpec((tf, d), lambda bb, i, j: (j, 0))],
        out_specs=pl.BlockSpec((1, tm, d), lambda bb, i, j: (bb, i, 0)),
        out_shape=jax.ShapeDtypeStruct((b, l, d), F32),
        scratch_shapes=[pltpu.VMEM((tm, d), BF16)],
        compiler_params=_cparams("parallel", "parallel", "arbitrary"),
        name="ffn",
    )(x, g.reshape(1, d), shift, scale, gate, w_gu, w_gu, w_down)


def _grid_pos_embed(rows, d):
    r, col = jnp.meshgrid(jnp.arange(rows), jnp.arange(GRID_W), indexing='ij')
    quarter = d // 4
    omega = 1.0 / (POS_BASE ** (jnp.arange(quarter, dtype=F32) / quarter))

    def axis_embed(pos):
        ang = pos.reshape(-1)[:, None].astype(F32) * omega[None, :]
        return jnp.concatenate([jnp.sin(ang), jnp.cos(ang)], axis=-1)

    return jnp.concatenate([axis_embed(r), axis_embed(col)], axis=-1).astype(F32)


class _Cols:
    def __init__(self, d):
        self.d = d
        self.di = 2 * d
        self.heads = self.di // M_HEADDIM
        self.xbc = self.di + 2 * M_GROUPS * M_STATE
        self.z0 = 0
        self.gate0 = self.z0 + self.di
        self.hy0 = self.gate0 + 2 * d
        self.xbc0 = self.hy0 + 3 * d
        self.dt0 = self.xbc0 + self.xbc
        self.total = self.dt0 + COL_TILE
        assert 2 * self.heads <= LANES and self.heads % (2 * M_GROUPS) == 0
        for off in (self.gate0, self.hy0, self.xbc0, self.dt0):
            assert off % COL_TILE == 0


def _pack_w_in(w_in, cols):
    o_dt = cols.xbc
    o_z = o_dt + 2 * cols.heads
    o_hy = o_z + cols.di
    o_gate = o_hy + 3 * cols.d
    parts = [w_in[:, o_z:o_hy], w_in[:, o_gate:], w_in[:, o_hy:o_gate], w_in[:, :o_dt], w_in[:, o_dt:o_z]]
    w = jnp.concatenate(parts, axis=1)
    return jnp.pad(w, ((0, 0), (0, cols.total - w.shape[1]))).astype(BF16)


def _pad_lanes(v):
    v = v.reshape(1, -1)
    return jnp.pad(v, ((0, 0), (0, LANES - v.shape[1])))


def _mix_sublayer(x, mods, lp, cols, hf, consts, init_f, init_b):
    p = _norm_mm(x, lp['norm1_g'], mods[0], mods[1], lp['w_in'], 0, cols.total)
    xbc = _mconv(p, cols.xbc0, lp['m_conv_w'], lp['m_conv_b'])
    y_f, fin_f = _ssd(xbc, p, cols.dt0, lp['dtb'], lp['alog'], init_f, cols.heads, False, True)
    ym, fin_b = _ssd(xbc, p, cols.dt0, lp['dtb'], lp['alog'], init_b, cols.heads, True, True,
                     extra=(y_f, cols.z0, lp['dskip'], lp['m_norm_g']))
    x0, u = _hconv(p, cols.hy0, lp['h_conv_w'], lp['h_conv_b'], cols.d)
    yh = _long_conv(u, x0, hf, consts, lp['h_bias'])
    merged = _merge(ym, yh, p, cols.gate0, lp['m_w_out'], lp['h_w_out'])
    return _resid_mm(merged, lp['w_merge_out'], x, mods[2]), fin_f, fin_b


def _ctx_final_states(x, mods, lp, cols, init_f, init_b):
    ncols = cols.total - cols.xbc0
    p = _norm_mm(x, lp['norm1_g'], mods[0], mods[1], lp['w_in'], cols.xbc0, ncols)
    xbc = _mconv(p, 0, lp['m_conv_w'], lp['m_conv_b'])
    _, fin_f = _ssd(xbc, p, cols.xbc, lp['dtb'], lp['alog'], init_f, cols.heads, False, False)
    _, fin_b = _ssd(xbc, p, cols.xbc, lp['dtb'], lp['alog'], init_b, cols.heads, True, False)
    return fin_f, fin_b


def kernel(x, c, ctx, c_ctx, ada_w, ada_b, norm1_g, w_in, m_conv_w, m_conv_b, m_dt_bias, m_a_log, m_d, m_norm_g,
           m_w_out, h_conv_w, h_conv_b, hf_w1, hf_b1, hf_w2, hf_b2, hf_freq, hf_w3, h_bias, h_w_out, w_merge_out,
           norm2_g, ffn_w_gu, ffn_w_down, final_g):
    b, l, d = x.shape
    lc = ctx.shape[1]
    depth = ada_w.shape[0]
    cols = _Cols(d)
    assert b % 2 == 0 and b + 1 <= SUBLANES and l % M_CHUNK == 0 and lc % M_CHUNK == 0

    cvecs = jnp.concatenate([c, c_ctx[None], jnp.zeros((SUBLANES - b - 1, d), F32)], axis=0)
    mods_all = _mods(cvecs, ada_w, ada_b)
    consts_l = _dft_consts(*_fft_split(2 * l))
    consts_c = _dft_consts(*_fft_split(2 * lc))

    x_l = _add_pos(x, _grid_pos_embed(l // GRID_W, d))
    x_c = ctx
    zero_state = jnp.zeros((b, M_GROUPS, M_STATE, cols.di // M_GROUPS), F32)
    for i in range(depth):
        lp = dict(norm1_g=norm1_g[i], w_in=_pack_w_in(w_in[i], cols), m_conv_w=m_conv_w[i], m_conv_b=m_conv_b[i],
                  dtb=_pad_lanes(m_dt_bias[i]), alog=_pad_lanes(m_a_log[i]),
                  dskip=jnp.repeat(m_d[i], M_HEADDIM).reshape(1, cols.di), m_norm_g=m_norm_g[i].reshape(1, cols.di),
                  m_w_out=m_w_out[i].astype(BF16), h_conv_w=h_conv_w[i], h_conv_b=h_conv_b[i],
                  hf_w1=hf_w1[i], hf_b1=hf_b1[i], hf_w2=hf_w2[i], hf_b2=hf_b2[i], hf_freq=hf_freq[i], hf_w3=hf_w3[i],
                  h_bias=h_bias[i], h_w_out=h_w_out[i].astype(BF16), w_merge_out=w_merge_out[i].astype(BF16),
                  norm2_g=norm2_g[i], ffn_w_gu=ffn_w_gu[i].astype(BF16), ffn_w_down=ffn_w_down[i].astype(BF16))
        m = mods_all[i].reshape(SUBLANES, ADA_CHUNKS, d)
        mods_l = [m[:b, k][:, None, :] for k in range(ADA_CHUNKS)]
        mods_c = [jnp.broadcast_to(m[b, k][None, None, :], (b, 1, d)) for k in range(ADA_CHUNKS)]
        if i < depth - 1:
            hf_c = _filter_spectrum(lc, consts_c, lp)
            x_c_mixed, fin_f, fin_b = _mix_sublayer(x_c, mods_c, lp, cols, hf_c, consts_c, zero_state, zero_state)
        else:
            fin_f, fin_b = _ctx_final_states(x_c, mods_c, lp, cols, zero_state, zero_state)
        hf_l = _filter_spectrum(l, consts_l, lp)
        x_l, _, _ = _mix_sublayer(x_l, mods_l, lp, cols, hf_l, consts_l, fin_f, fin_b)
        x_l = _ffn(x_l, lp['norm2_g'], mods_l[3], mods_l[4], mods_l[5], lp['ffn_w_gu'], lp['ffn_w_down'])
        if i < depth - 1:
            x_c = _ffn(x_c_mixed, lp['norm2_g'], mods_c[3], mods_c[4], mods_c[5], lp['ffn_w_gu'], lp['ffn_w_down'])
    return _final_norm(x_l, final_g)
```

```python
import functools
import math

import jax
import jax.numpy as jnp
from jax import lax
from jax.experimental import pallas as pl
from jax.experimental.pallas import tpu as pltpu

F32 = jnp.float32
BF16 = jnp.bfloat16
HIGHEST = lax.Precision.HIGHEST

RMS_EPS = 1e-6
GRID_W = 64
POS_BASE = 10000.0
ADA_CHUNKS = 6
M_HEADDIM = 64
M_GROUPS = 8
M_STATE = 128
M_CONV = 5
M_CHUNK = 128
H_SHORT = 3
H_EMB = 33
H_DECAY_TARGET = 1e-2
H_FAST_PCT = 0.3
H_SLOW_PCT = 1.5

LANES = 128
SUBLANES = 8
HALO = 16
VMEM_LIMIT_BYTES = 56 * 1024 * 1024

COL_TILE = 512
FEAT_COLS = 40


def _cparams(*sem):
    return pltpu.CompilerParams(dimension_semantics=sem, vmem_limit_bytes=VMEM_LIMIT_BYTES)


def _silu(v):
    return v * jax.nn.sigmoid(v)


def _round_up(a, m):
    return -(-a // m) * m


def _mods_kernel(c_ref, w_ref, b_ref, o_ref):
    s = _silu(c_ref[...])
    o_ref[0] = jnp.dot(s, w_ref[0], preferred_element_type=F32, precision=HIGHEST) + b_ref[0]


def _mods(cvecs, ada_w, ada_b):
    depth, d, n = ada_w.shape
    tn = min(n, 1024)
    return pl.pallas_call(
        _mods_kernel,
        grid=(depth, n // tn),
        in_specs=[pl.BlockSpec((SUBLANES, d), lambda i, j: (0, 0)),
                  pl.BlockSpec((1, d, tn), lambda i, j: (i, 0, j)),
                  pl.BlockSpec((1, 1, tn), lambda i, j: (i, 0, j))],
        out_specs=pl.BlockSpec((1, SUBLANES, tn), lambda i, j: (i, 0, j)),
        out_shape=jax.ShapeDtypeStruct((depth, SUBLANES, n), F32),
        compiler_params=_cparams("parallel", "parallel"),
        name="mods",
    )(cvecs, ada_w, ada_b.reshape(depth, 1, n))


def _add_kernel(x_ref, p_ref, o_ref):
    o_ref[0] = x_ref[0] + p_ref[...]


def _add_pos(x, pos):
    b, l, d = x.shape
    tl = min(l, 1024)
    return pl.pallas_call(
        _add_kernel,
        grid=(l // tl, b),
        in_specs=[pl.BlockSpec((1, tl, d), lambda i, bb: (bb, i, 0)),
                  pl.BlockSpec((tl, d), lambda i, bb: (i, 0))],
        out_specs=pl.BlockSpec((1, tl, d), lambda i, bb: (bb, i, 0)),
        out_shape=jax.ShapeDtypeStruct(x.shape, F32),
        compiler_params=_cparams("parallel", "parallel"),
        name="add_pos",
    )(x, pos)


def _modnorm(x, g, shift, scale):
    ms = jnp.mean(x * x, axis=-1, keepdims=True)
    y = x * lax.rsqrt(ms + RMS_EPS) * g
    return y * (1.0 + scale) + shift


def _prenorm_kernel(x_ref, g_ref, sh_ref, sc_ref, o_ref):
    o_ref[0] = _modnorm(x_ref[0], g_ref[...], sh_ref[0], sc_ref[0]).astype(o_ref.dtype)


def _prenorm(x, g, shift, scale):
    b, l, d = x.shape
    tl = min(l, 1024)
    vec = pl.BlockSpec((1, 1, d), lambda bb, i: (bb, 0, 0))
    return pl.pallas_call(
        _prenorm_kernel,
        grid=(b, l // tl),
        in_specs=[pl.BlockSpec((1, tl, d), lambda bb, i: (bb, i, 0)),
                  pl.BlockSpec((1, d), lambda bb, i: (0, 0)), vec, vec],
        out_specs=pl.BlockSpec((1, tl, d), lambda bb, i: (bb, i, 0)),
        out_shape=jax.ShapeDtypeStruct((b, l, d), BF16),
        compiler_params=_cparams("parallel", "parallel"),
        name="prenorm",
    )(x, g.reshape(1, d), shift, scale)


def _inproj_kernel(h_ref, w_ref, o_ref, dt_ref):
    j = pl.program_id(2)
    last = pl.num_programs(2) - 1
    acc = jnp.dot(h_ref[0], w_ref[...], preferred_element_type=F32)

    @pl.when(j < last)
    def _():
        o_ref[0] = acc.astype(o_ref.dtype)

    @pl.when(j == last)
    def _():
        dt_ref[0] = acc


def _inproj(h, w, col0, ncols):
    b, l, d = h.shape
    tm = min(l, 2048)
    tn = COL_TILE
    j0 = col0 // tn
    nj = ncols // tn
    return pl.pallas_call(
        _inproj_kernel,
        grid=(b, l // tm, nj),
        in_specs=[pl.BlockSpec((1, tm, d), lambda bb, i, j: (bb, i, 0)),
                  pl.BlockSpec((d, tn), lambda bb, i, j: (0, j + j0))],
        out_specs=[pl.BlockSpec((1, tm, tn), lambda bb, i, j: (bb, i, jnp.minimum(j, nj - 2))),
                   pl.BlockSpec((1, tm, tn), lambda bb, i, j: (bb, i, 0))],
        out_shape=[jax.ShapeDtypeStruct((b, l, ncols - tn), BF16), jax.ShapeDtypeStruct((b, l, tn), F32)],
        compiler_params=_cparams("parallel", "parallel", "arbitrary"),
        name="inproj",
    )(h, w)


def _conv_taps(prev_ref, main_ref, next_ref, w_ref, b_ref, ext_scr, ktaps):
    i = pl.program_id(2)
    nblk = pl.num_programs(2)
    tl = main_ref.shape[1]
    pad = ktaps // 2
    ext_scr[0:HALO, :] = jnp.where(i > 0, prev_ref[0].astype(F32), 0.0)
    ext_scr[HALO:HALO + tl, :] = main_ref[0].astype(F32)
    ext_scr[HALO + tl:, :] = jnp.where(i < nblk - 1, next_ref[0].astype(F32), 0.0)
    acc = None
    for j in range(ktaps):
        term = ext_scr[pl.ds(HALO + j - pad, tl), :] * w_ref[j:j + 1, :]
        acc = term if acc is None else acc + term
    return acc + b_ref[...]


def _conv_specs(tl, cw, l, col_block0):
    rpb = tl // HALO
    last = l // HALO - 1
    return [
        pl.BlockSpec((1, HALO, cw), lambda bb, c, i: (bb, jnp.maximum(i * rpb - 1, 0), c + col_block0)),
        pl.BlockSpec((1, tl, cw), lambda bb, c, i: (bb, i, c + col_block0)),
        pl.BlockSpec((1, HALO, cw), lambda bb, c, i: (bb, jnp.minimum((i + 1) * rpb, last), c + col_block0)),
    ]


def _mconv_kernel(prev_ref, main_ref, next_ref, w_ref, b_ref, o_ref, ext_scr):
    o_ref[0] = _silu(_conv_taps(prev_ref, main_ref, next_ref, w_ref, b_ref, ext_scr, M_CONV)).astype(o_ref.dtype)


def _mconv(p, col0, w, bias):
    b, l, _ = p.shape
    xbc = w.shape[1]
    tl = min(l, 512)
    cw = COL_TILE
    return pl.pallas_call(
        _mconv_kernel,
        grid=(b, xbc // cw, l // tl),
        in_specs=_conv_specs(tl, cw, l, col0 // cw) + [
            pl.BlockSpec((M_CONV, cw), lambda bb, c, i: (0, c)),
            pl.BlockSpec((1, cw), lambda bb, c, i: (0, c))],
        out_specs=pl.BlockSpec((1, tl, cw), lambda bb, c, i: (bb, i, c)),
        out_shape=jax.ShapeDtypeStruct((b, l, xbc), BF16),
        scratch_shapes=[pltpu.VMEM((tl + 2 * HALO, cw), F32)],
        compiler_params=_cparams("parallel", "parallel", "arbitrary"),
        name="mconv",
    )(p, p, p, w, bias.reshape(1, xbc))


def _hconv_kernel(p0, m0, n0, p1, m1, n1, p2, m2, n2, w0, b0, w1, b1, w2, b2, x0_ref, u_ref, e0, e1, e2):
    x0_ref[0] = _conv_taps(p0, m0, n0, w0, b0, e0, H_SHORT).astype(x0_ref.dtype)
    x1 = _conv_taps(p1, m1, n1, w1, b1, e1, H_SHORT)
    v = _conv_taps(p2, m2, n2, w2, b2, e2, H_SHORT)
    u_ref[0] = (x1 * v).astype(u_ref.dtype)


def _hconv(p, col0, w, bias, d):
    b, l, _ = p.shape
    tl = min(l, 512)
    cw = min(d, COL_TILE)
    nb = d // cw
    specs, wspecs = [], []
    for s in range(3):
        specs += _conv_specs(tl, cw, l, col0 // cw + s * nb)
        wspecs += [pl.BlockSpec((H_SHORT, cw), functools.partial(lambda bb, c, i, s: (0, c + s * nb), s=s)),
                   pl.BlockSpec((1, cw), functools.partial(lambda bb, c, i, s: (0, c + s * nb), s=s))]
    bias2 = bias.reshape(1, 3 * d)
    out_spec = pl.BlockSpec((1, tl, cw), lambda bb, c, i: (bb, i, c))
    return pl.pallas_call(
        _hconv_kernel,
        grid=(b, nb, l // tl),
        in_specs=specs + wspecs,
        out_specs=[out_spec, out_spec],
        out_shape=[jax.ShapeDtypeStruct((b, l, d), BF16)] * 2,
        scratch_shapes=[pltpu.VMEM((tl + 2 * HALO, cw), F32)] * 3,
        compiler_params=_cparams("parallel", "parallel", "arbitrary"),
        name="hconv",
    )(p, p, p, p, p, p, p, p, p, w, bias2, w, bias2, w, bias2)


def _ssd_kernel(*refs, reverse, with_output, epilogue, heads):
    if epilogue:
        (xbc_ref, dt_ref, dtb_ref, alog_ref, init_ref, yf_ref, z_ref, dsk_ref, ng_ref,
         y_ref, fin_ref, s_scr) = refs
    elif with_output:
        xbc_ref, dt_ref, dtb_ref, alog_ref, init_ref, y_ref, fin_ref, s_scr = refs
    else:
        xbc_ref, dt_ref, dtb_ref, alog_ref, init_ref, fin_ref, s_scr = refs
    t = M_CHUNK
    n = M_STATE
    di = heads * M_HEADDIM
    gw = di // M_GROUPS
    rpg = heads // M_GROUPS
    lane0 = heads if reverse else 0
    edge = 0 if reverse else t - 1
    c = pl.program_id(1)

    @pl.when(c == 0)
    def _():
        s_scr[...] = init_ref[0]

    x = dt_ref[0] + dtb_ref[...]
    dt = jnp.maximum(x, 0.0) + jnp.log1p(jnp.exp(-jnp.abs(x)))
    dta = dt * (-jnp.exp(alog_ref[...]))
    row = lax.broadcasted_iota(jnp.int32, (t, t), 0)
    col = lax.broadcasted_iota(jnp.int32, (t, t), 1)
    mask = (row <= col) if reverse else (row >= col)
    acs = jnp.dot(mask.astype(F32), dta, preferred_element_type=F32, precision=HIGHEST)
    acs_t = acs.T
    dt_t = dt.T
    eacs = jnp.exp(acs)
    arow_dt = acs_t - jnp.log(dt_t)
    w_t = jnp.exp(acs_t[:, edge:edge + 1] - acs_t) * dt_t
    dec = jnp.exp(acs[edge:edge + 1, :])
    lane = lax.broadcasted_iota(jnp.int32, (1, LANES), 1)
    lo = lane < M_HEADDIM
    mlo = jnp.where(lo, 1.0, 0.0).astype(BF16)
    mhi = jnp.where(lo, 0.0, 1.0).astype(BF16)

    for g in range(M_GROUPS):
        bm = xbc_ref[0, :, di + g * n:di + (g + 1) * n].astype(F32)
        cm = xbc_ref[0, :, di + M_GROUPS * n + g * n:di + M_GROUPS * n + (g + 1) * n].astype(F32)
        bt = bm.T
        if with_output:
            cb = jnp.dot(cm.astype(BF16), bt.astype(BF16), preferred_element_type=F32)
        y_parts = []
        for q in range(rpg // 2):
            c0 = g * gw + q * LANES
            xp = xbc_ref[0, :, c0:c0 + LANES]
            sp = s_scr[g, :, q * LANES:(q + 1) * LANES]
            xlo = xp * mlo
            xhi = xp * mhi
            lhs, btw, decs = [], [], []
            for r in range(2):
                hl = lane0 + g * rpg + 2 * q + r
                if with_output:
                    acol = jnp.broadcast_to(acs[:, hl:hl + 1], (t, t))
                    dk = jnp.exp(jnp.where(mask, acol - arow_dt[hl:hl + 1, :], -jnp.inf))
                    lhs.append((cb * dk).astype(BF16))
                    lhs.append((cm * jnp.broadcast_to(eacs[:, hl:hl + 1], (t, n))).astype(BF16))
                btw.append((bt * w_t[hl:hl + 1, :]).astype(BF16))
                decs.append(jnp.broadcast_to(dec[:, hl:hl + 1], (1, LANES)))
            if with_output:
                slo = jnp.where(lo, sp, 0.0).astype(BF16)
                shi = jnp.where(lo, 0.0, sp).astype(BF16)
                y_parts.append(jnp.dot(jnp.concatenate(lhs, axis=1), jnp.concatenate([xlo, slo, xhi, shi], axis=0),
                                       preferred_element_type=F32))
            ds = jnp.dot(jnp.concatenate(btw, axis=1), jnp.concatenate([xlo, xhi], axis=0),
                         preferred_element_type=F32)
            s_scr[g, :, q * LANES:(q + 1) * LANES] = jnp.where(lo, decs[0], decs[1]) * sp + ds
        if with_output:
            yg = jnp.concatenate(y_parts, axis=1) if len(y_parts) > 1 else y_parts[0]
            sl = slice(g * gw, (g + 1) * gw)
            if epilogue:
                yg = yg + yf_ref[0, :, sl].astype(F32)
                v = (yg + xbc_ref[0, :, sl].astype(F32) * dsk_ref[:, sl]) * _silu(z_ref[0, :, sl].astype(F32))
                ms = jnp.mean(v * v, axis=-1, keepdims=True)
                y_ref[0, :, sl] = (v * lax.rsqrt(ms + RMS_EPS) * ng_ref[:, sl]).astype(y_ref.dtype)
            else:
                y_ref[0, :, sl] = yg.astype(y_ref.dtype)

    @pl.when(c == pl.num_programs(1) - 1)
    def _():
        fin_ref[0] = s_scr[...]


def _ssd(xbc, dtp, dtb, alog, init, heads, reverse, with_output, extra=None):
    b, l, xw = xbc.shape
    t = M_CHUNK
    nc = l // t
    di = heads * M_HEADDIM
    gw = di // M_GROUPS
    epilogue = extra is not None
    cidx = (lambda c: nc - 1 - c) if reverse else (lambda c: c)
    state_spec = pl.BlockSpec((1, M_GROUPS, M_STATE, gw), lambda bb, c: (bb, 0, 0, 0))
    in_specs = [pl.BlockSpec((1, t, xw), lambda bb, c: (bb, cidx(c), 0)),
                pl.BlockSpec((1, t, LANES), lambda bb, c: (bb, cidx(c), 0)),
                pl.BlockSpec((1, LANES), lambda bb, c: (0, 0)),
                pl.BlockSpec((1, LANES), lambda bb, c: (0, 0)),
                state_spec]
    args = [xbc, dtp, dtb, alog, init]
    out_specs, out_shape = [], []
    if epilogue:
        yf, p, z_col0, dsk, ng = extra
        in_specs += [pl.BlockSpec((1, t, di), lambda bb, c: (bb, cidx(c), 0)),
                     pl.BlockSpec((1, t, di), lambda bb, c: (bb, cidx(c), z_col0 // di)),
                     pl.BlockSpec((1, di), lambda bb, c: (0, 0)),
                     pl.BlockSpec((1, di), lambda bb, c: (0, 0))]
        args += [yf, p, dsk, ng]
    if with_output:
        out_specs.append(pl.BlockSpec((1, t, di), lambda bb, c: (bb, cidx(c), 0)))
        out_shape.append(jax.ShapeDtypeStruct((b, l, di), BF16))
    out_specs.append(state_spec)
    out_shape.append(jax.ShapeDtypeStruct((b, M_GROUPS, M_STATE, gw), F32))
    outs = pl.pallas_call(
        functools.partial(_ssd_kernel, reverse=reverse, with_output=with_output, epilogue=epilogue, heads=heads),
        grid=(b, nc),
        in_specs=in_specs,
        out_specs=out_specs,
        out_shape=out_shape,
        scratch_shapes=[pltpu.VMEM((M_GROUPS, M_STATE, gw), F32)],
        compiler_params=_cparams("parallel", "arbitrary"),
        name="ssd_bwd" if reverse else "ssd_fwd",
    )(*args)
    return (outs[0], outs[1]) if with_output else (None, outs[0])


def _filt_kernel(f_ref, w1_ref, b1_ref, w2_ref, b2_ref, fr_ref, w3_ref, dl_ref, o_ref, ss_ref):
    f = f_ref[...]
    fr = fr_ref[...]
    h = jnp.sin(fr * (jnp.dot(f, w1_ref[...], preferred_element_type=F32, precision=HIGHEST) + b1_ref[...]))
    h = jnp.sin(fr * (jnp.dot(h, w2_ref[...], preferred_element_type=F32, precision=HIGHEST) + b2_ref[...]))
    filt = jnp.dot(h, w3_ref[...], preferred_element_type=F32, precision=HIGHEST)
    tcol = f[:, H_EMB:H_EMB + 1]
    mcol = f[:, H_EMB + 1:H_EMB + 2]
    out = filt * jnp.exp(-tcol * dl_ref[...]) * mcol
    o_ref[...] = out

    @pl.when(pl.program_id(1) == 0)
    def _():
        ss_ref[...] = jnp.zeros_like(ss_ref)

    ss_ref[...] += jnp.sum(out * out, axis=0, keepdims=True)


def _filter_features(l):
    t = jnp.linspace(0.0, 1.0, l, dtype=F32)[:, None]
    ang = (2.0 * math.pi / l) * jnp.arange(l, dtype=F32)[:, None]
    nb = (H_EMB - 1) // 2
    bands = jnp.linspace(1e-4, nb - 1, nb, dtype=F32)[None, :]
    feats = jnp.concatenate([t, jnp.cos(bands * ang), -jnp.sin(bands * ang), t, jnp.ones_like(t)], axis=-1)
    rev = jnp.concatenate([feats[0:1] * 0.0, jnp.flip(feats[1:], axis=0)], axis=0)
    full = jnp.concatenate([feats, rev], axis=0)
    return jnp.pad(full, ((0, 0), (0, FEAT_COLS - full.shape[1])))


def _hyena_filter(l, w1, b1, w2, b2, freq, w3):
    hid = w2.shape[0]
    d = w3.shape[1] // 2
    feats = _filter_features(l)
    w1p = jnp.pad(w1, ((0, FEAT_COLS - w1.shape[0]), (0, 0)))
    deltas = jnp.abs(jnp.linspace(math.log(H_DECAY_TARGET) / H_SLOW_PCT, math.log(H_DECAY_TARGET) / H_FAST_PCT,
                                  d, dtype=F32))[None, :]
    tl = min(l, 512)
    cw = d
    nb = d // cw
    nrow = l // tl
    small = lambda shape: pl.BlockSpec(shape, lambda c, i: (0, 0))
    return pl.pallas_call(
        _filt_kernel,
        grid=(nb, 2 * nrow),
        in_specs=[pl.BlockSpec((tl, FEAT_COLS), lambda c, i: (i, 0)),
                  small((FEAT_COLS, hid)), small((1, hid)), small((hid, hid)), small((1, hid)), small((1, hid)),
                  pl.BlockSpec((hid, cw), lambda c, i: (0, c + (i // nrow) * nb)),
                  pl.BlockSpec((1, cw), lambda c, i: (0, c))],
        out_specs=[pl.BlockSpec((tl, cw), lambda c, i: (i, c)),
                   pl.BlockSpec((1, cw), lambda c, i: (0, c))],
        out_shape=[jax.ShapeDtypeStruct((2 * l, d), F32), jax.ShapeDtypeStruct((1, d), F32)],
        compiler_params=_cparams("parallel", "arbitrary"),
        name="hyena_filter",
    )(feats, w1p, b1.reshape(1, hid), w2, b2.reshape(1, hid), freq.reshape(1, hid), w3, deltas)


def _fft_split(n):
    n1 = 1 << ((n.bit_length() - 1 + 1) // 2)
    return n1, n // n1


def _dft_consts(n1, n2):
    n = n1 * n2
    h = n1 // 2
    k1 = jnp.arange(n1, dtype=jnp.int32)
    t1 = jnp.arange(n1, dtype=jnp.int32)
    t2 = jnp.arange(n2, dtype=jnp.int32)
    idx = (k1[None, :, None] * (n2 * t1[None, None, :] + t2[:, None, None])) % n
    ang = idx.astype(F32) * (2.0 * math.pi / n)
    cs, sn = jnp.cos(ang), jnp.sin(ang)
    ch, sh = cs[..., :h], sn[..., :h]
    t_data = jnp.concatenate([jnp.concatenate([ch, sh], -1), jnp.concatenate([-sh, ch], -1)], axis=1)
    t_filt = jnp.concatenate([cs, -sn], axis=1)
    ct, st = jnp.swapaxes(ch, 1, 2), jnp.swapaxes(sh, 1, 2)
    t_inv = jnp.concatenate([jnp.concatenate([ct, -st], -1), jnp.concatenate([st, ct], -1)], axis=1) / n
    k2 = jnp.arange(n2, dtype=jnp.int32)
    ang2 = ((k2[:, None] * k2[None, :]) % n2).astype(F32) * (2.0 * math.pi / n2)
    c2, s2 = jnp.cos(ang2), jnp.sin(ang2)
    m_fwd = jnp.concatenate([jnp.concatenate([c2, s2], -1), jnp.concatenate([-s2, c2], -1)], axis=0)
    m_inv = jnp.concatenate([jnp.concatenate([c2, -s2], -1), jnp.concatenate([s2, c2], -1)], axis=0)
    return dict(t_data=t_data.astype(BF16), t_filt=t_filt.astype(BF16), t_inv=t_inv.astype(BF16),
                m_fwd=m_fwd.astype(BF16), m_inv=m_inv.astype(BF16))


def _fft1_kernel(x_ref, t_ref, o_ref):
    a = x_ref.shape[1]
    xs = [pltpu.einshape("rjc->jrc", x_ref[0, s].astype(F32)) for s in range(a)]
    outs = []
    for j in range(SUBLANES):
        x = jnp.concatenate([xs[s][j] for s in range(a)], axis=0).astype(BF16)
        outs.append(jnp.dot(t_ref[j], x, preferred_element_type=F32))
    o = pltpu.einshape("jkc->kjc", jnp.stack(outs, axis=0))
    o_ref[0] = o.reshape(o_ref.shape[1:])


def _fft_stage1(xv, tmat, n1, n2, d):
    p, a, r, _, _ = xv.shape
    cw = min(d, COL_TILE)
    return pl.pallas_call(
        _fft1_kernel,
        grid=(n2 // SUBLANES, d // cw, p),
        in_specs=[pl.BlockSpec((1, a, r, SUBLANES, cw), lambda j, c, pp: (pp, 0, 0, j, c)),
                  pl.BlockSpec((SUBLANES, 2 * n1, n1), lambda j, c, pp: (j, 0, 0))],
        out_specs=pl.BlockSpec((1, 2, n1, SUBLANES, cw), lambda j, c, pp: (pp, 0, 0, j, c)),
        out_shape=jax.ShapeDtypeStruct((p, 2, n1, n2, d), F32),
        compiler_params=_cparams("parallel", "parallel", "parallel"),
        name="fft_stage1",
    )(xv, tmat)


def _fft2_filter_kernel(a_ref, m_ref, ss_ref, o_ref):
    _, _, _, n2, cw = a_ref.shape
    a = a_ref[0].reshape(2 * n2, cw).astype(BF16)
    xk = jnp.dot(m_ref[...], a, preferred_element_type=F32) * lax.rsqrt(ss_ref[...] + RMS_EPS)
    o_ref[...] = xk.reshape(o_ref.shape)


def _fft_filter_stage2(av, m_fwd, ss, n1, n2, d):
    cw = min(d, 1024)
    return pl.pallas_call(
        _fft2_filter_kernel,
        grid=(n1, d // cw),
        in_specs=[pl.BlockSpec((1, 2, 1, n2, cw), lambda k, c: (0, 0, k, 0, c)),
                  pl.BlockSpec((2 * n2, 2 * n2), lambda k, c: (0, 0)),
                  pl.BlockSpec((1, cw), lambda k, c: (0, c))],
        out_specs=pl.BlockSpec((2, 1, n2, cw), lambda k, c: (0, k, 0, c)),
        out_shape=jax.ShapeDtypeStruct((2, n1, n2, d), F32),
        compiler_params=_cparams("parallel", "parallel"),
        name="fft_filter_stage2",
    )(av, m_fwd, ss)


def _fft2_kernel(a_ref, h_ref, mf_ref, mi_ref, o_ref):
    _, _, _, n2, cw = a_ref.shape
    a = a_ref[0].reshape(2 * n2, cw).astype(BF16)
    xk = jnp.dot(mf_ref[...], a, preferred_element_type=F32)
    xr, xi = xk[:n2], xk[n2:]
    hr, hi = h_ref[0, 0], h_ref[1, 0]
    z = jnp.concatenate([xr * hr - xi * hi, xr * hi + xi * hr], axis=0).astype(BF16)
    o_ref[0] = jnp.dot(mi_ref[...], z, preferred_element_type=F32).reshape(o_ref.shape[1:])


def _fft_stage2(av, hf, m_fwd, m_inv, n1, n2, d):
    p = av.shape[0]
    cw = min(d, 1024)
    return pl.pallas_call(
        _fft2_kernel,
        grid=(n1, d // cw, p),
        in_specs=[pl.BlockSpec((1, 2, 1, n2, cw), lambda k, c, pp: (pp, 0, k, 0, c)),
                  pl.BlockSpec((2, 1, n2, cw), lambda k, c, pp: (0, k, 0, c)),
                  pl.BlockSpec((2 * n2, 2 * n2), lambda k, c, pp: (0, 0)),
                  pl.BlockSpec((2 * n2, 2 * n2), lambda k, c, pp: (0, 0))],
        out_specs=pl.BlockSpec((1, 2, 1, n2, cw), lambda k, c, pp: (pp, 0, k, 0, c)),
        out_shape=jax.ShapeDtypeStruct(av.shape, F32),
        compiler_params=_cparams("parallel", "parallel", "parallel"),
        name="fft_stage2",
    )(av, hf, m_fwd, m_inv)


def _fft3_kernel(b_ref, t_ref, u_ref, x0_ref, bias_ref, o_ref):
    n1 = b_ref.shape[2]
    h = n1 // 2
    bs = [pltpu.einshape("kjc->jkc", b_ref[0, s]) for s in range(2)]
    ys = []
    for j in range(SUBLANES):
        bv = jnp.concatenate([bs[0][j], bs[1][j]], axis=0).astype(BF16)
        ys.append(jnp.dot(t_ref[j], bv, preferred_element_type=F32))
    y = pltpu.einshape("jtc->tjc", jnp.stack(ys, axis=0))
    for s in range(2):
        conv = y[s * h:(s + 1) * h] + u_ref[0, s].astype(F32) * bias_ref[...]
        o_ref[0, s] = (x0_ref[0, s].astype(F32) * conv).astype(o_ref.dtype)


def _fft_stage3(bv, t_inv, uv, x0v, bias, n1, n2, d):
    p = bv.shape[0]
    h = n1 // 2
    cw = min(d, COL_TILE)
    io_spec = pl.BlockSpec((1, 2, h, SUBLANES, cw), lambda j, c, pp: (pp, 0, 0, j, c))
    return pl.pallas_call(
        _fft3_kernel,
        grid=(n2 // SUBLANES, d // cw, p),
        in_specs=[pl.BlockSpec((1, 2, n1, SUBLANES, cw), lambda j, c, pp: (pp, 0, 0, j, c)),
                  pl.BlockSpec((SUBLANES, n1, 2 * n1), lambda j, c, pp: (j, 0, 0)),
                  io_spec, io_spec,
                  pl.BlockSpec((1, cw), lambda j, c, pp: (0, c))],
        out_specs=io_spec,
        out_shape=jax.ShapeDtypeStruct(uv.shape, BF16),
        compiler_params=_cparams("parallel", "parallel", "parallel"),
        name="fft_stage3",
    )(bv, t_inv, uv, x0v, bias)


def _filter_spectrum(l, consts, lp):
    n1, n2 = _fft_split(2 * l)
    d = lp['h_bias'].shape[0]
    full, ss = _hyena_filter(l, lp['hf_w1'], lp['hf_b1'], lp['hf_w2'], lp['hf_b2'], lp['hf_freq'], lp['hf_w3'])
    a = _fft_stage1(full.reshape(1, 1, n1, n2, d), consts['t_filt'], n1, n2, d)
    return _fft_filter_stage2(a, consts['m_fwd'], ss, n1, n2, d)


def _long_conv(u, x0, hf, consts, bias):
    b, l, d = u.shape
    n1, n2 = _fft_split(2 * l)
    p = b // 2
    h = n1 // 2
    uv = u.reshape(p, 2, h, n2, d)
    a = _fft_stage1(uv, consts['t_data'], n1, n2, d)
    bv = _fft_stage2(a, hf, consts['m_fwd'], consts['m_inv'], n1, n2, d)
    y = _fft_stage3(bv, consts['t_inv'], uv, x0.reshape(p, 2, h, n2, d), bias.reshape(1, d), n1, n2, d)
    return y.reshape(b, l, d)


def _merge_kernel(ym_ref, yh_ref, ga_ref, gb_ref, wm_ref, wh_ref, o_ref):
    a = jnp.dot(ym_ref[0], wm_ref[...], preferred_element_type=F32)
    bb = jnp.dot(yh_ref[0], wh_ref[...], preferred_element_type=F32)
    ga = jax.nn.sigmoid(ga_ref[0].astype(F32))
    gb = jax.nn.sigmoid(gb_ref[0].astype(F32))
    o_ref[0] = (ga * a + gb * bb).astype(o_ref.dtype)


def _merge(ym, yh, p, gate_col0, wm, wh):
    b, l, di = ym.shape
    d = yh.shape[2]
    tm = min(l, 1024)
    tn = min(d, COL_TILE)
    g0 = gate_col0 // tn
    nb = d // tn
    return pl.pallas_call(
        _merge_kernel,
        grid=(b, l // tm, nb),
        in_specs=[pl.BlockSpec((1, tm, di), lambda bb, i, j: (bb, i, 0)),
                  pl.BlockSpec((1, tm, d), lambda bb, i, j: (bb, i, 0)),
                  pl.BlockSpec((1, tm, tn), lambda bb, i, j: (bb, i, g0 + j)),
                  pl.BlockSpec((1, tm, tn), lambda bb, i, j: (bb, i, g0 + nb + j)),
                  pl.BlockSpec((di, tn), lambda bb, i, j: (0, j)),
                  pl.BlockSpec((d, tn), lambda bb, i, j: (0, j))],
        out_specs=pl.BlockSpec((1, tm, tn), lambda bb, i, j: (bb, i, j)),
        out_shape=jax.ShapeDtypeStruct((b, l, d), BF16),
        compiler_params=_cparams("parallel", "parallel", "arbitrary"),
        name="merge",
    )(ym, yh, p, p, wm, wh)


def _resid_mm_kernel(a_ref, w_ref, x_ref, g_ref, o_ref):
    o_ref[0] = x_ref[0] + g_ref[0] * jnp.dot(a_ref[0], w_ref[...], preferred_element_type=F32)


def _resid_mm(a, w, x, gate):
    b, l, k = a.shape
    d = w.shape[1]
    tm = min(l, 1024)
    tn = min(d, COL_TILE)
    return pl.pallas_call(
        _resid_mm_kernel,
        grid=(b, l // tm, d // tn),
        in_specs=[pl.BlockSpec((1, tm, k), lambda bb, i, j: (bb, i, 0)),
                  pl.BlockSpec((k, tn), lambda bb, i, j: (0, j)),
                  pl.BlockSpec((1, tm, tn), lambda bb, i, j: (bb, i, j)),
                  pl.BlockSpec((1, 1, tn), lambda bb, i, j: (bb, 0, j))],
        out_specs=pl.BlockSpec((1, tm, tn), lambda bb, i, j: (bb, i, j)),
        out_shape=jax.ShapeDtypeStruct((b, l, d), F32),
        compiler_params=_cparams("parallel", "parallel", "arbitrary"),
        name="resid_mm",
    )(a, w, x, gate)


def _ffn_kernel(x_ref, g_ref, sh_ref, sc_ref, gt_ref, wg_ref, wu_ref, wd_ref, fg_ref, o_ref, h_scr, *, final_norm):
    j = pl.program_id(2)

    @pl.when(j == 0)
    def _():
        h_scr[...] = _modnorm(x_ref[0], g_ref[...], sh_ref[0], sc_ref[0]).astype(BF16)
        o_ref[...] = jnp.zeros_like(o_ref)

    tm = h_scr.shape[0]
    rows_per_pass = min(tm, 512)
    for r0 in range(0, tm, rows_per_pass):
        h = h_scr[r0:r0 + rows_per_pass, :]
        gg = jnp.dot(h, wg_ref[...], preferred_element_type=F32)
        uu = jnp.dot(h, wu_ref[...], preferred_element_type=F32)
        o_ref[0, r0:r0 + rows_per_pass, :] += jnp.dot((_silu(gg) * uu).astype(BF16), wd_ref[...],
                                                      preferred_element_type=F32)

    @pl.when(j == pl.num_programs(2) - 1)
    def _():
        y = x_ref[0] + gt_ref[0] * o_ref[0]
        if final_norm:
            ms = jnp.mean(y * y, axis=-1, keepdims=True)
            y = y * lax.rsqrt(ms + RMS_EPS) * fg_ref[...]
        o_ref[0] = y


def _ffn(x, g, shift, scale, gate, w_gu, w_down, final_g=None):
    b, l, d = x.shape
    f = w_down.shape[0]
    tm = min(l, 1024)
    fg = jnp.ones((1, d), F32) if final_g is None else final_g.reshape(1, d)
    tf = 256
    nf = f // tf
    vec = pl.BlockSpec((1, 1, d), lambda bb, i, j: (bb, 0, 0))
    return pl.pallas_call(
        functools.partial(_ffn_kernel, final_norm=final_g is not None),
        grid=(b, l // tm, nf),
        in_specs=[pl.BlockSpec((1, tm, d), lambda bb, i, j: (bb, i, 0)),
                  pl.BlockSpec((1, d), lambda bb, i, j: (0, 0)),
                  vec, vec, vec,
                  pl.BlockSpec((d, tf), lambda bb, i, j: (0, j)),
                  pl.BlockSpec((d, tf), lambda bb, i, j: (0, j + nf)),
                  pl.BlockSpec((tf, d), lambda bb, i, j: (j, 0)),
                  pl.BlockSpec((1, d), lambda bb, i, j: (0, 0))],
        out_specs=pl.BlockSpec((1, tm, d), lambda bb, i, j: (bb, i, 0)),
        out_shape=jax.ShapeDtypeStruct((b, l, d), F32),
        scratch_shapes=[pltpu.VMEM((tm, d), BF16)],
        compiler_params=_cparams("parallel", "parallel", "arbitrary"),
        name="ffn",
    )(x, g.reshape(1, d), shift, scale, gate, w_gu, w_gu, w_down, fg)


def _grid_pos_embed(rows, d):
    r, col = jnp.meshgrid(jnp.arange(rows), jnp.arange(GRID_W), indexing='ij')
    quarter = d // 4
    omega = 1.0 / (POS_BASE ** (jnp.arange(quarter, dtype=F32) / quarter))

    def axis_embed(pos):
        ang = pos.reshape(-1)[:, None].astype(F32) * omega[None, :]
        return jnp.concatenate([jnp.sin(ang), jnp.cos(ang)], axis=-1)

    return jnp.concatenate([axis_embed(r), axis_embed(col)], axis=-1).astype(F32)


class _Cols:
    def __init__(self, d):
        self.d = d
        self.di = 2 * d
        self.heads = self.di // M_HEADDIM
        self.xbc = self.di + 2 * M_GROUPS * M_STATE
        self.z0 = 0
        self.gate0 = self.z0 + self.di
        self.hy0 = self.gate0 + 2 * d
        self.xbc0 = self.hy0 + 3 * d
        self.dt0 = self.xbc0 + self.xbc
        self.total = self.dt0 + COL_TILE
        assert 2 * self.heads <= LANES and self.heads % (2 * M_GROUPS) == 0
        for off in (self.gate0, self.hy0, self.xbc0, self.dt0):
            assert off % COL_TILE == 0


def _pack_w_in(w_in, cols):
    o_dt = cols.xbc
    o_z = o_dt + 2 * cols.heads
    o_hy = o_z + cols.di
    o_gate = o_hy + 3 * cols.d
    parts = [w_in[:, o_z:o_hy], w_in[:, o_gate:], w_in[:, o_hy:o_gate], w_in[:, :o_dt], w_in[:, o_dt:o_z]]
    w = jnp.concatenate(parts, axis=1)
    return jnp.pad(w, ((0, 0), (0, cols.total - w.shape[1]))).astype(BF16)


def _pad_lanes(v):
    v = v.reshape(1, -1)
    return jnp.pad(v, ((0, 0), (0, LANES - v.shape[1])))


def _mix_sublayer(x, mods, lp, cols, hf, consts, init_f, init_b):
    h = _prenorm(x, lp['norm1_g'], mods[0], mods[1])
    p, dtp = _inproj(h, lp['w_in'], 0, cols.total)
    xbc = _mconv(p, cols.xbc0, lp['m_conv_w'], lp['m_conv_b'])
    y_f, fin_f = _ssd(xbc, dtp, lp['dtb'], lp['alog'], init_f, cols.heads, False, True)
    ym, fin_b = _ssd(xbc, dtp, lp['dtb'], lp['alog'], init_b, cols.heads, True, True,
                     extra=(y_f, p, cols.z0, lp['dskip'], lp['m_norm_g']))
    x0, u = _hconv(p, cols.hy0, lp['h_conv_w'], lp['h_conv_b'], cols.d)
    yh = _long_conv(u, x0, hf, consts, lp['h_bias'])
    merged = _merge(ym, yh, p, cols.gate0, lp['m_w_out'], lp['h_w_out'])
    return _resid_mm(merged, lp['w_merge_out'], x, mods[2]), fin_f, fin_b


def _ctx_final_states(x, mods, lp, cols, init_f, init_b):
    ncols = cols.total - cols.xbc0
    h = _prenorm(x, lp['norm1_g'], mods[0], mods[1])
    p, dtp = _inproj(h, lp['w_in'], cols.xbc0, ncols)
    xbc = _mconv(p, 0, lp['m_conv_w'], lp['m_conv_b'])
    _, fin_f = _ssd(xbc, dtp, lp['dtb'], lp['alog'], init_f, cols.heads, False, False)
    _, fin_b = _ssd(xbc, dtp, lp['dtb'], lp['alog'], init_b, cols.heads, True, False)
    return fin_f, fin_b


def kernel(x, c, ctx, c_ctx, ada_w, ada_b, norm1_g, w_in, m_conv_w, m_conv_b, m_dt_bias, m_a_log, m_d, m_norm_g,
           m_w_out, h_conv_w, h_conv_b, hf_w1, hf_b1, hf_w2, hf_b2, hf_freq, hf_w3, h_bias, h_w_out, w_merge_out,
           norm2_g, ffn_w_gu, ffn_w_down, final_g):
    b, l, d = x.shape
    lc = ctx.shape[1]
    depth = ada_w.shape[0]
    cols = _Cols(d)
    assert b % 2 == 0 and b + 1 <= SUBLANES and l % M_CHUNK == 0 and lc % M_CHUNK == 0

    cvecs = jnp.concatenate([c, c_ctx[None], jnp.zeros((SUBLANES - b - 1, d), F32)], axis=0)
    mods_all = _mods(cvecs, ada_w, ada_b)
    consts_l = _dft_consts(*_fft_split(2 * l))
    consts_c = _dft_consts(*_fft_split(2 * lc))

    x_l = _add_pos(x, _grid_pos_embed(l // GRID_W, d))
    x_c = ctx
    zero_state = jnp.zeros((b, M_GROUPS, M_STATE, cols.di // M_GROUPS), F32)
    for i in range(depth):
        lp = dict(norm1_g=norm1_g[i], w_in=_pack_w_in(w_in[i], cols), m_conv_w=m_conv_w[i], m_conv_b=m_conv_b[i],
                  dtb=_pad_lanes(m_dt_bias[i]), alog=_pad_lanes(m_a_log[i]),
                  dskip=jnp.repeat(m_d[i], M_HEADDIM).reshape(1, cols.di), m_norm_g=m_norm_g[i].reshape(1, cols.di),
                  m_w_out=m_w_out[i].astype(BF16), h_conv_w=h_conv_w[i], h_conv_b=h_conv_b[i],
                  hf_w1=hf_w1[i], hf_b1=hf_b1[i], hf_w2=hf_w2[i], hf_b2=hf_b2[i], hf_freq=hf_freq[i], hf_w3=hf_w3[i],
                  h_bias=h_bias[i], h_w_out=h_w_out[i].astype(BF16), w_merge_out=w_merge_out[i].astype(BF16),
                  norm2_g=norm2_g[i], ffn_w_gu=ffn_w_gu[i].astype(BF16), ffn_w_down=ffn_w_down[i].astype(BF16))
        m = mods_all[i].reshape(SUBLANES, ADA_CHUNKS, d)
        mods_l = [m[:b, k][:, None, :] for k in range(ADA_CHUNKS)]
        mods_c = [jnp.broadcast_to(m[b, k][None, None, :], (b, 1, d)) for k in range(ADA_CHUNKS)]
        if i < depth - 1:
            hf_c = _filter_spectrum(lc, consts_c, lp)
            x_c_mixed, fin_f, fin_b = _mix_sublayer(x_c, mods_c, lp, cols, hf_c, consts_c, zero_state, zero_state)
        else:
            fin_f, fin_b = _ctx_final_states(x_c, mods_c, lp, cols, zero_state, zero_state)
        hf_l = _filter_spectrum(l, consts_l, lp)
        x_l, _, _ = _mix_sublayer(x_l, mods_l, lp, cols, hf_l, consts_l, fin_f, fin_b)
        x_l = _ffn(x_l, lp['norm2_g'], mods_l[3], mods_l[4], mods_l[5], lp['ffn_w_gu'], lp['ffn_w_down'],
                   final_g=final_g if i == depth - 1 else None)
        if i < depth - 1:
            x_c = _ffn(x_c_mixed, lp['norm2_g'], mods_c[3], mods_c[4], mods_c[5], lp['ffn_w_gu'], lp['ffn_w_down'])
    return x_l
```

```python
import functools
import math

import jax
import jax.numpy as jnp
from jax import lax
from jax.experimental import pallas as pl
from jax.experimental.pallas import tpu as pltpu

F32 = jnp.float32
BF16 = jnp.bfloat16
HIGHEST = lax.Precision.HIGHEST

RMS_EPS = 1e-6
GRID_W = 64
POS_BASE = 10000.0
ADA_CHUNKS = 6
M_HEADDIM = 64
M_GROUPS = 8
M_STATE = 128
M_CONV = 5
M_CHUNK = 128
H_SHORT = 3
H_EMB = 33
H_DECAY_TARGET = 1e-2
H_FAST_PCT = 0.3
H_SLOW_PCT = 1.5

LANES = 128
SUBLANES = 8
HALO = 16
VMEM_LIMIT_BYTES = 56 * 1024 * 1024

COL_TILE = 512
FEAT_COLS = 40
FFT_K1_BLOCK = 4


def _cparams(*sem):
    return pltpu.CompilerParams(dimension_semantics=sem, vmem_limit_bytes=VMEM_LIMIT_BYTES)


def _silu(v):
    h = 0.5 * v
    return h + h * jnp.tanh(h)


def _round_up(a, m):
    return -(-a // m) * m


def _mods_kernel(c_ref, w_ref, b_ref, o_ref):
    s = _silu(c_ref[...])
    o_ref[0] = jnp.dot(s, w_ref[0], preferred_element_type=F32, precision=HIGHEST) + b_ref[0]


def _mods(cvecs, ada_w, ada_b):
    depth, d, n = ada_w.shape
    tn = min(n, 1024)
    return pl.pallas_call(
        _mods_kernel,
        grid=(depth, n // tn),
        in_specs=[pl.BlockSpec((SUBLANES, d), lambda i, j: (0, 0)),
                  pl.BlockSpec((1, d, tn), lambda i, j: (i, 0, j)),
                  pl.BlockSpec((1, 1, tn), lambda i, j: (i, 0, j))],
        out_specs=pl.BlockSpec((1, SUBLANES, tn), lambda i, j: (i, 0, j)),
        out_shape=jax.ShapeDtypeStruct((depth, SUBLANES, n), F32),
        compiler_params=_cparams("parallel", "parallel"),
        name="mods",
    )(cvecs, ada_w, ada_b.reshape(depth, 1, n))


def _add_kernel(x_ref, p_ref, o_ref):
    o_ref[0] = x_ref[0] + p_ref[...]


def _add_pos(x, pos):
    b, l, d = x.shape
    tl = min(l, 1024)
    return pl.pallas_call(
        _add_kernel,
        grid=(l // tl, b),
        in_specs=[pl.BlockSpec((1, tl, d), lambda i, bb: (bb, i, 0)),
                  pl.BlockSpec((tl, d), lambda i, bb: (i, 0))],
        out_specs=pl.BlockSpec((1, tl, d), lambda i, bb: (bb, i, 0)),
        out_shape=jax.ShapeDtypeStruct(x.shape, F32),
        compiler_params=_cparams("parallel", "parallel"),
        name="add_pos",
    )(x, pos)


def _modnorm(x, g, shift, scale):
    ms = jnp.mean(x * x, axis=-1, keepdims=True)
    y = x * lax.rsqrt(ms + RMS_EPS) * g
    return y * (1.0 + scale) + shift


def _prenorm_kernel(x_ref, g_ref, sh_ref, sc_ref, o_ref):
    o_ref[0] = _modnorm(x_ref[0], g_ref[...], sh_ref[0], sc_ref[0]).astype(o_ref.dtype)


def _prenorm(x, g, shift, scale):
    b, l, d = x.shape
    tl = min(l, 1024)
    vec = pl.BlockSpec((1, 1, d), lambda bb, i: (bb, 0, 0))
    return pl.pallas_call(
        _prenorm_kernel,
        grid=(b, l // tl),
        in_specs=[pl.BlockSpec((1, tl, d), lambda bb, i: (bb, i, 0)),
                  pl.BlockSpec((1, d), lambda bb, i: (0, 0)), vec, vec],
        out_specs=pl.BlockSpec((1, tl, d), lambda bb, i: (bb, i, 0)),
        out_shape=jax.ShapeDtypeStruct((b, l, d), BF16),
        compiler_params=_cparams("parallel", "parallel"),
        name="prenorm",
    )(x, g.reshape(1, d), shift, scale)


def _inproj_kernel(h_ref, w_ref, o_ref, dt_ref):
    j = pl.program_id(2)
    last = pl.num_programs(2) - 1
    acc = jnp.dot(h_ref[0], w_ref[...], preferred_element_type=F32)

    @pl.when(j < last)
    def _():
        o_ref[0] = acc.astype(o_ref.dtype)

    @pl.when(j == last)
    def _():
        dt_ref[0] = acc


def _inproj(h, w, col0, ncols):
    b, l, d = h.shape
    tm = min(l, 2048)
    tn = COL_TILE
    j0 = col0 // tn
    nj = ncols // tn
    return pl.pallas_call(
        _inproj_kernel,
        grid=(b, l // tm, nj),
        in_specs=[pl.BlockSpec((1, tm, d), lambda bb, i, j: (bb, i, 0)),
                  pl.BlockSpec((d, tn), lambda bb, i, j: (0, j + j0))],
        out_specs=[pl.BlockSpec((1, tm, tn), lambda bb, i, j: (bb, i, jnp.minimum(j, nj - 2))),
                   pl.BlockSpec((1, tm, tn), lambda bb, i, j: (bb, i, 0))],
        out_shape=[jax.ShapeDtypeStruct((b, l, ncols - tn), BF16), jax.ShapeDtypeStruct((b, l, tn), F32)],
        compiler_params=_cparams("parallel", "parallel", "arbitrary"),
        name="inproj",
    )(h, w)


def _conv_taps(prev_ref, main_ref, next_ref, w_ref, b_ref, ktaps):
    i = pl.program_id(2)
    nblk = pl.num_programs(2)
    tl = main_ref.shape[1]
    pad = ktaps // 2
    prev = jnp.where(i > 0, prev_ref[0].astype(F32), 0.0)
    nxt = jnp.where(i < nblk - 1, next_ref[0].astype(F32), 0.0)
    ext = jnp.concatenate([prev, main_ref[0].astype(F32), nxt], axis=0)
    rows = tl + 2 * HALO
    acc = None
    for j in range(ktaps):
        shifted = ext if j == pad else pltpu.roll(ext, (pad - j) % rows, axis=0)
        term = shifted[HALO:HALO + tl] * w_ref[j:j + 1, :]
        acc = term if acc is None else acc + term
    return acc + b_ref[...]


def _conv_specs(tl, cw, l, col_block0):
    rpb = tl // HALO
    last = l // HALO - 1
    return [
        pl.BlockSpec((1, HALO, cw), lambda bb, c, i: (bb, jnp.maximum(i * rpb - 1, 0), c + col_block0)),
        pl.BlockSpec((1, tl, cw), lambda bb, c, i: (bb, i, c + col_block0)),
        pl.BlockSpec((1, HALO, cw), lambda bb, c, i: (bb, jnp.minimum((i + 1) * rpb, last), c + col_block0)),
    ]


def _mconv_kernel(prev_ref, main_ref, next_ref, w_ref, b_ref, o_ref):
    o_ref[0] = _silu(_conv_taps(prev_ref, main_ref, next_ref, w_ref, b_ref, M_CONV)).astype(o_ref.dtype)


def _mconv(p, col0, w, bias):
    b, l, _ = p.shape
    xbc = w.shape[1]
    tl = min(l, 512)
    cw = COL_TILE
    return pl.pallas_call(
        _mconv_kernel,
        grid=(b, xbc // cw, l // tl),
        in_specs=_conv_specs(tl, cw, l, col0 // cw) + [
            pl.BlockSpec((M_CONV, cw), lambda bb, c, i: (0, c)),
            pl.BlockSpec((1, cw), lambda bb, c, i: (0, c))],
        out_specs=pl.BlockSpec((1, tl, cw), lambda bb, c, i: (bb, i, c)),
        out_shape=jax.ShapeDtypeStruct((b, l, xbc), BF16),
        compiler_params=_cparams("parallel", "parallel", "arbitrary"),
        name="mconv",
    )(p, p, p, w, bias.reshape(1, xbc))


def _hconv_kernel(p0, m0, n0, p1, m1, n1, p2, m2, n2, w0, b0, w1, b1, w2, b2, x0_ref, u_ref):
    x0_ref[0] = _conv_taps(p0, m0, n0, w0, b0, H_SHORT).astype(x0_ref.dtype)
    x1 = _conv_taps(p1, m1, n1, w1, b1, H_SHORT)
    v = _conv_taps(p2, m2, n2, w2, b2, H_SHORT)
    u_ref[0] = (x1 * v).astype(u_ref.dtype)


def _hconv(p, col0, w, bias, d):
    b, l, _ = p.shape
    tl = min(l, 512)
    cw = min(d, COL_TILE)
    nb = d // cw
    specs, wspecs = [], []
    for s in range(3):
        specs += _conv_specs(tl, cw, l, col0 // cw + s * nb)
        wspecs += [pl.BlockSpec((H_SHORT, cw), functools.partial(lambda bb, c, i, s: (0, c + s * nb), s=s)),
                   pl.BlockSpec((1, cw), functools.partial(lambda bb, c, i, s: (0, c + s * nb), s=s))]
    bias2 = bias.reshape(1, 3 * d)
    out_spec = pl.BlockSpec((1, tl, cw), lambda bb, c, i: (bb, i, c))
    return pl.pallas_call(
        _hconv_kernel,
        grid=(b, nb, l // tl),
        in_specs=specs + wspecs,
        out_specs=[out_spec, out_spec],
        out_shape=[jax.ShapeDtypeStruct((b, l, d), BF16)] * 2,
        compiler_params=_cparams("parallel", "parallel", "arbitrary"),
        name="hconv",
    )(p, p, p, p, p, p, p, p, p, w, bias2, w, bias2, w, bias2)


def _ssd_kernel(*refs, reverse, with_output, epilogue, heads):
    if epilogue:
        (xbc_ref, dt_ref, dtb_ref, alog_ref, init_ref, yf_ref, z_ref, dsk_ref, ng_ref,
         y_ref, fin_ref, s_scr) = refs
    elif with_output:
        xbc_ref, dt_ref, dtb_ref, alog_ref, init_ref, y_ref, fin_ref, s_scr = refs
    else:
        xbc_ref, dt_ref, dtb_ref, alog_ref, init_ref, fin_ref, s_scr = refs
    t = M_CHUNK
    n = M_STATE
    assert n == t
    di = heads * M_HEADDIM
    gw = di // M_GROUPS
    rpg = heads // M_GROUPS
    lane0 = heads if reverse else 0
    edge = 0 if reverse else t - 1
    c = pl.program_id(1)

    @pl.when(c == 0)
    def _():
        s_scr[...] = init_ref[0]

    x = dt_ref[0] + dtb_ref[...]
    dt = jnp.maximum(x, 0.0) + jnp.log1p(jnp.exp(-jnp.abs(x)))
    dta = dt * (-jnp.exp(alog_ref[...]))
    row = lax.broadcasted_iota(jnp.int32, (t, t), 0)
    col = lax.broadcasted_iota(jnp.int32, (t, t), 1)
    mask = (row <= col) if reverse else (row >= col)
    acs = jnp.dot(mask.astype(F32), dta, preferred_element_type=F32, precision=HIGHEST)
    acs_t = acs.T
    dt_t = dt.T
    arow_dt = acs_t - jnp.log(dt_t)
    w_t = jnp.exp(acs_t[:, edge:edge + 1] - acs_t) * dt_t
    dec = jnp.exp(acs[edge:edge + 1, :])
    lane = lax.broadcasted_iota(jnp.int32, (1, LANES), 1)
    lo = lane < M_HEADDIM
    mlo = jnp.where(lo, 1.0, 0.0).astype(BF16)
    mhi = jnp.where(lo, 0.0, 1.0).astype(BF16)

    for g in range(M_GROUPS):
        bm = xbc_ref[0, :, di + g * n:di + (g + 1) * n].astype(F32)
        cm = xbc_ref[0, :, di + M_GROUPS * n + g * n:di + M_GROUPS * n + (g + 1) * n].astype(F32)
        bt = bm.T
        if with_output:
            cb = jnp.dot(cm.astype(BF16), bt.astype(BF16), preferred_element_type=F32)
        y_parts = []
        for q in range(rpg // 2):
            c0 = g * gw + q * LANES
            xp = xbc_ref[0, :, c0:c0 + LANES]
            sp = s_scr[g, :, q * LANES:(q + 1) * LANES]
            xlo = xp * mlo
            xhi = xp * mhi
            lhs, btw, decs = [], [], []
            for r in range(2):
                hl = lane0 + g * rpg + 2 * q + r
                if with_output:
                    acol = jnp.broadcast_to(acs[:, hl:hl + 1], (t, t))
                    dk = jnp.exp(jnp.where(mask, acol - arow_dt[hl:hl + 1, :], -jnp.inf))
                    lhs.append((cb * dk).astype(BF16))
                    lhs.append((cm * jnp.exp(acol)).astype(BF16))
                btw.append((bt * w_t[hl:hl + 1, :]).astype(BF16))
                decs.append(jnp.broadcast_to(dec[:, hl:hl + 1], (1, LANES)))
            if with_output:
                slo = jnp.where(lo, sp, 0.0).astype(BF16)
                shi = jnp.where(lo, 0.0, sp).astype(BF16)
                y_parts.append(jnp.dot(jnp.concatenate(lhs, axis=1), jnp.concatenate([xlo, slo, xhi, shi], axis=0),
                                       preferred_element_type=F32))
            ds = jnp.dot(jnp.concatenate(btw, axis=1), jnp.concatenate([xlo, xhi], axis=0),
                         preferred_element_type=F32)
            s_scr[g, :, q * LANES:(q + 1) * LANES] = jnp.where(lo, decs[0], decs[1]) * sp + ds
        if with_output:
            yg = jnp.concatenate(y_parts, axis=1) if len(y_parts) > 1 else y_parts[0]
            sl = slice(g * gw, (g + 1) * gw)
            if epilogue:
                yg = yg + yf_ref[0, :, sl].astype(F32)
                v = (yg + xbc_ref[0, :, sl].astype(F32) * dsk_ref[:, sl]) * _silu(z_ref[0, :, sl].astype(F32))
                ms = jnp.mean(v * v, axis=-1, keepdims=True)
                y_ref[0, :, sl] = (v * lax.rsqrt(ms + RMS_EPS) * ng_ref[:, sl]).astype(y_ref.dtype)
            else:
                y_ref[0, :, sl] = yg.astype(y_ref.dtype)

    @pl.when(c == pl.num_programs(1) - 1)
    def _():
        fin_ref[0] = s_scr[...]


def _ssd(xbc, dtp, dtb, alog, init, heads, reverse, with_output, extra=None):
    b, l, xw = xbc.shape
    t = M_CHUNK
    nc = l // t
    di = heads * M_HEADDIM
    gw = di // M_GROUPS
    epilogue = extra is not None
    cidx = (lambda c: nc - 1 - c) if reverse else (lambda c: c)
    state_spec = pl.BlockSpec((1, M_GROUPS, M_STATE, gw), lambda bb, c: (bb, 0, 0, 0))
    in_specs = [pl.BlockSpec((1, t, xw), lambda bb, c: (bb, cidx(c), 0)),
                pl.BlockSpec((1, t, LANES), lambda bb, c: (bb, cidx(c), 0)),
                pl.BlockSpec((1, LANES), lambda bb, c: (0, 0)),
                pl.BlockSpec((1, LANES), lambda bb, c: (0, 0)),
                state_spec]
    args = [xbc, dtp, dtb, alog, init]
    out_specs, out_shape = [], []
    if epilogue:
        yf, p, z_col0, dsk, ng = extra
        in_specs += [pl.BlockSpec((1, t, di), lambda bb, c: (bb, cidx(c), 0)),
                     pl.BlockSpec((1, t, di), lambda bb, c: (bb, cidx(c), z_col0 // di)),
                     pl.BlockSpec((1, di), lambda bb, c: (0, 0)),
                     pl.BlockSpec((1, di), lambda bb, c: (0, 0))]
        args += [yf, p, dsk, ng]
    if with_output:
        out_specs.append(pl.BlockSpec((1, t, di), lambda bb, c: (bb, cidx(c), 0)))
        out_shape.append(jax.ShapeDtypeStruct((b, l, di), BF16))
    out_specs.append(state_spec)
    out_shape.append(jax.ShapeDtypeStruct((b, M_GROUPS, M_STATE, gw), F32))
    outs = pl.pallas_call(
        functools.partial(_ssd_kernel, reverse=reverse, with_output=with_output, epilogue=epilogue, heads=heads),
        grid=(b, nc),
        in_specs=in_specs,
        out_specs=out_specs,
        out_shape=out_shape,
        scratch_shapes=[pltpu.VMEM((M_GROUPS, M_STATE, gw), F32)],
        compiler_params=_cparams("parallel", "arbitrary"),
        name="ssd_bwd" if reverse else "ssd_fwd",
    )(*args)
    return (outs[0], outs[1]) if with_output else (None, outs[0])


def _filt_kernel(f_ref, w1_ref, b1_ref, w2_ref, b2_ref, fr_ref, w3_ref, dl_ref, o_ref, ss_ref):
    f = f_ref[...]
    fr = fr_ref[...]
    h = jnp.sin(fr * (jnp.dot(f, w1_ref[...], preferred_element_type=F32, precision=HIGHEST) + b1_ref[...]))
    h = jnp.sin(fr * (jnp.dot(h, w2_ref[...], preferred_element_type=F32, precision=HIGHEST) + b2_ref[...]))
    filt = jnp.dot(h, w3_ref[...], preferred_element_type=F32, precision=HIGHEST)
    tcol = f[:, H_EMB:H_EMB + 1]
    mcol = f[:, H_EMB + 1:H_EMB + 2]
    out = filt * jnp.exp(-tcol * dl_ref[...]) * mcol
    o_ref[...] = out

    @pl.when(pl.program_id(1) == 0)
    def _():
        ss_ref[...] = jnp.zeros_like(ss_ref)

    ss_ref[...] += jnp.sum(out * out, axis=0, keepdims=True)


def _filter_features(l):
    t = jnp.linspace(0.0, 1.0, l, dtype=F32)[:, None]
    ang = (2.0 * math.pi / l) * jnp.arange(l, dtype=F32)[:, None]
    nb = (H_EMB - 1) // 2
    bands = jnp.linspace(1e-4, nb - 1, nb, dtype=F32)[None, :]
    feats = jnp.concatenate([t, jnp.cos(bands * ang), -jnp.sin(bands * ang), t, jnp.ones_like(t)], axis=-1)
    rev = jnp.concatenate([feats[0:1] * 0.0, jnp.flip(feats[1:], axis=0)], axis=0)
    full = jnp.concatenate([feats, rev], axis=0)
    return jnp.pad(full, ((0, 0), (0, FEAT_COLS - full.shape[1])))


def _hyena_filter(l, w1, b1, w2, b2, freq, w3):
    hid = w2.shape[0]
    d = w3.shape[1] // 2
    feats = _filter_features(l)
    w1p = jnp.pad(w1, ((0, FEAT_COLS - w1.shape[0]), (0, 0)))
    deltas = jnp.abs(jnp.linspace(math.log(H_DECAY_TARGET) / H_SLOW_PCT, math.log(H_DECAY_TARGET) / H_FAST_PCT,
                                  d, dtype=F32))[None, :]
    tl = min(l, 512)
    cw = d
    nb = d // cw
    nrow = l // tl
    small = lambda shape: pl.BlockSpec(shape, lambda c, i: (0, 0))
    return pl.pallas_call(
        _filt_kernel,
        grid=(nb, 2 * nrow),
        in_specs=[pl.BlockSpec((tl, FEAT_COLS), lambda c, i: (i, 0)),
                  small((FEAT_COLS, hid)), small((1, hid)), small((hid, hid)), small((1, hid)), small((1, hid)),
                  pl.BlockSpec((hid, cw), lambda c, i: (0, c + (i // nrow) * nb)),
                  pl.BlockSpec((1, cw), lambda c, i: (0, c))],
        out_specs=[pl.BlockSpec((tl, cw), lambda c, i: (i, c)),
                   pl.BlockSpec((1, cw), lambda c, i: (0, c))],
        out_shape=[jax.ShapeDtypeStruct((2 * l, d), F32), jax.ShapeDtypeStruct((1, d), F32)],
        compiler_params=_cparams("parallel", "arbitrary"),
        name="hyena_filter",
    )(feats, w1p, b1.reshape(1, hid), w2, b2.reshape(1, hid), freq.reshape(1, hid), w3, deltas)


def _fft_split(n):
    n1 = 1 << ((n.bit_length() - 1 + 1) // 2)
    return n1, n // n1


def _dft_consts(n1, n2):
    n = n1 * n2
    h = n1 // 2
    k1 = jnp.arange(n1, dtype=jnp.int32)
    t1 = jnp.arange(n1, dtype=jnp.int32)
    t2 = jnp.arange(n2, dtype=jnp.int32)
    idx = (k1[None, :, None] * (n2 * t1[None, None, :] + t2[:, None, None])) % n
    ang = idx.astype(F32) * (2.0 * math.pi / n)
    cs, sn = jnp.cos(ang), jnp.sin(ang)
    ch, sh = cs[..., :h], sn[..., :h]
    t_data = jnp.concatenate([jnp.concatenate([ch, sh], -1), jnp.concatenate([-sh, ch], -1)], axis=1)
    t_filt = jnp.concatenate([cs, -sn], axis=1)
    ct, st = jnp.swapaxes(ch, 1, 2), jnp.swapaxes(sh, 1, 2)
    t_inv = jnp.concatenate([jnp.concatenate([ct, -st], -1), jnp.concatenate([st, ct], -1)], axis=1) / n
    k2 = jnp.arange(n2, dtype=jnp.int32)
    ang2 = ((k2[:, None] * k2[None, :]) % n2).astype(F32) * (2.0 * math.pi / n2)
    c2, s2 = jnp.cos(ang2), jnp.sin(ang2)
    m_fwd = jnp.concatenate([jnp.concatenate([c2, s2], -1), jnp.concatenate([-s2, c2], -1)], axis=0)
    m_inv = jnp.concatenate([jnp.concatenate([c2, -s2], -1), jnp.concatenate([s2, c2], -1)], axis=0)
    return dict(t_data=t_data.astype(BF16), t_filt=t_filt.astype(BF16), t_inv=t_inv.astype(BF16),
                m_fwd=m_fwd.astype(BF16), m_inv=m_inv.astype(BF16))


def _fft1_kernel(x_ref, t_ref, o_ref):
    a = x_ref.shape[1]
    xs = [pltpu.einshape("rjc->jrc", x_ref[0, s].astype(F32)) for s in range(a)]
    outs = []
    for j in range(SUBLANES):
        x = jnp.concatenate([xs[s][j] for s in range(a)], axis=0).astype(BF16)
        outs.append(jnp.dot(t_ref[j], x, preferred_element_type=F32))
    o = pltpu.einshape("jkc->kjc", jnp.stack(outs, axis=0))
    n1 = o_ref.shape[1]
    o_ref[0] = _pack_complex(o[:n1], o[n1:])


def _pack_complex(re, im):
    r = lax.bitcast_convert_type(re.astype(BF16).astype(F32), jnp.uint32)
    i = lax.bitcast_convert_type(im.astype(BF16).astype(F32), jnp.uint32)
    return (r >> 16) | i


def _unpack_complex(w):
    re = lax.bitcast_convert_type(w << 16, F32)
    im = lax.bitcast_convert_type(w & jnp.uint32(0xFFFF0000), F32)
    return re, im


def _fft_stage1(xv, tmat, n1, n2, d):
    p, a, r, _, _ = xv.shape
    cw = min(d, COL_TILE)
    return pl.pallas_call(
        _fft1_kernel,
        grid=(n2 // SUBLANES, d // cw, p),
        in_specs=[pl.BlockSpec((1, a, r, SUBLANES, cw), lambda j, c, pp: (pp, 0, 0, j, c)),
                  pl.BlockSpec((SUBLANES, 2 * n1, n1), lambda j, c, pp: (j, 0, 0))],
        out_specs=pl.BlockSpec((1, n1, SUBLANES, cw), lambda j, c, pp: (pp, 0, j, c)),
        out_shape=jax.ShapeDtypeStruct((p, n1, n2, d), jnp.uint32),
        compiler_params=_cparams("parallel", "parallel", "parallel"),
        name="fft_stage1",
    )(xv, tmat)


def _fft2_kernel(a_ref, af_ref, ss_ref, mf_ref, mi_ref, o_ref, h_scr):
    n2 = a_ref.shape[2]

    def stage(w):
        re, im = _unpack_complex(w)
        return jnp.dot(mf_ref[...], jnp.concatenate([re, im], axis=0).astype(BF16), preferred_element_type=F32)

    @pl.when(pl.program_id(2) == 0)
    def _():
        for k in range(FFT_K1_BLOCK):
            h_scr[k] = stage(af_ref[0, k]) * lax.rsqrt(ss_ref[...] + RMS_EPS)

    for k in range(FFT_K1_BLOCK):
        xk = stage(a_ref[0, k])
        xr, xi = xk[:n2], xk[n2:]
        hr, hi = h_scr[k, 0:n2, :], h_scr[k, n2:, :]
        z = jnp.concatenate([xr * hr - xi * hi, xr * hi + xi * hr], axis=0).astype(BF16)
        bk = jnp.dot(mi_ref[...], z, preferred_element_type=F32)
        o_ref[0, k] = _pack_complex(bk[:n2], bk[n2:])


def _fft_stage2(av, afv, ss, m_fwd, m_inv, n1, n2, d):
    p = av.shape[0]
    cw = min(d, COL_TILE)
    kb = FFT_K1_BLOCK
    return pl.pallas_call(
        _fft2_kernel,
        grid=(n1 // kb, d // cw, p),
        in_specs=[pl.BlockSpec((1, kb, n2, cw), lambda k, c, pp: (pp, k, 0, c)),
                  pl.BlockSpec((1, kb, n2, cw), lambda k, c, pp: (0, k, 0, c)),
                  pl.BlockSpec((1, cw), lambda k, c, pp: (0, c)),
                  pl.BlockSpec((2 * n2, 2 * n2), lambda k, c, pp: (0, 0)),
                  pl.BlockSpec((2 * n2, 2 * n2), lambda k, c, pp: (0, 0))],
        out_specs=pl.BlockSpec((1, kb, n2, cw), lambda k, c, pp: (pp, k, 0, c)),
        out_shape=jax.ShapeDtypeStruct(av.shape, jnp.uint32),
        scratch_shapes=[pltpu.VMEM((kb, 2 * n2, cw), F32)],
        compiler_params=_cparams("parallel", "parallel", "arbitrary"),
        name="fft_stage2",
    )(av, afv, ss, m_fwd, m_inv)


def _fft3_kernel(b_ref, t_ref, u_ref, x0_ref, bias_ref, o_ref):
    n1 = b_ref.shape[1]
    h = n1 // 2
    bs = _unpack_complex(pltpu.einshape("kjc->jkc", b_ref[0]))
    ys = []
    for j in range(SUBLANES):
        bv = jnp.concatenate([bs[0][j], bs[1][j]], axis=0).astype(BF16)
        ys.append(jnp.dot(t_ref[j], bv, preferred_element_type=F32))
    y = pltpu.einshape("jtc->tjc", jnp.stack(ys, axis=0))
    for s in range(2):
        conv = y[s * h:(s + 1) * h] + u_ref[0, s].astype(F32) * bias_ref[...]
        o_ref[0, s] = (x0_ref[0, s].astype(F32) * conv).astype(o_ref.dtype)


def _fft_stage3(bv, t_inv, uv, x0v, bias, n1, n2, d):
    p = bv.shape[0]
    h = n1 // 2
    cw = min(d, COL_TILE)
    io_spec = pl.BlockSpec((1, 2, h, SUBLANES, cw), lambda j, c, pp: (pp, 0, 0, j, c))
    return pl.pallas_call(
        _fft3_kernel,
        grid=(n2 // SUBLANES, d // cw, p),
        in_specs=[pl.BlockSpec((1, n1, SUBLANES, cw), lambda j, c, pp: (pp, 0, j, c)),
                  pl.BlockSpec((SUBLANES, n1, 2 * n1), lambda j, c, pp: (j, 0, 0)),
                  io_spec, io_spec,
                  pl.BlockSpec((1, cw), lambda j, c, pp: (0, c))],
        out_specs=io_spec,
        out_shape=jax.ShapeDtypeStruct(uv.shape, BF16),
        compiler_params=_cparams("parallel", "parallel", "parallel"),
        name="fft_stage3",
    )(bv, t_inv, uv, x0v, bias)


def _filter_spectrum(l, consts, lp):
    n1, n2 = _fft_split(2 * l)
    d = lp['h_bias'].shape[0]
    full, ss = _hyena_filter(l, lp['hf_w1'], lp['hf_b1'], lp['hf_w2'], lp['hf_b2'], lp['hf_freq'], lp['hf_w3'])
    return _fft_stage1(full.reshape(1, 1, n1, n2, d), consts['t_filt'], n1, n2, d), ss


def _long_conv(u, x0, hf, consts, bias):
    b, l, d = u.shape
    n1, n2 = _fft_split(2 * l)
    p = b // 2
    h = n1 // 2
    uv = u.reshape(p, 2, h, n2, d)
    a = _fft_stage1(uv, consts['t_data'], n1, n2, d)
    bv = _fft_stage2(a, hf[0], hf[1], consts['m_fwd'], consts['m_inv'], n1, n2, d)
    y = _fft_stage3(bv, consts['t_inv'], uv, x0.reshape(p, 2, h, n2, d), bias.reshape(1, d), n1, n2, d)
    return y.reshape(b, l, d)


def _merge_kernel(ym_ref, yh_ref, ga_ref, gb_ref, wm_ref, wh_ref, o_ref):
    a = jnp.dot(ym_ref[0], wm_ref[...], preferred_element_type=F32)
    bb = jnp.dot(yh_ref[0], wh_ref[...], preferred_element_type=F32)
    ga = jax.nn.sigmoid(ga_ref[0].astype(F32))
    gb = jax.nn.sigmoid(gb_ref[0].astype(F32))
    o_ref[0] = (ga * a + gb * bb).astype(o_ref.dtype)


def _merge(ym, yh, p, gate_col0, wm, wh):
    b, l, di = ym.shape
    d = yh.shape[2]
    tm = min(l, 1024)
    tn = min(d, COL_TILE)
    g0 = gate_col0 // tn
    nb = d // tn
    return pl.pallas_call(
        _merge_kernel,
        grid=(b, l // tm, nb),
        in_specs=[pl.BlockSpec((1, tm, di), lambda bb, i, j: (bb, i, 0)),
                  pl.BlockSpec((1, tm, d), lambda bb, i, j: (bb, i, 0)),
                  pl.BlockSpec((1, tm, tn), lambda bb, i, j: (bb, i, g0 + j)),
                  pl.BlockSpec((1, tm, tn), lambda bb, i, j: (bb, i, g0 + nb + j)),
                  pl.BlockSpec((di, tn), lambda bb, i, j: (0, j)),
                  pl.BlockSpec((d, tn), lambda bb, i, j: (0, j))],
        out_specs=pl.BlockSpec((1, tm, tn), lambda bb, i, j: (bb, i, j)),
        out_shape=jax.ShapeDtypeStruct((b, l, d), BF16),
        compiler_params=_cparams("parallel", "parallel", "arbitrary"),
        name="merge",
    )(ym, yh, p, p, wm, wh)


def _resid_mm_kernel(a_ref, w_ref, x_ref, g_ref, o_ref):
    o_ref[0] = x_ref[0] + g_ref[0] * jnp.dot(a_ref[0], w_ref[...], preferred_element_type=F32)


def _resid_mm(a, w, x, gate):
    b, l, k = a.shape
    d = w.shape[1]
    tm = min(l, 1024)
    tn = min(d, COL_TILE)
    return pl.pallas_call(
        _resid_mm_kernel,
        grid=(b, l // tm, d // tn),
        in_specs=[pl.BlockSpec((1, tm, k), lambda bb, i, j: (bb, i, 0)),
                  pl.BlockSpec((k, tn), lambda bb, i, j: (0, j)),
                  pl.BlockSpec((1, tm, tn), lambda bb, i, j: (bb, i, j)),
                  pl.BlockSpec((1, 1, tn), lambda bb, i, j: (bb, 0, j))],
        out_specs=pl.BlockSpec((1, tm, tn), lambda bb, i, j: (bb, i, j)),
        out_shape=jax.ShapeDtypeStruct((b, l, d), F32),
        compiler_params=_cparams("parallel", "parallel", "arbitrary"),
        name="resid_mm",
    )(a, w, x, gate)


def _ffn_kernel(x_ref, g_ref, sh_ref, sc_ref, gt_ref, wg_ref, wu_ref, wd_ref, fg_ref, o_ref, h_scr, *, final_norm):
    j = pl.program_id(2)

    @pl.when(j == 0)
    def _():
        h_scr[...] = _modnorm(x_ref[0], g_ref[...], sh_ref[0], sc_ref[0]).astype(BF16)
        o_ref[...] = jnp.zeros_like(o_ref)

    tm = h_scr.shape[0]
    rows_per_pass = min(tm, 512)
    for r0 in range(0, tm, rows_per_pass):
        h = h_scr[r0:r0 + rows_per_pass, :]
        gg = jnp.dot(h, wg_ref[...], preferred_element_type=F32)
        uu = jnp.dot(h, wu_ref[...], preferred_element_type=F32)
        o_ref[0, r0:r0 + rows_per_pass, :] += jnp.dot((_silu(gg) * uu).astype(BF16), wd_ref[...],
                                                      preferred_element_type=F32)

    @pl.when(j == pl.num_programs(2) - 1)
    def _():
        y = x_ref[0] + gt_ref[0] * o_ref[0]
        if final_norm:
            ms = jnp.mean(y * y, axis=-1, keepdims=True)
            y = y * lax.rsqrt(ms + RMS_EPS) * fg_ref[...]
        o_ref[0] = y


def _ffn(x, g, shift, scale, gate, w_gu, w_down, final_g=None):
    b, l, d = x.shape
    f = w_down.shape[0]
    tm = min(l, 1024)
    fg = jnp.ones((1, d), F32) if final_g is None else final_g.reshape(1, d)
    tf = 256
    nf = f // tf
    vec = pl.BlockSpec((1, 1, d), lambda bb, i, j: (bb, 0, 0))
    return pl.pallas_call(
        functools.partial(_ffn_kernel, final_norm=final_g is not None),
        grid=(b, l // tm, nf),
        in_specs=[pl.BlockSpec((1, tm, d), lambda bb, i, j: (bb, i, 0)),
                  pl.BlockSpec((1, d), lambda bb, i, j: (0, 0)),
                  vec, vec, vec,
                  pl.BlockSpec((d, tf), lambda bb, i, j: (0, j)),
                  pl.BlockSpec((d, tf), lambda bb, i, j: (0, j + nf)),
                  pl.BlockSpec((tf, d), lambda bb, i, j: (j, 0)),
                  pl.BlockSpec((1, d), lambda bb, i, j: (0, 0))],
        out_specs=pl.BlockSpec((1, tm, d), lambda bb, i, j: (bb, i, 0)),
        out_shape=jax.ShapeDtypeStruct((b, l, d), F32),
        scratch_shapes=[pltpu.VMEM((tm, d), BF16)],
        compiler_params=_cparams("parallel", "parallel", "arbitrary"),
        name="ffn",
    )(x, g.reshape(1, d), shift, scale, gate, w_gu, w_gu, w_down, fg)


def _grid_pos_embed(rows, d):
    r, col = jnp.meshgrid(jnp.arange(rows), jnp.arange(GRID_W), indexing='ij')
    quarter = d // 4
    omega = 1.0 / (POS_BASE ** (jnp.arange(quarter, dtype=F32) / quarter))

    def axis_embed(pos):
        ang = pos.reshape(-1)[:, None].astype(F32) * omega[None, :]
        return jnp.concatenate([jnp.sin(ang), jnp.cos(ang)], axis=-1)

    return jnp.concatenate([axis_embed(r), axis_embed(col)], axis=-1).astype(F32)


class _Cols:
    def __init__(self, d):
        self.d = d
        self.di = 2 * d
        self.heads = self.di // M_HEADDIM
        self.xbc = self.di + 2 * M_GROUPS * M_STATE
        self.z0 = 0
        self.gate0 = self.z0 + self.di
        self.hy0 = self.gate0 + 2 * d
        self.xbc0 = self.hy0 + 3 * d
        self.dt0 = self.xbc0 + self.xbc
        self.total = self.dt0 + COL_TILE
        assert 2 * self.heads <= LANES and self.heads % (2 * M_GROUPS) == 0
        for off in (self.gate0, self.hy0, self.xbc0, self.dt0):
            assert off % COL_TILE == 0


def _pack_w_in(w_in, cols):
    o_dt = cols.xbc
    o_z = o_dt + 2 * cols.heads
    o_hy = o_z + cols.di
    o_gate = o_hy + 3 * cols.d
    parts = [w_in[:, o_z:o_hy], w_in[:, o_gate:], w_in[:, o_hy:o_gate], w_in[:, :o_dt], w_in[:, o_dt:o_z]]
    w = jnp.concatenate(parts, axis=1)
    return jnp.pad(w, ((0, 0), (0, cols.total - w.shape[1]))).astype(BF16)


def _pad_lanes(v):
    v = v.reshape(1, -1)
    return jnp.pad(v, ((0, 0), (0, LANES - v.shape[1])))


def _mix_sublayer(x, mods, lp, cols, hf, consts, init_f, init_b):
    h = _prenorm(x, lp['norm1_g'], mods[0], mods[1])
    p, dtp = _inproj(h, lp['w_in'], 0, cols.total)
    xbc = _mconv(p, cols.xbc0, lp['m_conv_w'], lp['m_conv_b'])
    y_f, fin_f = _ssd(xbc, dtp, lp['dtb'], lp['alog'], init_f, cols.heads, False, True)
    ym, fin_b = _ssd(xbc, dtp, lp['dtb'], lp['alog'], init_b, cols.heads, True, True,
                     extra=(y_f, p, cols.z0, lp['dskip'], lp['m_norm_g']))
    x0, u = _hconv(p, cols.hy0, lp['h_conv_w'], lp['h_conv_b'], cols.d)
    yh = _long_conv(u, x0, hf, consts, lp['h_bias'])
    merged = _merge(ym, yh, p, cols.gate0, lp['m_w_out'], lp['h_w_out'])
    return _resid_mm(merged, lp['w_merge_out'], x, mods[2]), fin_f, fin_b


def _ctx_final_states(x, mods, lp, cols, init_f, init_b):
    ncols = cols.total - cols.xbc0
    h = _prenorm(x, lp['norm1_g'], mods[0], mods[1])
    p, dtp = _inproj(h, lp['w_in'], cols.xbc0, ncols)
    xbc = _mconv(p, 0, lp['m_conv_w'], lp['m_conv_b'])
    _, fin_f = _ssd(xbc, dtp, lp['dtb'], lp['alog'], init_f, cols.heads, False, False)
    _, fin_b = _ssd(xbc, dtp, lp['dtb'], lp['alog'], init_b, cols.heads, True, False)
    return fin_f, fin_b


def kernel(x, c, ctx, c_ctx, ada_w, ada_b, norm1_g, w_in, m_conv_w, m_conv_b, m_dt_bias, m_a_log, m_d, m_norm_g,
           m_w_out, h_conv_w, h_conv_b, hf_w1, hf_b1, hf_w2, hf_b2, hf_freq, hf_w3, h_bias, h_w_out, w_merge_out,
           norm2_g, ffn_w_gu, ffn_w_down, final_g):
    b, l, d = x.shape
    lc = ctx.shape[1]
    depth = ada_w.shape[0]
    cols = _Cols(d)
    assert b % 2 == 0 and b + 1 <= SUBLANES and l % M_CHUNK == 0 and lc % M_CHUNK == 0

    cvecs = jnp.concatenate([c, c_ctx[None], jnp.zeros((SUBLANES - b - 1, d), F32)], axis=0)
    mods_all = _mods(cvecs, ada_w, ada_b)
    consts_l = _dft_consts(*_fft_split(2 * l))
    consts_c = _dft_consts(*_fft_split(2 * lc))

    x_l = _add_pos(x, _grid_pos_embed(l // GRID_W, d))
    x_c = ctx
    zero_state = jnp.zeros((b, M_GROUPS, M_STATE, cols.di // M_GROUPS), F32)
    for i in range(depth):
        lp = dict(norm1_g=norm1_g[i], w_in=_pack_w_in(w_in[i], cols), m_conv_w=m_conv_w[i], m_conv_b=m_conv_b[i],
                  dtb=_pad_lanes(m_dt_bias[i]), alog=_pad_lanes(m_a_log[i]),
                  dskip=jnp.repeat(m_d[i], M_HEADDIM).reshape(1, cols.di), m_norm_g=m_norm_g[i].reshape(1, cols.di),
                  m_w_out=m_w_out[i].astype(BF16), h_conv_w=h_conv_w[i], h_conv_b=h_conv_b[i],
                  hf_w1=hf_w1[i], hf_b1=hf_b1[i], hf_w2=hf_w2[i], hf_b2=hf_b2[i], hf_freq=hf_freq[i], hf_w3=hf_w3[i],
                  h_bias=h_bias[i], h_w_out=h_w_out[i].astype(BF16), w_merge_out=w_merge_out[i].astype(BF16),
                  norm2_g=norm2_g[i], ffn_w_gu=ffn_w_gu[i].astype(BF16), ffn_w_down=ffn_w_down[i].astype(BF16))
        m = mods_all[i].reshape(SUBLANES, ADA_CHUNKS, d)
        mods_l = [m[:b, k][:, None, :] for k in range(ADA_CHUNKS)]
        mods_c = [jnp.broadcast_to(m[b, k][None, None, :], (b, 1, d)) for k in range(ADA_CHUNKS)]
        if i < depth - 1:
            hf_c = _filter_spectrum(lc, consts_c, lp)
            x_c_mixed, fin_f, fin_b = _mix_sublayer(x_c, mods_c, lp, cols, hf_c, consts_c, zero_state, zero_state)
        else:
            fin_f, fin_b = _ctx_final_states(x_c, mods_c, lp, cols, zero_state, zero_state)
        hf_l = _filter_spectrum(l, consts_l, lp)
        x_l, _, _ = _mix_sublayer(x_l, mods_l, lp, cols, hf_l, consts_l, fin_f, fin_b)
        x_l = _ffn(x_l, lp['norm2_g'], mods_l[3], mods_l[4], mods_l[5], lp['ffn_w_gu'], lp['ffn_w_down'],
                   final_g=final_g if i == depth - 1 else None)
        if i < depth - 1:
            x_c = _ffn(x_c_mixed, lp['norm2_g'], mods_c[3], mods_c[4], mods_c[5], lp['ffn_w_gu'], lp['ffn_w_down'])
    return x_l
```

```python
import functools
import math

import jax
import jax.numpy as jnp
from jax import lax
from jax.experimental import pallas as pl
from jax.experimental.pallas import tpu as pltpu

F32 = jnp.float32
BF16 = jnp.bfloat16
HIGHEST = lax.Precision.HIGHEST

RMS_EPS = 1e-6
GRID_W = 64
POS_BASE = 10000.0
ADA_CHUNKS = 6
M_HEADDIM = 64
M_GROUPS = 8
M_STATE = 128
M_CONV = 5
M_CHUNK = 128
H_SHORT = 3
H_EMB = 33
H_DECAY_TARGET = 1e-2
H_FAST_PCT = 0.3
H_SLOW_PCT = 1.5

LANES = 128
SUBLANES = 8
HALO = 16
VMEM_LIMIT_BYTES = 56 * 1024 * 1024

COL_TILE = 512
FEAT_COLS = 40
FFT_K1_BLOCK = 4
SSD_BATCH_ROWS = 2


def _cparams(*sem):
    return pltpu.CompilerParams(dimension_semantics=sem, vmem_limit_bytes=VMEM_LIMIT_BYTES)


def _silu(v):
    h = 0.5 * v
    return h + h * jnp.tanh(h)


def _round_up(a, m):
    return -(-a // m) * m


def _mods_kernel(c_ref, w_ref, b_ref, o_ref):
    s = _silu(c_ref[...])
    o_ref[0] = jnp.dot(s, w_ref[0], preferred_element_type=F32, precision=HIGHEST) + b_ref[0]


def _mods(cvecs, ada_w, ada_b):
    depth, d, n = ada_w.shape
    tn = min(n, 1024)
    return pl.pallas_call(
        _mods_kernel,
        grid=(depth, n // tn),
        in_specs=[pl.BlockSpec((SUBLANES, d), lambda i, j: (0, 0)),
                  pl.BlockSpec((1, d, tn), lambda i, j: (i, 0, j)),
                  pl.BlockSpec((1, 1, tn), lambda i, j: (i, 0, j))],
        out_specs=pl.BlockSpec((1, SUBLANES, tn), lambda i, j: (i, 0, j)),
        out_shape=jax.ShapeDtypeStruct((depth, SUBLANES, n), F32),
        compiler_params=_cparams("parallel", "parallel"),
        name="mods",
    )(cvecs, ada_w, ada_b.reshape(depth, 1, n))


def _add_kernel(x_ref, p_ref, o_ref):
    o_ref[0] = x_ref[0] + p_ref[...]


def _add_pos(x, pos):
    b, l, d = x.shape
    tl = min(l, 1024)
    return pl.pallas_call(
        _add_kernel,
        grid=(l // tl, b),
        in_specs=[pl.BlockSpec((1, tl, d), lambda i, bb: (bb, i, 0)),
                  pl.BlockSpec((tl, d), lambda i, bb: (i, 0))],
        out_specs=pl.BlockSpec((1, tl, d), lambda i, bb: (bb, i, 0)),
        out_shape=jax.ShapeDtypeStruct(x.shape, F32),
        compiler_params=_cparams("parallel", "parallel"),
        name="add_pos",
    )(x, pos)


def _modnorm(x, g, shift, scale):
    ms = jnp.mean(x * x, axis=-1, keepdims=True)
    y = x * lax.rsqrt(ms + RMS_EPS) * g
    return y * (1.0 + scale) + shift


INPROJ_ROWS = 2048
CONV_ROWS = 256


def _prenorm_kernel(prev_ref, x_ref, next_ref, g_ref, sh_ref, sc_ref, o_ref):
    k = pl.program_id(2)
    rows = x_ref.shape[1]
    tl = o_ref.shape[2] - 2 * HALO
    norm = lambda v: _modnorm(v, g_ref[...], sh_ref[0], sc_ref[0]).astype(o_ref.dtype)
    o_ref[0, 0, pl.ds(pl.multiple_of(HALO + k * rows, HALO), rows), :] = norm(x_ref[0])

    @pl.when(k == 0)
    def _():
        o_ref[0, 0, 0:HALO, :] = norm(prev_ref[0])

    @pl.when(k == pl.num_programs(2) - 1)
    def _():
        o_ref[0, 0, HALO + tl:, :] = norm(next_ref[0])


def _prenorm(x, g, shift, scale):
    b, l, d = x.shape
    tl = min(l, INPROJ_ROWS)
    rows = min(tl, 512)
    nk = tl // rows
    rpb = tl // HALO
    last = l // HALO - 1
    vec = pl.BlockSpec((1, 1, d), lambda bb, i, k: (bb, 0, 0))
    return pl.pallas_call(
        _prenorm_kernel,
        grid=(b, l // tl, nk),
        in_specs=[pl.BlockSpec((1, HALO, d), lambda bb, i, k: (bb, jnp.maximum(i * rpb - 1, 0), 0)),
                  pl.BlockSpec((1, rows, d), lambda bb, i, k: (bb, i * nk + k, 0)),
                  pl.BlockSpec((1, HALO, d), lambda bb, i, k: (bb, jnp.minimum((i + 1) * rpb, last), 0)),
                  pl.BlockSpec((1, d), lambda bb, i, k: (0, 0)), vec, vec],
        out_specs=pl.BlockSpec((1, 1, tl + 2 * HALO, d), lambda bb, i, k: (bb, i, 0, 0)),
        out_shape=jax.ShapeDtypeStruct((b, l // tl, tl + 2 * HALO, d), BF16),
        compiler_params=_cparams("parallel", "parallel", "arbitrary"),
        name="prenorm",
    )(x, x, x, g.reshape(1, d), shift, scale)


def _inproj_kernel(*refs, n_plain, n_conv):
    if n_plain:
        h_ref, w_ref, cw_ref, cb_ref, o_ref, xbc_ref, dt_ref, acc_scr = refs
    else:
        h_ref, w_ref, cw_ref, cb_ref, xbc_ref, dt_ref, acc_scr = refs
    i = pl.program_id(1)
    k = pl.program_id(2) - n_plain
    tm = dt_ref.shape[1]
    project = lambda: jnp.dot(h_ref[0, 0], w_ref[...], preferred_element_type=F32)

    def project_into(dst):
        dst[...] = project()
        dst[0:HALO, :] = jnp.where(i > 0, dst[0:HALO, :], 0.0)
        dst[HALO + tm:, :] = jnp.where(i < pl.num_programs(1) - 1, dst[HALO + tm:, :], 0.0)

    def conv_from(src):
        pad = M_CONV // 2
        rows = min(tm, CONV_ROWS)
        for r0 in range(0, tm, rows):
            ext = src[r0:r0 + rows + 2 * HALO, :]
            acc = None
            for tap in range(M_CONV):
                shifted = ext if tap == pad else pltpu.roll(ext, (pad - tap) % (rows + 2 * HALO), axis=0)
                term = shifted[HALO:HALO + rows] * cw_ref[tap:tap + 1, :]
                acc = term if acc is None else acc + term
            xbc_ref[0, r0:r0 + rows, :] = _silu(acc + cb_ref[...]).astype(xbc_ref.dtype)

    if n_plain:
        @pl.when(k < 0)
        def _():
            o_ref[0] = project()[HALO:HALO + tm].astype(o_ref.dtype)

    @pl.when(jnp.logical_and(k >= 0, k < n_conv))
    def _():
        project_into(acc_scr)
        conv_from(acc_scr)

    @pl.when(k == n_conv)
    def _():
        dt_ref[0] = project()[HALO:HALO + tm, 0:LANES]


def _inproj(h, w, col0, n_plain, conv_w, conv_b):
    b, nt, tmh, d = h.shape
    tm = tmh - 2 * HALO
    l = nt * tm
    tn = COL_TILE
    j0 = col0 // tn
    n_conv = conv_w.shape[1] // tn
    nj = n_plain + n_conv + 1
    conv_idx = lambda j: jnp.clip(j - n_plain, 0, n_conv - 1)
    out_specs = [pl.BlockSpec((1, tm, tn), lambda bb, i, j: (bb, i, conv_idx(j))),
                 pl.BlockSpec((1, tm, LANES), lambda bb, i, j: (bb, i, 0))]
    out_shape = [jax.ShapeDtypeStruct((b, l, n_conv * tn), BF16), jax.ShapeDtypeStruct((b, l, LANES), F32)]
    if n_plain:
        out_specs.insert(0, pl.BlockSpec((1, tm, tn), lambda bb, i, j: (bb, i, jnp.minimum(j, n_plain - 1))))
        out_shape.insert(0, jax.ShapeDtypeStruct((b, l, n_plain * tn), BF16))
    outs = pl.pallas_call(
        functools.partial(_inproj_kernel, n_plain=n_plain, n_conv=n_conv),
        grid=(b, nt, nj),
        in_specs=[pl.BlockSpec((1, 1, tmh, d), lambda bb, i, j: (bb, i, 0, 0)),
                  pl.BlockSpec((d, tn), lambda bb, i, j: (0, j + j0)),
                  pl.BlockSpec((M_CONV, tn), lambda bb, i, j: (0, conv_idx(j))),
                  pl.BlockSpec((1, tn), lambda bb, i, j: (0, conv_idx(j)))],
        out_specs=out_specs,
        out_shape=out_shape,
        scratch_shapes=[pltpu.VMEM((tmh, tn), F32)],
        compiler_params=_cparams("parallel", "parallel", "arbitrary"),
        name="inproj",
    )(h, w, conv_w, conv_b.reshape(1, -1))
    return outs if n_plain else [None] + list(outs)


def _conv_taps(prev_ref, main_ref, next_ref, w_ref, b_ref, ktaps):
    i = pl.program_id(2)
    nblk = pl.num_programs(2)
    tl = main_ref.shape[1]
    pad = ktaps // 2
    prev = jnp.where(i > 0, prev_ref[0].astype(F32), 0.0)
    nxt = jnp.where(i < nblk - 1, next_ref[0].astype(F32), 0.0)
    ext = jnp.concatenate([prev, main_ref[0].astype(F32), nxt], axis=0)
    rows = tl + 2 * HALO
    acc = None
    for j in range(ktaps):
        shifted = ext if j == pad else pltpu.roll(ext, (pad - j) % rows, axis=0)
        term = shifted[HALO:HALO + tl] * w_ref[j:j + 1, :]
        acc = term if acc is None else acc + term
    return acc + b_ref[...]


def _conv_specs(tl, cw, l, col_block0):
    rpb = tl // HALO
    last = l // HALO - 1
    return [
        pl.BlockSpec((1, HALO, cw), lambda bb, c, i: (bb, jnp.maximum(i * rpb - 1, 0), c + col_block0)),
        pl.BlockSpec((1, tl, cw), lambda bb, c, i: (bb, i, c + col_block0)),
        pl.BlockSpec((1, HALO, cw), lambda bb, c, i: (bb, jnp.minimum((i + 1) * rpb, last), c + col_block0)),
    ]


def _hconv_kernel(p0, m0, n0, p1, m1, n1, p2, m2, n2, w0, b0, w1, b1, w2, b2, x0_ref, u_ref):
    x0_ref[0] = _conv_taps(p0, m0, n0, w0, b0, H_SHORT).astype(x0_ref.dtype)
    x1 = _conv_taps(p1, m1, n1, w1, b1, H_SHORT)
    v = _conv_taps(p2, m2, n2, w2, b2, H_SHORT)
    u_ref[0] = (x1 * v).astype(u_ref.dtype)


def _hconv(p, col0, w, bias, d):
    b, l, _ = p.shape
    tl = min(l, 512)
    cw = min(d, COL_TILE)
    nb = d // cw
    specs, wspecs = [], []
    for s in range(3):
        specs += _conv_specs(tl, cw, l, col0 // cw + s * nb)
        wspecs += [pl.BlockSpec((H_SHORT, cw), functools.partial(lambda bb, c, i, s: (0, c + s * nb), s=s)),
                   pl.BlockSpec((1, cw), functools.partial(lambda bb, c, i, s: (0, c + s * nb), s=s))]
    bias2 = bias.reshape(1, 3 * d)
    out_spec = pl.BlockSpec((1, tl, cw), lambda bb, c, i: (bb, i, c))
    return pl.pallas_call(
        _hconv_kernel,
        grid=(b, nb, l // tl),
        in_specs=specs + wspecs,
        out_specs=[out_spec, out_spec],
        out_shape=[jax.ShapeDtypeStruct((b, l, d), BF16)] * 2,
        compiler_params=_cparams("parallel", "parallel", "arbitrary"),
        name="hconv",
    )(p, p, p, p, p, p, p, p, p, w, bias2, w, bias2, w, bias2)


def _ssd_kernel(*refs, reverse, with_output, epilogue, heads):
    if epilogue:
        (xbc_ref, dt_ref, dtb_ref, alog_ref, init_ref, yf_ref, z_ref, dsk_ref, ng_ref,
         y_ref, fin_ref, s_scr) = refs
    elif with_output:
        xbc_ref, dt_ref, dtb_ref, alog_ref, init_ref, y_ref, fin_ref, s_scr = refs
    else:
        xbc_ref, dt_ref, dtb_ref, alog_ref, init_ref, fin_ref, s_scr = refs
    t = M_CHUNK
    n = M_STATE
    assert n == t
    di = heads * M_HEADDIM
    gw = di // M_GROUPS
    rpg = heads // M_GROUPS
    lane0 = heads if reverse else 0
    edge = 0 if reverse else t - 1
    c = pl.program_id(1)

    nb = xbc_ref.shape[0]

    @pl.when(c == 0)
    def _():
        s_scr[...] = init_ref[...]

    row = lax.broadcasted_iota(jnp.int32, (t, t), 0)
    col = lax.broadcasted_iota(jnp.int32, (t, t), 1)
    mask = (row <= col) if reverse else (row >= col)
    lane = lax.broadcasted_iota(jnp.int32, (1, LANES), 1)
    lo = lane < M_HEADDIM
    mlo = jnp.where(lo, 1.0, 0.0).astype(BF16)
    mhi = jnp.where(lo, 0.0, 1.0).astype(BF16)

    def chunk_decays(bi):
        x = dt_ref[bi] + dtb_ref[...]
        dt = jnp.maximum(x, 0.0) + jnp.log1p(jnp.exp(-jnp.abs(x)))
        dta = dt * (-jnp.exp(alog_ref[...]))
        acs = jnp.dot(mask.astype(F32), dta, preferred_element_type=F32, precision=HIGHEST)
        acs_t = acs.T
        dt_t = dt.T
        arow_dt = acs_t - jnp.log(dt_t)
        w_t = jnp.exp(acs_t[:, edge:edge + 1] - acs_t) * dt_t
        dec = jnp.exp(acs[edge:edge + 1, :])
        return acs, arow_dt, w_t, dec

    def group(bi, g, acs, arow_dt, w_t, dec):
        bm = xbc_ref[bi, :, di + g * n:di + (g + 1) * n].astype(F32)
        cm = xbc_ref[bi, :, di + M_GROUPS * n + g * n:di + M_GROUPS * n + (g + 1) * n].astype(F32)
        bt = bm.T
        if with_output:
            cb = jnp.dot(cm.astype(BF16), bt.astype(BF16), preferred_element_type=F32)
        y_parts = []
        for q in range(rpg // 2):
            c0 = g * gw + q * LANES
            xp = xbc_ref[bi, :, c0:c0 + LANES]
            sp = s_scr[bi, g, :, q * LANES:(q + 1) * LANES]
            xlo = xp * mlo
            xhi = xp * mhi
            lhs, btw, decs = [], [], []
            for r in range(2):
                hl = lane0 + g * rpg + 2 * q + r
                if with_output:
                    acol = jnp.broadcast_to(acs[:, hl:hl + 1], (t, t))
                    dk = jnp.exp(jnp.where(mask, acol - arow_dt[hl:hl + 1, :], -jnp.inf))
                    lhs.append((cb * dk).astype(BF16))
                    lhs.append((cm * jnp.exp(acol)).astype(BF16))
                btw.append((bt * w_t[hl:hl + 1, :]).astype(BF16))
                decs.append(jnp.broadcast_to(dec[:, hl:hl + 1], (1, LANES)))
            if with_output:
                slo = jnp.where(lo, sp, 0.0).astype(BF16)
                shi = jnp.where(lo, 0.0, sp).astype(BF16)
                y_parts.append(jnp.dot(jnp.concatenate(lhs, axis=1), jnp.concatenate([xlo, slo, xhi, shi], axis=0),
                                       preferred_element_type=F32))
            ds = jnp.dot(jnp.concatenate(btw, axis=1), jnp.concatenate([xlo, xhi], axis=0),
                         preferred_element_type=F32)
            s_scr[bi, g, :, q * LANES:(q + 1) * LANES] = jnp.where(lo, decs[0], decs[1]) * sp + ds
        if with_output:
            yg = jnp.concatenate(y_parts, axis=1) if len(y_parts) > 1 else y_parts[0]
            sl = slice(g * gw, (g + 1) * gw)
            if epilogue:
                yg = yg + yf_ref[bi, :, sl].astype(F32)
                v = (yg + xbc_ref[bi, :, sl].astype(F32) * dsk_ref[:, sl]) * _silu(z_ref[bi, :, sl].astype(F32))
                ms = jnp.mean(v * v, axis=-1, keepdims=True)
                y_ref[bi, :, sl] = (v * lax.rsqrt(ms + RMS_EPS) * ng_ref[:, sl]).astype(y_ref.dtype)
            else:
                y_ref[bi, :, sl] = yg.astype(y_ref.dtype)

    decays = [chunk_decays(bi) for bi in range(nb)]
    for g in range(M_GROUPS):
        for bi in range(nb):
            group(bi, g, *decays[bi])

    @pl.when(c == pl.num_programs(1) - 1)
    def _():
        fin_ref[...] = s_scr[...]


def _ssd(xbc, dtp, dtb, alog, init, heads, reverse, with_output, extra=None):
    b, l, xw = xbc.shape
    t = M_CHUNK
    nc = l // t
    di = heads * M_HEADDIM
    gw = di // M_GROUPS
    epilogue = extra is not None
    cidx = (lambda c: nc - 1 - c) if reverse else (lambda c: c)
    nb = SSD_BATCH_ROWS if b % SSD_BATCH_ROWS == 0 else 1
    state_spec = pl.BlockSpec((nb, M_GROUPS, M_STATE, gw), lambda bb, c: (bb, 0, 0, 0))
    in_specs = [pl.BlockSpec((nb, t, xw), lambda bb, c: (bb, cidx(c), 0)),
                pl.BlockSpec((nb, t, LANES), lambda bb, c: (bb, cidx(c), 0)),
                pl.BlockSpec((1, LANES), lambda bb, c: (0, 0)),
                pl.BlockSpec((1, LANES), lambda bb, c: (0, 0)),
                state_spec]
    args = [xbc, dtp, dtb, alog, init]
    out_specs, out_shape = [], []
    if epilogue:
        yf, p, z_col0, dsk, ng = extra
        in_specs += [pl.BlockSpec((nb, t, di), lambda bb, c: (bb, cidx(c), 0)),
                     pl.BlockSpec((nb, t, di), lambda bb, c: (bb, cidx(c), z_col0 // di)),
                     pl.BlockSpec((1, di), lambda bb, c: (0, 0)),
                     pl.BlockSpec((1, di), lambda bb, c: (0, 0))]
        args += [yf, p, dsk, ng]
    if with_output:
        out_specs.append(pl.BlockSpec((nb, t, di), lambda bb, c: (bb, cidx(c), 0)))
        out_shape.append(jax.ShapeDtypeStruct((b, l, di), BF16))
    out_specs.append(state_spec)
    out_shape.append(jax.ShapeDtypeStruct((b, M_GROUPS, M_STATE, gw), F32))
    outs = pl.pallas_call(
        functools.partial(_ssd_kernel, reverse=reverse, with_output=with_output, epilogue=epilogue, heads=heads),
        grid=(b // nb, nc),
        in_specs=in_specs,
        out_specs=out_specs,
        out_shape=out_shape,
        scratch_shapes=[pltpu.VMEM((nb, M_GROUPS, M_STATE, gw), F32)],
        compiler_params=_cparams("parallel", "arbitrary"),
        name="ssd_bwd" if reverse else "ssd_fwd",
    )(*args)
    return (outs[0], outs[1]) if with_output else (None, outs[0])


def _filt_kernel(f_ref, w1_ref, b1_ref, w2_ref, b2_ref, fr_ref, w3_ref, dl_ref, o_ref, ss_ref):
    f = f_ref[...]
    fr = fr_ref[...]
    h = jnp.sin(fr * (jnp.dot(f, w1_ref[...], preferred_element_type=F32, precision=HIGHEST) + b1_ref[...]))
    h = jnp.sin(fr * (jnp.dot(h, w2_ref[...], preferred_element_type=F32, precision=HIGHEST) + b2_ref[...]))
    filt = jnp.dot(h, w3_ref[...], preferred_element_type=F32, precision=HIGHEST)
    tcol = f[:, H_EMB:H_EMB + 1]
    mcol = f[:, H_EMB + 1:H_EMB + 2]
    out = filt * jnp.exp(-tcol * dl_ref[...]) * mcol
    o_ref[...] = out

    @pl.when(pl.program_id(1) == 0)
    def _():
        ss_ref[...] = jnp.zeros_like(ss_ref)

    ss_ref[...] += jnp.sum(out * out, axis=0, keepdims=True)


def _filter_features(l):
    t = jnp.linspace(0.0, 1.0, l, dtype=F32)[:, None]
    ang = (2.0 * math.pi / l) * jnp.arange(l, dtype=F32)[:, None]
    nb = (H_EMB - 1) // 2
    bands = jnp.linspace(1e-4, nb - 1, nb, dtype=F32)[None, :]
    feats = jnp.concatenate([t, jnp.cos(bands * ang), -jnp.sin(bands * ang), t, jnp.ones_like(t)], axis=-1)
    rev = jnp.concatenate([feats[0:1] * 0.0, jnp.flip(feats[1:], axis=0)], axis=0)
    full = jnp.concatenate([feats, rev], axis=0)
    return jnp.pad(full, ((0, 0), (0, FEAT_COLS - full.shape[1])))


def _hyena_filter(l, w1, b1, w2, b2, freq, w3):
    hid = w2.shape[0]
    d = w3.shape[1] // 2
    feats = _filter_features(l)
    w1p = jnp.pad(w1, ((0, FEAT_COLS - w1.shape[0]), (0, 0)))
    deltas = jnp.abs(jnp.linspace(math.log(H_DECAY_TARGET) / H_SLOW_PCT, math.log(H_DECAY_TARGET) / H_FAST_PCT,
                                  d, dtype=F32))[None, :]
    tl = min(l, 512)
    cw = d
    nb = d // cw
    nrow = l // tl
    small = lambda shape: pl.BlockSpec(shape, lambda c, i: (0, 0))
    return pl.pallas_call(
        _filt_kernel,
        grid=(nb, 2 * nrow),
        in_specs=[pl.BlockSpec((tl, FEAT_COLS), lambda c, i: (i, 0)),
                  small((FEAT_COLS, hid)), small((1, hid)), small((hid, hid)), small((1, hid)), small((1, hid)),
                  pl.BlockSpec((hid, cw), lambda c, i: (0, c + (i // nrow) * nb)),
                  pl.BlockSpec((1, cw), lambda c, i: (0, c))],
        out_specs=[pl.BlockSpec((tl, cw), lambda c, i: (i, c)),
                   pl.BlockSpec((1, cw), lambda c, i: (0, c))],
        out_shape=[jax.ShapeDtypeStruct((2 * l, d), F32), jax.ShapeDtypeStruct((1, d), F32)],
        compiler_params=_cparams("parallel", "arbitrary"),
        name="hyena_filter",
    )(feats, w1p, b1.reshape(1, hid), w2, b2.reshape(1, hid), freq.reshape(1, hid), w3, deltas)


def _fft_split(n):
    n1 = 1 << ((n.bit_length() - 1 + 1) // 2)
    return n1, n // n1


def _dft_consts(n1, n2):
    n = n1 * n2
    h = n1 // 2
    k1 = jnp.arange(n1, dtype=jnp.int32)
    t1 = jnp.arange(n1, dtype=jnp.int32)
    t2 = jnp.arange(n2, dtype=jnp.int32)
    idx = (k1[None, :, None] * (n2 * t1[None, None, :] + t2[:, None, None])) % n
    ang = idx.astype(F32) * (2.0 * math.pi / n)
    cs, sn = jnp.cos(ang), jnp.sin(ang)
    ch, sh = cs[..., :h], sn[..., :h]
    t_data = jnp.concatenate([jnp.concatenate([ch, sh], -1), jnp.concatenate([-sh, ch], -1)], axis=1)
    t_filt = jnp.concatenate([cs, -sn], axis=1)
    ct, st = jnp.swapaxes(ch, 1, 2), jnp.swapaxes(sh, 1, 2)
    t_inv = jnp.concatenate([jnp.concatenate([ct, -st], -1), jnp.concatenate([st, ct], -1)], axis=1) / n
    k2 = jnp.arange(n2, dtype=jnp.int32)
    ang2 = ((k2[:, None] * k2[None, :]) % n2).astype(F32) * (2.0 * math.pi / n2)
    c2, s2 = jnp.cos(ang2), jnp.sin(ang2)
    m_fwd = jnp.concatenate([jnp.concatenate([c2, s2], -1), jnp.concatenate([-s2, c2], -1)], axis=0)
    m_inv = jnp.concatenate([jnp.concatenate([c2, -s2], -1), jnp.concatenate([s2, c2], -1)], axis=0)
    return dict(t_data=t_data.astype(BF16), t_filt=t_filt.astype(BF16), t_inv=t_inv.astype(BF16),
                m_fwd=m_fwd.astype(BF16), m_inv=m_inv.astype(BF16))


def _fft1_kernel(x_ref, t_ref, o_ref):
    a = x_ref.shape[1]
    xs = [pltpu.einshape("rjc->jrc", x_ref[0, s].astype(F32)) for s in range(a)]
    outs = []
    for j in range(SUBLANES):
        x = jnp.concatenate([xs[s][j] for s in range(a)], axis=0).astype(BF16)
        outs.append(jnp.dot(t_ref[j], x, preferred_element_type=F32))
    o = pltpu.einshape("jkc->kjc", jnp.stack(outs, axis=0))
    n1 = o_ref.shape[1]
    o_ref[0] = _pack_complex(o[:n1], o[n1:])


def _pack_complex(re, im):
    r = lax.bitcast_convert_type(re.astype(BF16).astype(F32), jnp.uint32)
    i = lax.bitcast_convert_type(im.astype(BF16).astype(F32), jnp.uint32)
    return (r >> 16) | i


def _unpack_complex(w):
    re = lax.bitcast_convert_type(w << 16, F32)
    im = lax.bitcast_convert_type(w & jnp.uint32(0xFFFF0000), F32)
    return re, im


def _fft_stage1(xv, tmat, n1, n2, d):
    p, a, r, _, _ = xv.shape
    cw = min(d, COL_TILE)
    return pl.pallas_call(
        _fft1_kernel,
        grid=(n2 // SUBLANES, d // cw, p),
        in_specs=[pl.BlockSpec((1, a, r, SUBLANES, cw), lambda j, c, pp: (pp, 0, 0, j, c)),
                  pl.BlockSpec((SUBLANES, 2 * n1, n1), lambda j, c, pp: (j, 0, 0))],
        out_specs=pl.BlockSpec((1, n1, SUBLANES, cw), lambda j, c, pp: (pp, 0, j, c)),
        out_shape=jax.ShapeDtypeStruct((p, n1, n2, d), jnp.uint32),
        compiler_params=_cparams("parallel", "parallel", "parallel"),
        name="fft_stage1",
    )(xv, tmat)


def _fft2_kernel(a_ref, af_ref, ss_ref, mf_ref, mi_ref, o_ref, h_scr):
    n2 = a_ref.shape[2]

    def stage(w):
        re, im = _unpack_complex(w)
        return jnp.dot(mf_ref[...], jnp.concatenate([re, im], axis=0).astype(BF16), preferred_element_type=F32)

    @pl.when(pl.program_id(2) == 0)
    def _():
        for k in range(FFT_K1_BLOCK):
            h_scr[k] = stage(af_ref[0, k]) * lax.rsqrt(ss_ref[...] + RMS_EPS)

    for k in range(FFT_K1_BLOCK):
        xk = stage(a_ref[0, k])
        xr, xi = xk[:n2], xk[n2:]
        hr, hi = h_scr[k, 0:n2, :], h_scr[k, n2:, :]
        z = jnp.concatenate([xr * hr - xi * hi, xr * hi + xi * hr], axis=0).astype(BF16)
        bk = jnp.dot(mi_ref[...], z, preferred_element_type=F32)
        o_ref[0, k] = _pack_complex(bk[:n2], bk[n2:])


def _fft_stage2(av, afv, ss, m_fwd, m_inv, n1, n2, d):
    p = av.shape[0]
    cw = min(d, COL_TILE)
    kb = FFT_K1_BLOCK
    return pl.pallas_call(
        _fft2_kernel,
        grid=(n1 // kb, d // cw, p),
        in_specs=[pl.BlockSpec((1, kb, n2, cw), lambda k, c, pp: (pp, k, 0, c)),
                  pl.BlockSpec((1, kb, n2, cw), lambda k, c, pp: (0, k, 0, c)),
                  pl.BlockSpec((1, cw), lambda k, c, pp: (0, c)),
                  pl.BlockSpec((2 * n2, 2 * n2), lambda k, c, pp: (0, 0)),
                  pl.BlockSpec((2 * n2, 2 * n2), lambda k, c, pp: (0, 0))],
        out_specs=pl.BlockSpec((1, kb, n2, cw), lambda k, c, pp: (pp, k, 0, c)),
        out_shape=jax.ShapeDtypeStruct(av.shape, jnp.uint32),
        scratch_shapes=[pltpu.VMEM((kb, 2 * n2, cw), F32)],
        compiler_params=_cparams("parallel", "parallel", "arbitrary"),
        name="fft_stage2",
    )(av, afv, ss, m_fwd, m_inv)


def _fft3_kernel(b_ref, t_ref, u_ref, x0_ref, bias_ref, o_ref):
    n1 = b_ref.shape[1]
    h = n1 // 2
    bs = _unpack_complex(pltpu.einshape("kjc->jkc", b_ref[0]))
    ys = []
    for j in range(SUBLANES):
        bv = jnp.concatenate([bs[0][j], bs[1][j]], axis=0).astype(BF16)
        ys.append(jnp.dot(t_ref[j], bv, preferred_element_type=F32))
    y = pltpu.einshape("jtc->tjc", jnp.stack(ys, axis=0))
    for s in range(2):
        conv = y[s * h:(s + 1) * h] + u_ref[0, s].astype(F32) * bias_ref[...]
        o_ref[0, s] = (x0_ref[0, s].astype(F32) * conv).astype(o_ref.dtype)


def _fft_stage3(bv, t_inv, uv, x0v, bias, n1, n2, d):
    p = bv.shape[0]
    h = n1 // 2
    cw = min(d, COL_TILE)
    io_spec = pl.BlockSpec((1, 2, h, SUBLANES, cw), lambda j, c, pp: (pp, 0, 0, j, c))
    return pl.pallas_call(
        _fft3_kernel,
        grid=(n2 // SUBLANES, d // cw, p),
        in_specs=[pl.BlockSpec((1, n1, SUBLANES, cw), lambda j, c, pp: (pp, 0, j, c)),
                  pl.BlockSpec((SUBLANES, n1, 2 * n1), lambda j, c, pp: (j, 0, 0)),
                  io_spec, io_spec,
                  pl.BlockSpec((1, cw), lambda j, c, pp: (0, c))],
        out_specs=io_spec,
        out_shape=jax.ShapeDtypeStruct(uv.shape, BF16),
        compiler_params=_cparams("parallel", "parallel", "parallel"),
        name="fft_stage3",
    )(bv, t_inv, uv, x0v, bias)


def _filter_spectrum(l, consts, lp):
    n1, n2 = _fft_split(2 * l)
    d = lp['h_bias'].shape[0]
    full, ss = _hyena_filter(l, lp['hf_w1'], lp['hf_b1'], lp['hf_w2'], lp['hf_b2'], lp['hf_freq'], lp['hf_w3'])
    return _fft_stage1(full.reshape(1, 1, n1, n2, d), consts['t_filt'], n1, n2, d), ss


def _long_conv(u, x0, hf, consts, bias):
    b, l, d = u.shape
    n1, n2 = _fft_split(2 * l)
    p = b // 2
    h = n1 // 2
    uv = u.reshape(p, 2, h, n2, d)
    a = _fft_stage1(uv, consts['t_data'], n1, n2, d)
    bv = _fft_stage2(a, hf[0], hf[1], consts['m_fwd'], consts['m_inv'], n1, n2, d)
    y = _fft_stage3(bv, consts['t_inv'], uv, x0.reshape(p, 2, h, n2, d), bias.reshape(1, d), n1, n2, d)
    return y.reshape(b, l, d)


def _merge_kernel(ym_ref, yh_ref, ga_ref, gb_ref, wm_ref, wh_ref, o_ref):
    a = jnp.dot(ym_ref[0], wm_ref[...], preferred_element_type=F32)
    bb = jnp.dot(yh_ref[0], wh_ref[...], preferred_element_type=F32)
    ga = jax.nn.sigmoid(ga_ref[0].astype(F32))
    gb = jax.nn.sigmoid(gb_ref[0].astype(F32))
    o_ref[0] = (ga * a + gb * bb).astype(o_ref.dtype)


def _merge(ym, yh, p, gate_col0, wm, wh):
    b, l, di = ym.shape
    d = yh.shape[2]
    tm = min(l, 1024)
    tn = min(d, COL_TILE)
    g0 = gate_col0 // tn
    nb = d // tn
    return pl.pallas_call(
        _merge_kernel,
        grid=(b, l // tm, nb),
        in_specs=[pl.BlockSpec((1, tm, di), lambda bb, i, j: (bb, i, 0)),
                  pl.BlockSpec((1, tm, d), lambda bb, i, j: (bb, i, 0)),
                  pl.BlockSpec((1, tm, tn), lambda bb, i, j: (bb, i, g0 + j)),
                  pl.BlockSpec((1, tm, tn), lambda bb, i, j: (bb, i, g0 + nb + j)),
                  pl.BlockSpec((di, tn), lambda bb, i, j: (0, j)),
                  pl.BlockSpec((d, tn), lambda bb, i, j: (0, j))],
        out_specs=pl.BlockSpec((1, tm, tn), lambda bb, i, j: (bb, i, j)),
        out_shape=jax.ShapeDtypeStruct((b, l, d), BF16),
        compiler_params=_cparams("parallel", "parallel", "arbitrary"),
        name="merge",
    )(ym, yh, p, p, wm, wh)


def _resid_mm_kernel(a_ref, w_ref, x_ref, g_ref, o_ref):
    o_ref[0] = x_ref[0] + g_ref[0] * jnp.dot(a_ref[0], w_ref[...], preferred_element_type=F32)


def _resid_mm(a, w, x, gate):
    b, l, k = a.shape
    d = w.shape[1]
    tm = min(l, 1024)
    tn = min(d, COL_TILE)
    return pl.pallas_call(
        _resid_mm_kernel,
        grid=(b, l // tm, d // tn),
        in_specs=[pl.BlockSpec((1, tm, k), lambda bb, i, j: (bb, i, 0)),
                  pl.BlockSpec((k, tn), lambda bb, i, j: (0, j)),
                  pl.BlockSpec((1, tm, tn), lambda bb, i, j: (bb, i, j)),
                  pl.BlockSpec((1, 1, tn), lambda bb, i, j: (bb, 0, j))],
        out_specs=pl.BlockSpec((1, tm, tn), lambda bb, i, j: (bb, i, j)),
        out_shape=jax.ShapeDtypeStruct((b, l, d), F32),
        compiler_params=_cparams("parallel", "parallel", "arbitrary"),
        name="resid_mm",
    )(a, w, x, gate)


def _ffn_kernel(x_ref, g_ref, sh_ref, sc_ref, gt_ref, wg_ref, wu_ref, wd_ref, fg_ref, o_ref, h_scr, *, final_norm):
    j = pl.program_id(2)

    @pl.when(j == 0)
    def _():
        h_scr[...] = _modnorm(x_ref[0], g_ref[...], sh_ref[0], sc_ref[0]).astype(BF16)
        o_ref[...] = jnp.zeros_like(o_ref)

    tm = h_scr.shape[0]
    rows_per_pass = min(tm, 512)
    for r0 in range(0, tm, rows_per_pass):
        h = h_scr[r0:r0 + rows_per_pass, :]
        gg = jnp.dot(h, wg_ref[...], preferred_element_type=F32)
        uu = jnp.dot(h, wu_ref[...], preferred_element_type=F32)
        o_ref[0, r0:r0 + rows_per_pass, :] += jnp.dot((_silu(gg) * uu).astype(BF16), wd_ref[...],
                                                      preferred_element_type=F32)

    @pl.when(j == pl.num_programs(2) - 1)
    def _():
        y = x_ref[0] + gt_ref[0] * o_ref[0]
        if final_norm:
            ms = jnp.mean(y * y, axis=-1, keepdims=True)
            y = y * lax.rsqrt(ms + RMS_EPS) * fg_ref[...]
        o_ref[0] = y


def _ffn(x, g, shift, scale, gate, w_gu, w_down, final_g=None):
    b, l, d = x.shape
    f = w_down.shape[0]
    tm = min(l, 1024)
    fg = jnp.ones((1, d), F32) if final_g is None else final_g.reshape(1, d)
    tf = 256
    nf = f // tf
    vec = pl.BlockSpec((1, 1, d), lambda bb, i, j: (bb, 0, 0))
    return pl.pallas_call(
        functools.partial(_ffn_kernel, final_norm=final_g is not None),
        grid=(b, l // tm, nf),
        in_specs=[pl.BlockSpec((1, tm, d), lambda bb, i, j: (bb, i, 0)),
                  pl.BlockSpec((1, d), lambda bb, i, j: (0, 0)),
                  vec, vec, vec,
                  pl.BlockSpec((d, tf), lambda bb, i, j: (0, j)),
                  pl.BlockSpec((d, tf), lambda bb, i, j: (0, j + nf)),
                  pl.BlockSpec((tf, d), lambda bb, i, j: (j, 0)),
                  pl.BlockSpec((1, d), lambda bb, i, j: (0, 0))],
        out_specs=pl.BlockSpec((1, tm, d), lambda bb, i, j: (bb, i, 0)),
        out_shape=jax.ShapeDtypeStruct((b, l, d), F32),
        scratch_shapes=[pltpu.VMEM((tm, d), BF16)],
        compiler_params=_cparams("parallel", "parallel", "arbitrary"),
        name="ffn",
    )(x, g.reshape(1, d), shift, scale, gate, w_gu, w_gu, w_down, fg)


def _grid_pos_embed(rows, d):
    r, col = jnp.meshgrid(jnp.arange(rows), jnp.arange(GRID_W), indexing='ij')
    quarter = d // 4
    omega = 1.0 / (POS_BASE ** (jnp.arange(quarter, dtype=F32) / quarter))

    def axis_embed(pos):
        ang = pos.reshape(-1)[:, None].astype(F32) * omega[None, :]
        return jnp.concatenate([jnp.sin(ang), jnp.cos(ang)], axis=-1)

    return jnp.concatenate([axis_embed(r), axis_embed(col)], axis=-1).astype(F32)


class _Cols:
    def __init__(self, d):
        self.d = d
        self.di = 2 * d
        self.heads = self.di // M_HEADDIM
        self.xbc = self.di + 2 * M_GROUPS * M_STATE
        self.z0 = 0
        self.gate0 = self.z0 + self.di
        self.hy0 = self.gate0 + 2 * d
        self.xbc0 = self.hy0 + 3 * d
        self.dt0 = self.xbc0 + self.xbc
        self.total = self.dt0 + COL_TILE
        assert 2 * self.heads <= LANES and self.heads % (2 * M_GROUPS) == 0
        for off in (self.gate0, self.hy0, self.xbc0, self.dt0):
            assert off % COL_TILE == 0


def _pack_w_in(w_in, cols):
    o_dt = cols.xbc
    o_z = o_dt + 2 * cols.heads
    o_hy = o_z + cols.di
    o_gate = o_hy + 3 * cols.d
    parts = [w_in[:, o_z:o_hy], w_in[:, o_gate:], w_in[:, o_hy:o_gate], w_in[:, :o_dt], w_in[:, o_dt:o_z]]
    w = jnp.concatenate(parts, axis=1)
    return jnp.pad(w, ((0, 0), (0, cols.total - w.shape[1]))).astype(BF16)


def _pad_lanes(v):
    v = v.reshape(1, -1)
    return jnp.pad(v, ((0, 0), (0, LANES - v.shape[1])))


def _mix_sublayer(x, mods, lp, cols, hf, consts, init_f, init_b):
    h = _prenorm(x, lp['norm1_g'], mods[0], mods[1])
    p, xbc, dtp = _inproj(h, lp['w_in'], 0, cols.xbc0 // COL_TILE, lp['m_conv_w'], lp['m_conv_b'])
    y_f, fin_f = _ssd(xbc, dtp, lp['dtb'], lp['alog'], init_f, cols.heads, False, True)
    ym, fin_b = _ssd(xbc, dtp, lp['dtb'], lp['alog'], init_b, cols.heads, True, True,
                     extra=(y_f, p, cols.z0, lp['dskip'], lp['m_norm_g']))
    x0, u = _hconv(p, cols.hy0, lp['h_conv_w'], lp['h_conv_b'], cols.d)
    yh = _long_conv(u, x0, hf, consts, lp['h_bias'])
    merged = _merge(ym, yh, p, cols.gate0, lp['m_w_out'], lp['h_w_out'])
    return _resid_mm(merged, lp['w_merge_out'], x, mods[2]), fin_f, fin_b


def _ctx_final_states(x, mods, lp, cols, init_f, init_b):
    h = _prenorm(x, lp['norm1_g'], mods[0], mods[1])
    _, xbc, dtp = _inproj(h, lp['w_in'], cols.xbc0, 0, lp['m_conv_w'], lp['m_conv_b'])
    _, fin_f = _ssd(xbc, dtp, lp['dtb'], lp['alog'], init_f, cols.heads, False, False)
    _, fin_b = _ssd(xbc, dtp, lp['dtb'], lp['alog'], init_b, cols.heads, True, False)
    return fin_f, fin_b


def kernel(x, c, ctx, c_ctx, ada_w, ada_b, norm1_g, w_in, m_conv_w, m_conv_b, m_dt_bias, m_a_log, m_d, m_norm_g,
           m_w_out, h_conv_w, h_conv_b, hf_w1, hf_b1, hf_w2, hf_b2, hf_freq, hf_w3, h_bias, h_w_out, w_merge_out,
           norm2_g, ffn_w_gu, ffn_w_down, final_g):
    b, l, d = x.shape
    lc = ctx.shape[1]
    depth = ada_w.shape[0]
    cols = _Cols(d)
    assert b % 2 == 0 and b + 1 <= SUBLANES and l % M_CHUNK == 0 and lc % M_CHUNK == 0

    cvecs = jnp.concatenate([c, c_ctx[None], jnp.zeros((SUBLANES - b - 1, d), F32)], axis=0)
    mods_all = _mods(cvecs, ada_w, ada_b)
    consts_l = _dft_consts(*_fft_split(2 * l))
    consts_c = _dft_consts(*_fft_split(2 * lc))

    x_l = _add_pos(x, _grid_pos_embed(l // GRID_W, d))
    x_c = ctx
    zero_state = jnp.zeros((b, M_GROUPS, M_STATE, cols.di // M_GROUPS), F32)
    for i in range(depth):
        lp = dict(norm1_g=norm1_g[i], w_in=_pack_w_in(w_in[i], cols), m_conv_w=m_conv_w[i], m_conv_b=m_conv_b[i],
                  dtb=_pad_lanes(m_dt_bias[i]), alog=_pad_lanes(m_a_log[i]),
                  dskip=jnp.repeat(m_d[i], M_HEADDIM).reshape(1, cols.di), m_norm_g=m_norm_g[i].reshape(1, cols.di),
                  m_w_out=m_w_out[i].astype(BF16), h_conv_w=h_conv_w[i], h_conv_b=h_conv_b[i],
                  hf_w1=hf_w1[i], hf_b1=hf_b1[i], hf_w2=hf_w2[i], hf_b2=hf_b2[i], hf_freq=hf_freq[i], hf_w3=hf_w3[i],
                  h_bias=h_bias[i], h_w_out=h_w_out[i].astype(BF16), w_merge_out=w_merge_out[i].astype(BF16),
                  norm2_g=norm2_g[i], ffn_w_gu=ffn_w_gu[i].astype(BF16), ffn_w_down=ffn_w_down[i].astype(BF16))
        m = mods_all[i].reshape(SUBLANES, ADA_CHUNKS, d)
        mods_l = [m[:b, k][:, None, :] for k in range(ADA_CHUNKS)]
        mods_c = [jnp.broadcast_to(m[b, k][None, None, :], (b, 1, d)) for k in range(ADA_CHUNKS)]
        if i < depth - 1:
            hf_c = _filter_spectrum(lc, consts_c, lp)
            x_c_mixed, fin_f, fin_b = _mix_sublayer(x_c, mods_c, lp, cols, hf_c, consts_c, zero_state, zero_state)
        else:
            fin_f, fin_b = _ctx_final_states(x_c, mods_c, lp, cols, zero_state, zero_state)
        hf_l = _filter_spectrum(l, consts_l, lp)
        x_l, _, _ = _mix_sublayer(x_l, mods_l, lp, cols, hf_l, consts_l, fin_f, fin_b)
        x_l = _ffn(x_l, lp['norm2_g'], mods_l[3], mods_l[4], mods_l[5], lp['ffn_w_gu'], lp['ffn_w_down'],
                   final_g=final_g if i == depth - 1 else None)
        if i < depth - 1:
            x_c = _ffn(x_c_mixed, lp['norm2_g'], mods_c[3], mods_c[4], mods_c[5], lp['ffn_w_gu'], lp['ffn_w_down'])
    return x_l
```

```python
import functools
import math

import jax
import jax.numpy as jnp
from jax import lax
from jax.experimental import pallas as pl
from jax.experimental.pallas import tpu as pltpu

F32 = jnp.float32
BF16 = jnp.bfloat16
HIGHEST = lax.Precision.HIGHEST

RMS_EPS = 1e-6
GRID_W = 64
POS_BASE = 10000.0
ADA_CHUNKS = 6
M_HEADDIM = 64
M_GROUPS = 8
M_STATE = 128
M_CONV = 5
M_CHUNK = 128
H_SHORT = 3
H_EMB = 33
H_DECAY_TARGET = 1e-2
H_FAST_PCT = 0.3
H_SLOW_PCT = 1.5

LANES = 128
SUBLANES = 8
HALO = 16
VMEM_LIMIT_BYTES = 56 * 1024 * 1024

COL_TILE = 512
FEAT_COLS = 40
FFT_K1_BLOCK = 4
SSD_BATCH_ROWS = 2


def _cparams(*sem):
    return pltpu.CompilerParams(dimension_semantics=sem, vmem_limit_bytes=VMEM_LIMIT_BYTES)


def _silu(v):
    h = 0.5 * v
    return h + h * jnp.tanh(h)


def _round_up(a, m):
    return -(-a // m) * m


def _mods_kernel(c_ref, w_ref, b_ref, o_ref):
    s = _silu(c_ref[...])
    o_ref[0] = jnp.dot(s, w_ref[0], preferred_element_type=F32, precision=HIGHEST) + b_ref[0]


def _mods(cvecs, ada_w, ada_b):
    depth, d, n = ada_w.shape
    tn = min(n, 1024)
    return pl.pallas_call(
        _mods_kernel,
        grid=(depth, n // tn),
        in_specs=[pl.BlockSpec((SUBLANES, d), lambda i, j: (0, 0)),
                  pl.BlockSpec((1, d, tn), lambda i, j: (i, 0, j)),
                  pl.BlockSpec((1, 1, tn), lambda i, j: (i, 0, j))],
        out_specs=pl.BlockSpec((1, SUBLANES, tn), lambda i, j: (i, 0, j)),
        out_shape=jax.ShapeDtypeStruct((depth, SUBLANES, n), F32),
        compiler_params=_cparams("parallel", "parallel"),
        name="mods",
    )(cvecs, ada_w, ada_b.reshape(depth, 1, n))


def _add_kernel(x_ref, p_ref, o_ref):
    o_ref[0] = x_ref[0] + p_ref[...]


def _add_pos(x, pos):
    b, l, d = x.shape
    tl = min(l, 1024)
    return pl.pallas_call(
        _add_kernel,
        grid=(l // tl, b),
        in_specs=[pl.BlockSpec((1, tl, d), lambda i, bb: (bb, i, 0)),
                  pl.BlockSpec((tl, d), lambda i, bb: (i, 0))],
        out_specs=pl.BlockSpec((1, tl, d), lambda i, bb: (bb, i, 0)),
        out_shape=jax.ShapeDtypeStruct(x.shape, F32),
        compiler_params=_cparams("parallel", "parallel"),
        name="add_pos",
    )(x, pos)


def _modnorm(x, g, shift, scale):
    ms = jnp.mean(x * x, axis=-1, keepdims=True)
    y = x * lax.rsqrt(ms + RMS_EPS) * g
    return y * (1.0 + scale) + shift


INPROJ_ROWS = 2048
CONV_ROWS = 256


def _prenorm_kernel(prev_ref, x_ref, next_ref, g_ref, sh_ref, sc_ref, o_ref):
    k = pl.program_id(2)
    rows = x_ref.shape[1]
    tl = o_ref.shape[2] - 2 * HALO
    norm = lambda v: _modnorm(v, g_ref[...], sh_ref[0], sc_ref[0]).astype(o_ref.dtype)
    o_ref[0, 0, pl.ds(pl.multiple_of(HALO + k * rows, HALO), rows), :] = norm(x_ref[0])

    @pl.when(k == 0)
    def _():
        o_ref[0, 0, 0:HALO, :] = norm(prev_ref[0])

    @pl.when(k == pl.num_programs(2) - 1)
    def _():
        o_ref[0, 0, HALO + tl:, :] = norm(next_ref[0])


def _prenorm(x, g, shift, scale):
    b, l, d = x.shape
    tl = min(l, INPROJ_ROWS)
    rows = min(tl, 512)
    nk = tl // rows
    rpb = tl // HALO
    last = l // HALO - 1
    vec = pl.BlockSpec((1, 1, d), lambda bb, i, k: (bb, 0, 0))
    return pl.pallas_call(
        _prenorm_kernel,
        grid=(b, l // tl, nk),
        in_specs=[pl.BlockSpec((1, HALO, d), lambda bb, i, k: (bb, jnp.maximum(i * rpb - 1, 0), 0)),
                  pl.BlockSpec((1, rows, d), lambda bb, i, k: (bb, i * nk + k, 0)),
                  pl.BlockSpec((1, HALO, d), lambda bb, i, k: (bb, jnp.minimum((i + 1) * rpb, last), 0)),
                  pl.BlockSpec((1, d), lambda bb, i, k: (0, 0)), vec, vec],
        out_specs=pl.BlockSpec((1, 1, tl + 2 * HALO, d), lambda bb, i, k: (bb, i, 0, 0)),
        out_shape=jax.ShapeDtypeStruct((b, l // tl, tl + 2 * HALO, d), BF16),
        compiler_params=_cparams("parallel", "parallel", "arbitrary"),
        name="prenorm",
    )(x, x, x, g.reshape(1, d), shift, scale)


def _inproj_kernel(*refs, n_plain, n_conv):
    if n_plain:
        h_ref, w_ref, cw_ref, cb_ref, o_ref, xbc_ref, dt_ref, acc_scr = refs
    else:
        h_ref, w_ref, cw_ref, cb_ref, xbc_ref, dt_ref, acc_scr = refs
    i = pl.program_id(1)
    k = pl.program_id(2) - n_plain
    tm = dt_ref.shape[1]
    project = lambda: jnp.dot(h_ref[0, 0], w_ref[...], preferred_element_type=F32)

    def project_into(dst):
        dst[...] = project()
        dst[0:HALO, :] = jnp.where(i > 0, dst[0:HALO, :], 0.0)
        dst[HALO + tm:, :] = jnp.where(i < pl.num_programs(1) - 1, dst[HALO + tm:, :], 0.0)

    def conv_from(src):
        pad = M_CONV // 2
        rows = min(tm, CONV_ROWS)
        for r0 in range(0, tm, rows):
            ext = src[r0:r0 + rows + 2 * HALO, :]
            acc = None
            for tap in range(M_CONV):
                shifted = ext if tap == pad else pltpu.roll(ext, (pad - tap) % (rows + 2 * HALO), axis=0)
                term = shifted[HALO:HALO + rows] * cw_ref[tap:tap + 1, :]
                acc = term if acc is None else acc + term
            xbc_ref[0, r0:r0 + rows, :] = _silu(acc + cb_ref[...]).astype(xbc_ref.dtype)

    if n_plain:
        @pl.when(k < 0)
        def _():
            o_ref[0] = project()[HALO:HALO + tm].astype(o_ref.dtype)

    @pl.when(jnp.logical_and(k >= 0, k < n_conv))
    def _():
        project_into(acc_scr)
        conv_from(acc_scr)

    @pl.when(k == n_conv)
    def _():
        dt_ref[0] = project()[HALO:HALO + tm, 0:LANES]


def _inproj(h, w, col0, n_plain, conv_w, conv_b):
    b, nt, tmh, d = h.shape
    tm = tmh - 2 * HALO
    l = nt * tm
    tn = COL_TILE
    j0 = col0 // tn
    n_conv = conv_w.shape[1] // tn
    nj = n_plain + n_conv + 1
    conv_idx = lambda j: jnp.clip(j - n_plain, 0, n_conv - 1)
    out_specs = [pl.BlockSpec((1, tm, tn), lambda bb, i, j: (bb, i, conv_idx(j))),
                 pl.BlockSpec((1, tm, LANES), lambda bb, i, j: (bb, i, 0))]
    out_shape = [jax.ShapeDtypeStruct((b, l, n_conv * tn), BF16), jax.ShapeDtypeStruct((b, l, LANES), F32)]
    if n_plain:
        out_specs.insert(0, pl.BlockSpec((1, tm, tn), lambda bb, i, j: (bb, i, jnp.minimum(j, n_plain - 1))))
        out_shape.insert(0, jax.ShapeDtypeStruct((b, l, n_plain * tn), BF16))
    outs = pl.pallas_call(
        functools.partial(_inproj_kernel, n_plain=n_plain, n_conv=n_conv),
        grid=(b, nt, nj),
        in_specs=[pl.BlockSpec((1, 1, tmh, d), lambda bb, i, j: (bb, i, 0, 0)),
                  pl.BlockSpec((d, tn), lambda bb, i, j: (0, j + j0)),
                  pl.BlockSpec((M_CONV, tn), lambda bb, i, j: (0, conv_idx(j))),
                  pl.BlockSpec((1, tn), lambda bb, i, j: (0, conv_idx(j)))],
        out_specs=out_specs,
        out_shape=out_shape,
        scratch_shapes=[pltpu.VMEM((tmh, tn), F32)],
        compiler_params=_cparams("parallel", "parallel", "arbitrary"),
        name="inproj",
    )(h, w, conv_w, conv_b.reshape(1, -1))
    return outs if n_plain else [None] + list(outs)


def _conv_taps(prev_ref, main_ref, next_ref, w_ref, b_ref, ktaps):
    i = pl.program_id(2)
    nblk = pl.num_programs(2)
    tl = main_ref.shape[1]
    pad = ktaps // 2
    prev = jnp.where(i > 0, prev_ref[0].astype(F32), 0.0)
    nxt = jnp.where(i < nblk - 1, next_ref[0].astype(F32), 0.0)
    ext = jnp.concatenate([prev, main_ref[0].astype(F32), nxt], axis=0)
    rows = tl + 2 * HALO
    acc = None
    for j in range(ktaps):
        shifted = ext if j == pad else pltpu.roll(ext, (pad - j) % rows, axis=0)
        term = shifted[HALO:HALO + tl] * w_ref[j:j + 1, :]
        acc = term if acc is None else acc + term
    return acc + b_ref[...]


def _conv_specs(tl, cw, l, col_block0):
    rpb = tl // HALO
    last = l // HALO - 1
    return [
        pl.BlockSpec((1, HALO, cw), lambda bb, c, i: (bb, jnp.maximum(i * rpb - 1, 0), c + col_block0)),
        pl.BlockSpec((1, tl, cw), lambda bb, c, i: (bb, i, c + col_block0)),
        pl.BlockSpec((1, HALO, cw), lambda bb, c, i: (bb, jnp.minimum((i + 1) * rpb, last), c + col_block0)),
    ]


def _hconv_kernel(p0, m0, n0, p1, m1, n1, p2, m2, n2, w0, b0, w1, b1, w2, b2, x0_ref, u_ref):
    x0_ref[0] = _conv_taps(p0, m0, n0, w0, b0, H_SHORT).astype(x0_ref.dtype)
    x1 = _conv_taps(p1, m1, n1, w1, b1, H_SHORT)
    v = _conv_taps(p2, m2, n2, w2, b2, H_SHORT)
    u_ref[0] = (x1 * v).astype(u_ref.dtype)


def _hconv(p, col0, w, bias, d):
    b, l, _ = p.shape
    tl = min(l, 512)
    cw = min(d, COL_TILE)
    nb = d // cw
    specs, wspecs = [], []
    for s in range(3):
        specs += _conv_specs(tl, cw, l, col0 // cw + s * nb)
        wspecs += [pl.BlockSpec((H_SHORT, cw), functools.partial(lambda bb, c, i, s: (0, c + s * nb), s=s)),
                   pl.BlockSpec((1, cw), functools.partial(lambda bb, c, i, s: (0, c + s * nb), s=s))]
    bias2 = bias.reshape(1, 3 * d)
    out_spec = pl.BlockSpec((1, tl, cw), lambda bb, c, i: (bb, i, c))
    return pl.pallas_call(
        _hconv_kernel,
        grid=(b, nb, l // tl),
        in_specs=specs + wspecs,
        out_specs=[out_spec, out_spec],
        out_shape=[jax.ShapeDtypeStruct((b, l, d), BF16)] * 2,
        compiler_params=_cparams("parallel", "parallel", "arbitrary"),
        name="hconv",
    )(p, p, p, p, p, p, p, p, p, w, bias2, w, bias2, w, bias2)


def _ssd_kernel(*refs, reverse, with_output, epilogue, heads):
    if epilogue:
        (xbc_ref, dt_ref, dtb_ref, alog_ref, init_ref, sel_ref, yf_ref, z_ref, dsk_ref, ng_ref,
         y_ref, fin_ref, s_scr) = refs
    elif with_output:
        xbc_ref, dt_ref, dtb_ref, alog_ref, init_ref, sel_ref, y_ref, fin_ref, s_scr = refs
    else:
        xbc_ref, dt_ref, dtb_ref, alog_ref, init_ref, fin_ref, s_scr = refs
    t = M_CHUNK
    n = M_STATE
    assert n == t
    di = heads * M_HEADDIM
    gw = di // M_GROUPS
    rpg = heads // M_GROUPS
    lane0 = heads if reverse else 0
    edge = 0 if reverse else t - 1
    c = pl.program_id(1)

    nb = xbc_ref.shape[0]

    @pl.when(c == 0)
    def _():
        s_scr[...] = init_ref[...]

    row = lax.broadcasted_iota(jnp.int32, (t, t), 0)
    col = lax.broadcasted_iota(jnp.int32, (t, t), 1)
    mask = (row <= col) if reverse else (row >= col)
    lane = lax.broadcasted_iota(jnp.int32, (1, LANES), 1)
    lo = lane < M_HEADDIM
    mlo = jnp.where(lo, 1.0, 0.0).astype(BF16)
    mhi = jnp.where(lo, 0.0, 1.0).astype(BF16)

    def chunk_decays(bi):
        x = dt_ref[bi] + dtb_ref[...]
        dt = jnp.maximum(x, 0.0) + jnp.log1p(jnp.exp(-jnp.abs(x)))
        dta = dt * (-jnp.exp(alog_ref[...]))
        acs = jnp.dot(mask.astype(F32), dta, preferred_element_type=F32, precision=HIGHEST)
        acs_t = acs.T
        dt_t = dt.T
        arow_dt = acs_t - jnp.log(dt_t)
        w_t = jnp.exp(acs_t[:, edge:edge + 1] - acs_t) * dt_t
        dec = jnp.exp(acs[edge:edge + 1, :])
        eacs = jnp.exp(acs).astype(BF16) if with_output else None
        return acs, arow_dt, w_t, dec, eacs

    def group(bi, g, acs, arow_dt, w_t, dec, eacs):
        bt = xbc_ref[bi, :, di + g * n:di + (g + 1) * n].astype(F32).T
        if with_output:
            cm = xbc_ref[bi, :, di + M_GROUPS * n + g * n:di + M_GROUPS * n + (g + 1) * n]
            cb = jnp.dot(cm, bt.astype(BF16), preferred_element_type=F32)
            y_off = jnp.dot(cm, s_scr[bi, g].astype(BF16), preferred_element_type=F32)
        y_parts = []
        for q in range(rpg // 2):
            c0 = g * gw + q * LANES
            xp = xbc_ref[bi, :, c0:c0 + LANES]
            sp = s_scr[bi, g, :, q * LANES:(q + 1) * LANES]
            x2 = jnp.concatenate([xp * mlo, xp * mhi], axis=0)
            hl0 = lane0 + g * rpg + 2 * q
            lhs, btw, decs = [], [], []
            for r in range(2):
                hl = hl0 + r
                if with_output:
                    acol = jnp.broadcast_to(acs[:, hl:hl + 1], (t, t))
                    dk = jnp.exp(jnp.where(mask, acol - arow_dt[hl:hl + 1, :], -jnp.inf))
                    lhs.append((cb * dk).astype(BF16))
                btw.append((bt * w_t[hl:hl + 1, :]).astype(BF16))
                decs.append(jnp.broadcast_to(dec[:, hl:hl + 1], (1, LANES)))
            if with_output:
                e_pair = jnp.dot(eacs, sel_ref[hl0 // 2], preferred_element_type=F32)
                y_diag = jnp.dot(jnp.concatenate(lhs, axis=1), x2, preferred_element_type=F32)
                y_parts.append(y_diag + e_pair * y_off[:, q * LANES:(q + 1) * LANES])
            ds = jnp.dot(jnp.concatenate(btw, axis=1), x2, preferred_element_type=F32)
            s_scr[bi, g, :, q * LANES:(q + 1) * LANES] = jnp.where(lo, decs[0], decs[1]) * sp + ds
        if with_output:
            yg = jnp.concatenate(y_parts, axis=1) if len(y_parts) > 1 else y_parts[0]
            sl = slice(g * gw, (g + 1) * gw)
            if epilogue:
                yg = yg + yf_ref[bi, :, sl].astype(F32)
                v = (yg + xbc_ref[bi, :, sl].astype(F32) * dsk_ref[:, sl]) * _silu(z_ref[bi, :, sl].astype(F32))
                ms = jnp.mean(v * v, axis=-1, keepdims=True)
                y_ref[bi, :, sl] = (v * lax.rsqrt(ms + RMS_EPS) * ng_ref[:, sl]).astype(y_ref.dtype)
            else:
                y_ref[bi, :, sl] = yg.astype(y_ref.dtype)

    decays = [chunk_decays(bi) for bi in range(nb)]
    for g in range(M_GROUPS):
        for bi in range(nb):
            group(bi, g, *decays[bi])

    @pl.when(c == pl.num_programs(1) - 1)
    def _():
        fin_ref[...] = s_scr[...]


def _pair_selectors():
    p = jnp.arange(LANES // 2)[:, None, None]
    h = jnp.arange(LANES)[None, :, None]
    lane = jnp.arange(LANES)[None, None, :]
    return (h == 2 * p + (lane >= M_HEADDIM)).astype(BF16)


def _ssd(xbc, dtp, dtb, alog, init, heads, reverse, with_output, extra=None):
    b, l, xw = xbc.shape
    t = M_CHUNK
    nc = l // t
    di = heads * M_HEADDIM
    gw = di // M_GROUPS
    epilogue = extra is not None
    cidx = (lambda c: nc - 1 - c) if reverse else (lambda c: c)
    nb = SSD_BATCH_ROWS if b % SSD_BATCH_ROWS == 0 else 1
    state_spec = pl.BlockSpec((nb, M_GROUPS, M_STATE, gw), lambda bb, c: (bb, 0, 0, 0))
    in_specs = [pl.BlockSpec((nb, t, xw), lambda bb, c: (bb, cidx(c), 0)),
                pl.BlockSpec((nb, t, LANES), lambda bb, c: (bb, cidx(c), 0)),
                pl.BlockSpec((1, LANES), lambda bb, c: (0, 0)),
                pl.BlockSpec((1, LANES), lambda bb, c: (0, 0)),
                state_spec]
    args = [xbc, dtp, dtb, alog, init]
    out_specs, out_shape = [], []
    if with_output:
        in_specs.append(pl.BlockSpec((LANES // 2, LANES, LANES), lambda bb, c: (0, 0, 0)))
        args.append(_pair_selectors())
    if epilogue:
        yf, p, z_col0, dsk, ng = extra
        in_specs += [pl.BlockSpec((nb, t, di), lambda bb, c: (bb, cidx(c), 0)),
                     pl.BlockSpec((nb, t, di), lambda bb, c: (bb, cidx(c), z_col0 // di)),
                     pl.BlockSpec((1, di), lambda bb, c: (0, 0)),
                     pl.BlockSpec((1, di), lambda bb, c: (0, 0))]
        args += [yf, p, dsk, ng]
    if with_output:
        out_specs.append(pl.BlockSpec((nb, t, di), lambda bb, c: (bb, cidx(c), 0)))
        out_shape.append(jax.ShapeDtypeStruct((b, l, di), BF16))
    out_specs.append(state_spec)
    out_shape.append(jax.ShapeDtypeStruct((b, M_GROUPS, M_STATE, gw), F32))
    outs = pl.pallas_call(
        functools.partial(_ssd_kernel, reverse=reverse, with_output=with_output, epilogue=epilogue, heads=heads),
        grid=(b // nb, nc),
        in_specs=in_specs,
        out_specs=out_specs,
        out_shape=out_shape,
        scratch_shapes=[pltpu.VMEM((nb, M_GROUPS, M_STATE, gw), F32)],
        compiler_params=_cparams("parallel", "arbitrary"),
        name="ssd_bwd" if reverse else "ssd_fwd",
    )(*args)
    return (outs[0], outs[1]) if with_output else (None, outs[0])


def _filt_kernel(f_ref, w1_ref, b1_ref, w2_ref, b2_ref, fr_ref, w3_ref, dl_ref, o_ref, ss_ref):
    f = f_ref[...]
    fr = fr_ref[...]
    h = jnp.sin(fr * (jnp.dot(f, w1_ref[...], preferred_element_type=F32, precision=HIGHEST) + b1_ref[...]))
    h = jnp.sin(fr * (jnp.dot(h, w2_ref[...], preferred_element_type=F32, precision=HIGHEST) + b2_ref[...]))
    filt = jnp.dot(h, w3_ref[...], preferred_element_type=F32, precision=HIGHEST)
    tcol = f[:, H_EMB:H_EMB + 1]
    mcol = f[:, H_EMB + 1:H_EMB + 2]
    out = filt * jnp.exp(-tcol * dl_ref[...]) * mcol
    o_ref[...] = out

    @pl.when(pl.program_id(1) == 0)
    def _():
        ss_ref[...] = jnp.zeros_like(ss_ref)

    ss_ref[...] += jnp.sum(out * out, axis=0, keepdims=True)


def _filter_features(l):
    t = jnp.linspace(0.0, 1.0, l, dtype=F32)[:, None]
    ang = (2.0 * math.pi / l) * jnp.arange(l, dtype=F32)[:, None]
    nb = (H_EMB - 1) // 2
    bands = jnp.linspace(1e-4, nb - 1, nb, dtype=F32)[None, :]
    feats = jnp.concatenate([t, jnp.cos(bands * ang), -jnp.sin(bands * ang), t, jnp.ones_like(t)], axis=-1)
    rev = jnp.concatenate([feats[0:1] * 0.0, jnp.flip(feats[1:], axis=0)], axis=0)
    full = jnp.concatenate([feats, rev], axis=0)
    return jnp.pad(full, ((0, 0), (0, FEAT_COLS - full.shape[1])))


def _hyena_filter(l, w1, b1, w2, b2, freq, w3):
    hid = w2.shape[0]
    d = w3.shape[1] // 2
    feats = _filter_features(l)
    w1p = jnp.pad(w1, ((0, FEAT_COLS - w1.shape[0]), (0, 0)))
    deltas = jnp.abs(jnp.linspace(math.log(H_DECAY_TARGET) / H_SLOW_PCT, math.log(H_DECAY_TARGET) / H_FAST_PCT,
                                  d, dtype=F32))[None, :]
    tl = min(l, 512)
    cw = d
    nb = d // cw
    nrow = l // tl
    small = lambda shape: pl.BlockSpec(shape, lambda c, i: (0, 0))
    return pl.pallas_call(
        _filt_kernel,
        grid=(nb, 2 * nrow),
        in_specs=[pl.BlockSpec((tl, FEAT_COLS), lambda c, i: (i, 0)),
                  small((FEAT_COLS, hid)), small((1, hid)), small((hid, hid)), small((1, hid)), small((1, hid)),
                  pl.BlockSpec((hid, cw), lambda c, i: (0, c + (i // nrow) * nb)),
                  pl.BlockSpec((1, cw), lambda c, i: (0, c))],
        out_specs=[pl.BlockSpec((tl, cw), lambda c, i: (i, c)),
                   pl.BlockSpec((1, cw), lambda c, i: (0, c))],
        out_shape=[jax.ShapeDtypeStruct((2 * l, d), F32), jax.ShapeDtypeStruct((1, d), F32)],
        compiler_params=_cparams("parallel", "arbitrary"),
        name="hyena_filter",
    )(feats, w1p, b1.reshape(1, hid), w2, b2.reshape(1, hid), freq.reshape(1, hid), w3, deltas)


def _fft_split(n):
    n1 = 1 << ((n.bit_length() - 1 + 1) // 2)
    return n1, n // n1


def _dft_consts(n1, n2):
    n = n1 * n2
    h = n1 // 2
    k1 = jnp.arange(n1, dtype=jnp.int32)
    t1 = jnp.arange(n1, dtype=jnp.int32)
    t2 = jnp.arange(n2, dtype=jnp.int32)
    idx = (k1[None, :, None] * (n2 * t1[None, None, :] + t2[:, None, None])) % n
    ang = idx.astype(F32) * (2.0 * math.pi / n)
    cs, sn = jnp.cos(ang), jnp.sin(ang)
    ch, sh = cs[..., :h], sn[..., :h]
    t_data = jnp.concatenate([jnp.concatenate([ch, sh], -1), jnp.concatenate([-sh, ch], -1)], axis=1)
    t_filt = jnp.concatenate([cs, -sn], axis=1)
    ct, st = jnp.swapaxes(ch, 1, 2), jnp.swapaxes(sh, 1, 2)
    t_inv = jnp.concatenate([jnp.concatenate([ct, -st], -1), jnp.concatenate([st, ct], -1)], axis=1) / n
    k2 = jnp.arange(n2, dtype=jnp.int32)
    ang2 = ((k2[:, None] * k2[None, :]) % n2).astype(F32) * (2.0 * math.pi / n2)
    c2, s2 = jnp.cos(ang2), jnp.sin(ang2)
    m_fwd = jnp.concatenate([jnp.concatenate([c2, s2], -1), jnp.concatenate([-s2, c2], -1)], axis=0)
    m_inv = jnp.concatenate([jnp.concatenate([c2, -s2], -1), jnp.concatenate([s2, c2], -1)], axis=0)
    return dict(t_data=t_data.astype(BF16), t_filt=t_filt.astype(BF16), t_inv=t_inv.astype(BF16),
                m_fwd=m_fwd.astype(BF16), m_inv=m_inv.astype(BF16))


def _fft1_kernel(x_ref, t_ref, o_ref):
    a = x_ref.shape[1]
    xs = [pltpu.einshape("rjc->jrc", x_ref[0, s].astype(F32)) for s in range(a)]
    outs = []
    for j in range(SUBLANES):
        x = jnp.concatenate([xs[s][j] for s in range(a)], axis=0).astype(BF16)
        outs.append(jnp.dot(t_ref[j], x, preferred_element_type=F32))
    o = pltpu.einshape("jkc->kjc", jnp.stack(outs, axis=0))
    n1 = o_ref.shape[1]
    o_ref[0] = _pack_complex(o[:n1], o[n1:])


def _pack_complex(re, im):
    r = lax.bitcast_convert_type(re.astype(BF16).astype(F32), jnp.uint32)
    i = lax.bitcast_convert_type(im.astype(BF16).astype(F32), jnp.uint32)
    return (r >> 16) | i


def _unpack_complex(w):
    re = lax.bitcast_convert_type(w << 16, F32)
    im = lax.bitcast_convert_type(w & jnp.uint32(0xFFFF0000), F32)
    return re, im


def _fft_stage1(xv, tmat, n1, n2, d):
    p, a, r, _, _ = xv.shape
    cw = min(d, COL_TILE)
    return pl.pallas_call(
        _fft1_kernel,
        grid=(n2 // SUBLANES, d // cw, p),
        in_specs=[pl.BlockSpec((1, a, r, SUBLANES, cw), lambda j, c, pp: (pp, 0, 0, j, c)),
                  pl.BlockSpec((SUBLANES, 2 * n1, n1), lambda j, c, pp: (j, 0, 0))],
        out_specs=pl.BlockSpec((1, n1, SUBLANES, cw), lambda j, c, pp: (pp, 0, j, c)),
        out_shape=jax.ShapeDtypeStruct((p, n1, n2, d), jnp.uint32),
        compiler_params=_cparams("parallel", "parallel", "parallel"),
        name="fft_stage1",
    )(xv, tmat)


def _fft2_kernel(a_ref, af_ref, ss_ref, mf_ref, mi_ref, o_ref, h_scr):
    n2 = a_ref.shape[2]

    def stage(w):
        re, im = _unpack_complex(w)
        return jnp.dot(mf_ref[...], jnp.concatenate([re, im], axis=0).astype(BF16), preferred_element_type=F32)

    @pl.when(pl.program_id(2) == 0)
    def _():
        for k in range(FFT_K1_BLOCK):
            h_scr[k] = stage(af_ref[0, k]) * lax.rsqrt(ss_ref[...] + RMS_EPS)

    for k in range(FFT_K1_BLOCK):
        xk = stage(a_ref[0, k])
        xr, xi = xk[:n2], xk[n2:]
        hr, hi = h_scr[k, 0:n2, :], h_scr[k, n2:, :]
        z = jnp.concatenate([xr * hr - xi * hi, xr * hi + xi * hr], axis=0).astype(BF16)
        bk = jnp.dot(mi_ref[...], z, preferred_element_type=F32)
        o_ref[0, k] = _pack_complex(bk[:n2], bk[n2:])


def _fft_stage2(av, afv, ss, m_fwd, m_inv, n1, n2, d):
    p = av.shape[0]
    cw = min(d, COL_TILE)
    kb = FFT_K1_BLOCK
    return pl.pallas_call(
        _fft2_kernel,
        grid=(n1 // kb, d // cw, p),
        in_specs=[pl.BlockSpec((1, kb, n2, cw), lambda k, c, pp: (pp, k, 0, c)),
                  pl.BlockSpec((1, kb, n2, cw), lambda k, c, pp: (0, k, 0, c)),
                  pl.BlockSpec((1, cw), lambda k, c, pp: (0, c)),
                  pl.BlockSpec((2 * n2, 2 * n2), lambda k, c, pp: (0, 0)),
                  pl.BlockSpec((2 * n2, 2 * n2), lambda k, c, pp: (0, 0))],
        out_specs=pl.BlockSpec((1, kb, n2, cw), lambda k, c, pp: (pp, k, 0, c)),
        out_shape=jax.ShapeDtypeStruct(av.shape, jnp.uint32),
        scratch_shapes=[pltpu.VMEM((kb, 2 * n2, cw), F32)],
        compiler_params=_cparams("parallel", "parallel", "arbitrary"),
        name="fft_stage2",
    )(av, afv, ss, m_fwd, m_inv)


def _fft3_kernel(b_ref, t_ref, u_ref, x0_ref, bias_ref, o_ref):
    n1 = b_ref.shape[1]
    h = n1 // 2
    bs = _unpack_complex(pltpu.einshape("kjc->jkc", b_ref[0]))
    ys = []
    for j in range(SUBLANES):
        bv = jnp.concatenate([bs[0][j], bs[1][j]], axis=0).astype(BF16)
        ys.append(jnp.dot(t_ref[j], bv, preferred_element_type=F32))
    y = pltpu.einshape("jtc->tjc", jnp.stack(ys, axis=0))
    for s in range(2):
        conv = y[s * h:(s + 1) * h] + u_ref[0, s].astype(F32) * bias_ref[...]
        o_ref[0, s] = (x0_ref[0, s].astype(F32) * conv).astype(o_ref.dtype)


def _fft_stage3(bv, t_inv, uv, x0v, bias, n1, n2, d):
    p = bv.shape[0]
    h = n1 // 2
    cw = min(d, COL_TILE)
    io_spec = pl.BlockSpec((1, 2, h, SUBLANES, cw), lambda j, c, pp: (pp, 0, 0, j, c))
    return pl.pallas_call(
        _fft3_kernel,
        grid=(n2 // SUBLANES, d // cw, p),
        in_specs=[pl.BlockSpec((1, n1, SUBLANES, cw), lambda j, c, pp: (pp, 0, j, c)),
                  pl.BlockSpec((SUBLANES, n1, 2 * n1), lambda j, c, pp: (j, 0, 0)),
                  io_spec, io_spec,
                  pl.BlockSpec((1, cw), lambda j, c, pp: (0, c))],
        out_specs=io_spec,
        out_shape=jax.ShapeDtypeStruct(uv.shape, BF16),
        compiler_params=_cparams("parallel", "parallel", "parallel"),
        name="fft_stage3",
    )(bv, t_inv, uv, x0v, bias)


def _filter_spectrum(l, consts, lp):
    n1, n2 = _fft_split(2 * l)
    d = lp['h_bias'].shape[0]
    full, ss = _hyena_filter(l, lp['hf_w1'], lp['hf_b1'], lp['hf_w2'], lp['hf_b2'], lp['hf_freq'], lp['hf_w3'])
    return _fft_stage1(full.reshape(1, 1, n1, n2, d), consts['t_filt'], n1, n2, d), ss


def _long_conv(u, x0, hf, consts, bias):
    b, l, d = u.shape
    n1, n2 = _fft_split(2 * l)
    p = b // 2
    h = n1 // 2
    uv = u.reshape(p, 2, h, n2, d)
    a = _fft_stage1(uv, consts['t_data'], n1, n2, d)
    bv = _fft_stage2(a, hf[0], hf[1], consts['m_fwd'], consts['m_inv'], n1, n2, d)
    y = _fft_stage3(bv, consts['t_inv'], uv, x0.reshape(p, 2, h, n2, d), bias.reshape(1, d), n1, n2, d)
    return y.reshape(b, l, d)


def _merge_kernel(ym_ref, yh_ref, ga_ref, gb_ref, wm_ref, wh_ref, o_ref):
    a = jnp.dot(ym_ref[0], wm_ref[...], preferred_element_type=F32)
    bb = jnp.dot(yh_ref[0], wh_ref[...], preferred_element_type=F32)
    ga = jax.nn.sigmoid(ga_ref[0].astype(F32))
    gb = jax.nn.sigmoid(gb_ref[0].astype(F32))
    o_ref[0] = (ga * a + gb * bb).astype(o_ref.dtype)


def _merge(ym, yh, p, gate_col0, wm, wh):
    b, l, di = ym.shape
    d = yh.shape[2]
    tm = min(l, 1024)
    tn = min(d, COL_TILE)
    g0 = gate_col0 // tn
    nb = d // tn
    return pl.pallas_call(
        _merge_kernel,
        grid=(b, l // tm, nb),
        in_specs=[pl.BlockSpec((1, tm, di), lambda bb, i, j: (bb, i, 0)),
                  pl.BlockSpec((1, tm, d), lambda bb, i, j: (bb, i, 0)),
                  pl.BlockSpec((1, tm, tn), lambda bb, i, j: (bb, i, g0 + j)),
                  pl.BlockSpec((1, tm, tn), lambda bb, i, j: (bb, i, g0 + nb + j)),
                  pl.BlockSpec((di, tn), lambda bb, i, j: (0, j)),
                  pl.BlockSpec((d, tn), lambda bb, i, j: (0, j))],
        out_specs=pl.BlockSpec((1, tm, tn), lambda bb, i, j: (bb, i, j)),
        out_shape=jax.ShapeDtypeStruct((b, l, d), BF16),
        compiler_params=_cparams("parallel", "parallel", "arbitrary"),
        name="merge",
    )(ym, yh, p, p, wm, wh)


def _resid_mm_kernel(a_ref, w_ref, x_ref, g_ref, o_ref):
    o_ref[0] = x_ref[0] + g_ref[0] * jnp.dot(a_ref[0], w_ref[...], preferred_element_type=F32)


def _resid_mm(a, w, x, gate):
    b, l, k = a.shape
    d = w.shape[1]
    tm = min(l, 1024)
    tn = min(d, COL_TILE)
    return pl.pallas_call(
        _resid_mm_kernel,
        grid=(b, l // tm, d // tn),
        in_specs=[pl.BlockSpec((1, tm, k), lambda bb, i, j: (bb, i, 0)),
                  pl.BlockSpec((k, tn), lambda bb, i, j: (0, j)),
                  pl.BlockSpec((1, tm, tn), lambda bb, i, j: (bb, i, j)),
                  pl.BlockSpec((1, 1, tn), lambda bb, i, j: (bb, 0, j))],
        out_specs=pl.BlockSpec((1, tm, tn), lambda bb, i, j: (bb, i, j)),
        out_shape=jax.ShapeDtypeStruct((b, l, d), F32),
        compiler_params=_cparams("parallel", "parallel", "arbitrary"),
        name="resid_mm",
    )(a, w, x, gate)


def _ffn_kernel(x_ref, g_ref, sh_ref, sc_ref, gt_ref, wg_ref, wu_ref, wd_ref, fg_ref, o_ref, h_scr, *, final_norm):
    j = pl.program_id(2)

    @pl.when(j == 0)
    def _():
        h_scr[...] = _modnorm(x_ref[0], g_ref[...], sh_ref[0], sc_ref[0]).astype(BF16)
        o_ref[...] = jnp.zeros_like(o_ref)

    tm = h_scr.shape[0]
    rows_per_pass = min(tm, 512)
    for r0 in range(0, tm, rows_per_pass):
        h = h_scr[r0:r0 + rows_per_pass, :]
        gg = jnp.dot(h, wg_ref[...], preferred_element_type=F32)
        uu = jnp.dot(h, wu_ref[...], preferred_element_type=F32)
        o_ref[0, r0:r0 + rows_per_pass, :] += jnp.dot((_silu(gg) * uu).astype(BF16), wd_ref[...],
                                                      preferred_element_type=F32)

    @pl.when(j == pl.num_programs(2) - 1)
    def _():
        y = x_ref[0] + gt_ref[0] * o_ref[0]
        if final_norm:
            ms = jnp.mean(y * y, axis=-1, keepdims=True)
            y = y * lax.rsqrt(ms + RMS_EPS) * fg_ref[...]
        o_ref[0] = y


def _ffn(x, g, shift, scale, gate, w_gu, w_down, final_g=None):
    b, l, d = x.shape
    f = w_down.shape[0]
    tm = min(l, 1024)
    fg = jnp.ones((1, d), F32) if final_g is None else final_g.reshape(1, d)
    tf = 256
    nf = f // tf
    vec = pl.BlockSpec((1, 1, d), lambda bb, i, j: (bb, 0, 0))
    return pl.pallas_call(
        functools.partial(_ffn_kernel, final_norm=final_g is not None),
        grid=(b, l // tm, nf),
        in_specs=[pl.BlockSpec((1, tm, d), lambda bb, i, j: (bb, i, 0)),
                  pl.BlockSpec((1, d), lambda bb, i, j: (0, 0)),
                  vec, vec, vec,
                  pl.BlockSpec((d, tf), lambda bb, i, j: (0, j)),
                  pl.BlockSpec((d, tf), lambda bb, i, j: (0, j + nf)),
                  pl.BlockSpec((tf, d), lambda bb, i, j: (j, 0)),
                  pl.BlockSpec((1, d), lambda bb, i, j: (0, 0))],
        out_specs=pl.BlockSpec((1, tm, d), lambda bb, i, j: (bb, i, 0)),
        out_shape=jax.ShapeDtypeStruct((b, l, d), F32),
        scratch_shapes=[pltpu.VMEM((tm, d), BF16)],
        compiler_params=_cparams("parallel", "parallel", "arbitrary"),
        name="ffn",
    )(x, g.reshape(1, d), shift, scale, gate, w_gu, w_gu, w_down, fg)


def _grid_pos_embed(rows, d):
    r, col = jnp.meshgrid(jnp.arange(rows), jnp.arange(GRID_W), indexing='ij')
    quarter = d // 4
    omega = 1.0 / (POS_BASE ** (jnp.arange(quarter, dtype=F32) / quarter))

    def axis_embed(pos):
        ang = pos.reshape(-1)[:, None].astype(F32) * omega[None, :]
        return jnp.concatenate([jnp.sin(ang), jnp.cos(ang)], axis=-1)

    return jnp.concatenate([axis_embed(r), axis_embed(col)], axis=-1).astype(F32)


class _Cols:
    def __init__(self, d):
        self.d = d
        self.di = 2 * d
        self.heads = self.di // M_HEADDIM
        self.xbc = self.di + 2 * M_GROUPS * M_STATE
        self.z0 = 0
        self.gate0 = self.z0 + self.di
        self.hy0 = self.gate0 + 2 * d
        self.xbc0 = self.hy0 + 3 * d
        self.dt0 = self.xbc0 + self.xbc
        self.total = self.dt0 + COL_TILE
        assert 2 * self.heads <= LANES and self.heads % (2 * M_GROUPS) == 0
        for off in (self.gate0, self.hy0, self.xbc0, self.dt0):
            assert off % COL_TILE == 0


def _pack_w_in(w_in, cols):
    o_dt = cols.xbc
    o_z = o_dt + 2 * cols.heads
    o_hy = o_z + cols.di
    o_gate = o_hy + 3 * cols.d
    parts = [w_in[:, o_z:o_hy], w_in[:, o_gate:], w_in[:, o_hy:o_gate], w_in[:, :o_dt], w_in[:, o_dt:o_z]]
    w = jnp.concatenate(parts, axis=1)
    return jnp.pad(w, ((0, 0), (0, cols.total - w.shape[1]))).astype(BF16)


def _pad_lanes(v):
    v = v.reshape(1, -1)
    return jnp.pad(v, ((0, 0), (0, LANES - v.shape[1])))


def _mix_sublayer(x, mods, lp, cols, hf, consts, init_f, init_b):
    h = _prenorm(x, lp['norm1_g'], mods[0], mods[1])
    p, xbc, dtp = _inproj(h, lp['w_in'], 0, cols.xbc0 // COL_TILE, lp['m_conv_w'], lp['m_conv_b'])
    y_f, fin_f = _ssd(xbc, dtp, lp['dtb'], lp['alog'], init_f, cols.heads, False, True)
    ym, fin_b = _ssd(xbc, dtp, lp['dtb'], lp['alog'], init_b, cols.heads, True, True,
                     extra=(y_f, p, cols.z0, lp['dskip'], lp['m_norm_g']))
    x0, u = _hconv(p, cols.hy0, lp['h_conv_w'], lp['h_conv_b'], cols.d)
    yh = _long_conv(u, x0, hf, consts, lp['h_bias'])
    merged = _merge(ym, yh, p, cols.gate0, lp['m_w_out'], lp['h_w_out'])
    return _resid_mm(merged, lp['w_merge_out'], x, mods[2]), fin_f, fin_b


def _ctx_final_states(x, mods, lp, cols, init_f, init_b):
    h = _prenorm(x, lp['norm1_g'], mods[0], mods[1])
    _, xbc, dtp = _inproj(h, lp['w_in'], cols.xbc0, 0, lp['m_conv_w'], lp['m_conv_b'])
    _, fin_f = _ssd(xbc, dtp, lp['dtb'], lp['alog'], init_f, cols.heads, False, False)
    _, fin_b = _ssd(xbc, dtp, lp['dtb'], lp['alog'], init_b, cols.heads, True, False)
    return fin_f, fin_b


def kernel(x, c, ctx, c_ctx, ada_w, ada_b, norm1_g, w_in, m_conv_w, m_conv_b, m_dt_bias, m_a_log, m_d, m_norm_g,
           m_w_out, h_conv_w, h_conv_b, hf_w1, hf_b1, hf_w2, hf_b2, hf_freq, hf_w3, h_bias, h_w_out, w_merge_out,
           norm2_g, ffn_w_gu, ffn_w_down, final_g):
    b, l, d = x.shape
    lc = ctx.shape[1]
    depth = ada_w.shape[0]
    cols = _Cols(d)
    assert b % 2 == 0 and b + 1 <= SUBLANES and l % M_CHUNK == 0 and lc % M_CHUNK == 0

    cvecs = jnp.concatenate([c, c_ctx[None], jnp.zeros((SUBLANES - b - 1, d), F32)], axis=0)
    mods_all = _mods(cvecs, ada_w, ada_b)
    consts_l = _dft_consts(*_fft_split(2 * l))
    consts_c = _dft_consts(*_fft_split(2 * lc))

    x_l = _add_pos(x, _grid_pos_embed(l // GRID_W, d))
    x_c = ctx
    zero_state = jnp.zeros((b, M_GROUPS, M_STATE, cols.di // M_GROUPS), F32)
    for i in range(depth):
        lp = dict(norm1_g=norm1_g[i], w_in=_pack_w_in(w_in[i], cols), m_conv_w=m_conv_w[i], m_conv_b=m_conv_b[i],
                  dtb=_pad_lanes(m_dt_bias[i]), alog=_pad_lanes(m_a_log[i]),
                  dskip=jnp.repeat(m_d[i], M_HEADDIM).reshape(1, cols.di), m_norm_g=m_norm_g[i].reshape(1, cols.di),
                  m_w_out=m_w_out[i].astype(BF16), h_conv_w=h_conv_w[i], h_conv_b=h_conv_b[i],
                  hf_w1=hf_w1[i], hf_b1=hf_b1[i], hf_w2=hf_w2[i], hf_b2=hf_b2[i], hf_freq=hf_freq[i], hf_w3=hf_w3[i],
                  h_bias=h_bias[i], h_w_out=h_w_out[i].astype(BF16), w_merge_out=w_merge_out[i].astype(BF16),
                  norm2_g=norm2_g[i], ffn_w_gu=ffn_w_gu[i].astype(BF16), ffn_w_down=ffn_w_down[i].astype(BF16))
        m = mods_all[i].reshape(SUBLANES, ADA_CHUNKS, d)
        mods_l = [m[:b, k][:, None, :] for k in range(ADA_CHUNKS)]
        mods_c = [jnp.broadcast_to(m[b, k][None, None, :], (b, 1, d)) for k in range(ADA_CHUNKS)]
        if i < depth - 1:
            hf_c = _filter_spectrum(lc, consts_c, lp)
            x_c_mixed, fin_f, fin_b = _mix_sublayer(x_c, mods_c, lp, cols, hf_c, consts_c, zero_state, zero_state)
        else:
            fin_f, fin_b = _ctx_final_states(x_c, mods_c, lp, cols, zero_state, zero_state)
        hf_l = _filter_spectrum(l, consts_l, lp)
        x_l, _, _ = _mix_sublayer(x_l, mods_l, lp, cols, hf_l, consts_l, fin_f, fin_b)
        x_l = _ffn(x_l, lp['norm2_g'], mods_l[3], mods_l[4], mods_l[5], lp['ffn_w_gu'], lp['ffn_w_down'],
                   final_g=final_g if i == depth - 1 else None)
        if i < depth - 1:
            x_c = _ffn(x_c_mixed, lp['norm2_g'], mods_c[3], mods_c[4], mods_c[5], lp['ffn_w_gu'], lp['ffn_w_down'])
    return x_l
```

```python
import functools
import math

import jax
import jax.numpy as jnp
from jax import lax
from jax.experimental import pallas as pl
from jax.experimental.pallas import tpu as pltpu

F32 = jnp.float32
BF16 = jnp.bfloat16
HIGHEST = lax.Precision.HIGHEST

RMS_EPS = 1e-6
LOG2_E = 1.4426950408889634
GRID_W = 64
POS_BASE = 10000.0
ADA_CHUNKS = 6
M_HEADDIM = 64
M_GROUPS = 8
M_STATE = 128
M_CONV = 5
M_CHUNK = 128
H_SHORT = 3
H_EMB = 33
H_DECAY_TARGET = 1e-2
H_FAST_PCT = 0.3
H_SLOW_PCT = 1.5

LANES = 128
SUBLANES = 8
HALO = 16
VMEM_LIMIT_BYTES = 56 * 1024 * 1024

COL_TILE = 512
FEAT_COLS = 40
SSD_BATCH_ROWS = 2


def _cparams(*sem):
    return pltpu.CompilerParams(dimension_semantics=sem, vmem_limit_bytes=VMEM_LIMIT_BYTES)


def _silu_of_twice(h):
    return h + h * jnp.tanh(h)


def _silu(v):
    return _silu_of_twice(0.5 * v)


def _round_up(a, m):
    return -(-a // m) * m


def _mods_kernel(c_ref, w_ref, b_ref, o_ref):
    s = _silu(c_ref[...])
    o_ref[0] = jnp.dot(s, w_ref[0], preferred_element_type=F32, precision=HIGHEST) + b_ref[0]


def _mods(cvecs, ada_w, ada_b):
    depth, d, n = ada_w.shape
    tn = min(n, 1024)
    return pl.pallas_call(
        _mods_kernel,
        grid=(depth, n // tn),
        in_specs=[pl.BlockSpec((SUBLANES, d), lambda i, j: (0, 0)),
                  pl.BlockSpec((1, d, tn), lambda i, j: (i, 0, j)),
                  pl.BlockSpec((1, 1, tn), lambda i, j: (i, 0, j))],
        out_specs=pl.BlockSpec((1, SUBLANES, tn), lambda i, j: (i, 0, j)),
        out_shape=jax.ShapeDtypeStruct((depth, SUBLANES, n), F32),
        compiler_params=_cparams("parallel", "parallel"),
        name="mods",
    )(cvecs, ada_w, ada_b.reshape(depth, 1, n))


def _add_kernel(x_ref, p_ref, o_ref):
    o_ref[0] = x_ref[0] + p_ref[...]


def _add_pos(x, pos):
    b, l, d = x.shape
    tl = min(l, 1024)
    return pl.pallas_call(
        _add_kernel,
        grid=(l // tl, b),
        in_specs=[pl.BlockSpec((1, tl, d), lambda i, bb: (bb, i, 0)),
                  pl.BlockSpec((tl, d), lambda i, bb: (i, 0))],
        out_specs=pl.BlockSpec((1, tl, d), lambda i, bb: (bb, i, 0)),
        out_shape=jax.ShapeDtypeStruct(x.shape, F32),
        compiler_params=_cparams("parallel", "parallel"),
        name="add_pos",
    )(x, pos)


def _modnorm(x, g, shift, scale):
    ms = jnp.mean(x * x, axis=-1, keepdims=True)
    y = x * lax.rsqrt(ms + RMS_EPS) * g
    return y * (1.0 + scale) + shift


INPROJ_ROWS = 2048
CONV_ROWS = 256


def _prenorm_kernel(prev_ref, x_ref, next_ref, g_ref, sh_ref, sc_ref, o_ref):
    k = pl.program_id(2)
    rows = x_ref.shape[1]
    tl = o_ref.shape[2] - 2 * HALO
    norm = lambda v: _modnorm(v, g_ref[...], sh_ref[0], sc_ref[0]).astype(o_ref.dtype)
    o_ref[0, 0, pl.ds(pl.multiple_of(HALO + k * rows, HALO), rows), :] = norm(x_ref[0])

    @pl.when(k == 0)
    def _():
        o_ref[0, 0, 0:HALO, :] = norm(prev_ref[0])

    @pl.when(k == pl.num_programs(2) - 1)
    def _():
        o_ref[0, 0, HALO + tl:, :] = norm(next_ref[0])


def _prenorm(x, g, shift, scale):
    b, l, d = x.shape
    tl = min(l, INPROJ_ROWS)
    rows = min(tl, 512)
    nk = tl // rows
    rpb = tl // HALO
    last = l // HALO - 1
    vec = pl.BlockSpec((1, 1, d), lambda bb, i, k: (bb, 0, 0))
    return pl.pallas_call(
        _prenorm_kernel,
        grid=(b, l // tl, nk),
        in_specs=[pl.BlockSpec((1, HALO, d), lambda bb, i, k: (bb, jnp.maximum(i * rpb - 1, 0), 0)),
                  pl.BlockSpec((1, rows, d), lambda bb, i, k: (bb, i * nk + k, 0)),
                  pl.BlockSpec((1, HALO, d), lambda bb, i, k: (bb, jnp.minimum((i + 1) * rpb, last), 0)),
                  pl.BlockSpec((1, d), lambda bb, i, k: (0, 0)), vec, vec],
        out_specs=pl.BlockSpec((1, 1, tl + 2 * HALO, d), lambda bb, i, k: (bb, i, 0, 0)),
        out_shape=jax.ShapeDtypeStruct((b, l // tl, tl + 2 * HALO, d), BF16),
        compiler_params=_cparams("parallel", "parallel", "arbitrary"),
        name="prenorm",
    )(x, x, x, g.reshape(1, d), shift, scale)


def _inproj_kernel(*refs, n_plain, n_conv):
    if n_plain:
        h_ref, w_ref, cw_ref, cb_ref, o_ref, xbc_ref, dt_ref, acc_scr = refs
    else:
        h_ref, w_ref, cw_ref, cb_ref, xbc_ref, dt_ref, acc_scr = refs
    i = pl.program_id(1)
    k = pl.program_id(2) - n_plain
    tm = dt_ref.shape[1]
    project = lambda: jnp.dot(h_ref[0, 0], w_ref[...], preferred_element_type=F32)

    def project_into(dst):
        dst[...] = project()
        dst[0:HALO, :] = jnp.where(i > 0, dst[0:HALO, :], 0.0)
        dst[HALO + tm:, :] = jnp.where(i < pl.num_programs(1) - 1, dst[HALO + tm:, :], 0.0)

    def conv_from(src):
        pad = M_CONV // 2
        rows = min(tm, CONV_ROWS)
        w_half = 0.5 * cw_ref[...]
        b_half = 0.5 * cb_ref[...]
        for r0 in range(0, tm, rows):
            ext = src[r0:r0 + rows + 2 * HALO, :]
            acc = None
            for tap in range(M_CONV):
                shifted = ext if tap == pad else pltpu.roll(ext, (pad - tap) % (rows + 2 * HALO), axis=0)
                term = shifted[HALO:HALO + rows] * w_half[tap:tap + 1, :]
                acc = term if acc is None else acc + term
            xbc_ref[0, r0:r0 + rows, :] = _silu_of_twice(acc + b_half).astype(xbc_ref.dtype)

    if n_plain:
        @pl.when(k < 0)
        def _():
            o_ref[0] = project()[HALO:HALO + tm].astype(o_ref.dtype)

    @pl.when(jnp.logical_and(k >= 0, k < n_conv))
    def _():
        project_into(acc_scr)
        conv_from(acc_scr)

    @pl.when(k == n_conv)
    def _():
        dt_ref[0] = project()[HALO:HALO + tm, 0:LANES]


def _inproj(h, w, col0, n_plain, conv_w, conv_b):
    b, nt, tmh, d = h.shape
    tm = tmh - 2 * HALO
    l = nt * tm
    tn = COL_TILE
    j0 = col0 // tn
    n_conv = conv_w.shape[1] // tn
    nj = n_plain + n_conv + 1
    conv_idx = lambda j: jnp.clip(j - n_plain, 0, n_conv - 1)
    out_specs = [pl.BlockSpec((1, tm, tn), lambda bb, i, j: (bb, i, conv_idx(j))),
                 pl.BlockSpec((1, tm, LANES), lambda bb, i, j: (bb, i, 0))]
    out_shape = [jax.ShapeDtypeStruct((b, l, n_conv * tn), BF16), jax.ShapeDtypeStruct((b, l, LANES), F32)]
    if n_plain:
        out_specs.insert(0, pl.BlockSpec((1, tm, tn), lambda bb, i, j: (bb, i, jnp.minimum(j, n_plain - 1))))
        out_shape.insert(0, jax.ShapeDtypeStruct((b, l, n_plain * tn), BF16))
    outs = pl.pallas_call(
        functools.partial(_inproj_kernel, n_plain=n_plain, n_conv=n_conv),
        grid=(b, nt, nj),
        in_specs=[pl.BlockSpec((1, 1, tmh, d), lambda bb, i, j: (bb, i, 0, 0)),
                  pl.BlockSpec((d, tn), lambda bb, i, j: (0, j + j0)),
                  pl.BlockSpec((M_CONV, tn), lambda bb, i, j: (0, conv_idx(j))),
                  pl.BlockSpec((1, tn), lambda bb, i, j: (0, conv_idx(j)))],
        out_specs=out_specs,
        out_shape=out_shape,
        scratch_shapes=[pltpu.VMEM((tmh, tn), F32)],
        compiler_params=_cparams("parallel", "parallel", "arbitrary"),
        name="inproj",
    )(h, w, conv_w, conv_b.reshape(1, -1))
    return outs if n_plain else [None] + list(outs)


def _conv_taps(prev_ref, main_ref, next_ref, w_ref, b_ref, ktaps):
    i = pl.program_id(2)
    nblk = pl.num_programs(2)
    tl = main_ref.shape[1]
    pad = ktaps // 2
    prev = jnp.where(i > 0, prev_ref[0].astype(F32), 0.0)
    nxt = jnp.where(i < nblk - 1, next_ref[0].astype(F32), 0.0)
    ext = jnp.concatenate([prev, main_ref[0].astype(F32), nxt], axis=0)
    rows = tl + 2 * HALO
    acc = None
    for j in range(ktaps):
        shifted = ext if j == pad else pltpu.roll(ext, (pad - j) % rows, axis=0)
        term = shifted[HALO:HALO + tl] * w_ref[j:j + 1, :]
        acc = term if acc is None else acc + term
    return acc + b_ref[...]


def _conv_specs(tl, cw, l, col_block0):
    rpb = tl // HALO
    last = l // HALO - 1
    return [
        pl.BlockSpec((1, HALO, cw), lambda bb, c, i: (bb, jnp.maximum(i * rpb - 1, 0), c + col_block0)),
        pl.BlockSpec((1, tl, cw), lambda bb, c, i: (bb, i, c + col_block0)),
        pl.BlockSpec((1, HALO, cw), lambda bb, c, i: (bb, jnp.minimum((i + 1) * rpb, last), c + col_block0)),
    ]


def _hconv_kernel(p0, m0, n0, p1, m1, n1, p2, m2, n2, w0, b0, w1, b1, w2, b2, x0_ref, u_ref):
    x0_ref[0] = _conv_taps(p0, m0, n0, w0, b0, H_SHORT).astype(x0_ref.dtype)
    x1 = _conv_taps(p1, m1, n1, w1, b1, H_SHORT)
    v = _conv_taps(p2, m2, n2, w2, b2, H_SHORT)
    u_ref[0] = (x1 * v).astype(u_ref.dtype)


def _hconv(p, col0, w, bias, d):
    b, l, _ = p.shape
    tl = min(l, 512)
    cw = min(d, COL_TILE)
    nb = d // cw
    specs, wspecs = [], []
    for s in range(3):
        specs += _conv_specs(tl, cw, l, col0 // cw + s * nb)
        wspecs += [pl.BlockSpec((H_SHORT, cw), functools.partial(lambda bb, c, i, s: (0, c + s * nb), s=s)),
                   pl.BlockSpec((1, cw), functools.partial(lambda bb, c, i, s: (0, c + s * nb), s=s))]
    bias2 = bias.reshape(1, 3 * d)
    out_spec = pl.BlockSpec((1, tl, cw), lambda bb, c, i: (bb, i, c))
    return pl.pallas_call(
        _hconv_kernel,
        grid=(b, nb, l // tl),
        in_specs=specs + wspecs,
        out_specs=[out_spec, out_spec],
        out_shape=[jax.ShapeDtypeStruct((b, l, d), BF16)] * 2,
        compiler_params=_cparams("parallel", "parallel", "arbitrary"),
        name="hconv",
    )(p, p, p, p, p, p, p, p, p, w, bias2, w, bias2, w, bias2)


def _ssd_kernel(*refs, reverse, with_output, epilogue, heads):
    if epilogue:
        (xbc_ref, dt_ref, dtb_ref, alog_ref, init_ref, sel_ref, yf_ref, z_ref, dsk_ref, ng_ref,
         y_ref, fin_ref, s_scr) = refs
    elif with_output:
        xbc_ref, dt_ref, dtb_ref, alog_ref, init_ref, sel_ref, y_ref, fin_ref, s_scr = refs
    else:
        xbc_ref, dt_ref, dtb_ref, alog_ref, init_ref, fin_ref, s_scr = refs
    t = M_CHUNK
    n = M_STATE
    assert n == t
    di = heads * M_HEADDIM
    gw = di // M_GROUPS
    rpg = heads // M_GROUPS
    lane0 = heads if reverse else 0
    edge = 0 if reverse else t - 1
    c = pl.program_id(1)

    nb = xbc_ref.shape[0]

    @pl.when(c == 0)
    def _():
        s_scr[...] = init_ref[...]

    row = lax.broadcasted_iota(jnp.int32, (t, t), 0)
    col = lax.broadcasted_iota(jnp.int32, (t, t), 1)
    mask = (row <= col) if reverse else (row >= col)
    lane = lax.broadcasted_iota(jnp.int32, (1, LANES), 1)
    lo = lane < M_HEADDIM
    mlo = jnp.where(lo, 1.0, 0.0).astype(BF16)
    mhi = jnp.where(lo, 0.0, 1.0).astype(BF16)

    def chunk_decays(bi):
        x = dt_ref[bi] + dtb_ref[...]
        dt = jnp.maximum(x, 0.0) + jnp.log1p(jnp.exp(-jnp.abs(x)))
        dta = dt * (-jnp.exp(alog_ref[...]))
        acs = jnp.dot(mask.astype(F32), dta, preferred_element_type=F32, precision=HIGHEST)
        acs_t = acs.T
        dt_t = dt.T
        acs2 = acs * LOG2_E
        arow_dt2 = (acs_t - jnp.log(dt_t)) * LOG2_E
        w_t = jnp.exp(acs_t[:, edge:edge + 1] - acs_t) * dt_t
        dec = jnp.exp(acs[edge:edge + 1, :])
        eacs = jnp.exp(acs).astype(BF16) if with_output else None
        return acs2, arow_dt2, w_t, dec, eacs

    def group(bi, g, acs2, arow_dt2, w_t, dec, eacs):
        bt = xbc_ref[bi, :, di + g * n:di + (g + 1) * n].astype(F32).T
        if with_output:
            cm = xbc_ref[bi, :, di + M_GROUPS * n + g * n:di + M_GROUPS * n + (g + 1) * n]
            cb = jnp.dot(cm, bt.astype(BF16), preferred_element_type=F32)
            y_off = jnp.dot(cm, s_scr[bi, g].astype(BF16), preferred_element_type=F32)
        y_parts = []
        for q in range(rpg // 2):
            c0 = g * gw + q * LANES
            xp = xbc_ref[bi, :, c0:c0 + LANES]
            sp = s_scr[bi, g, :, q * LANES:(q + 1) * LANES]
            x2 = jnp.concatenate([xp * mlo, xp * mhi], axis=0)
            hl0 = lane0 + g * rpg + 2 * q
            lhs, btw, decs = [], [], []
            for r in range(2):
                hl = hl0 + r
                if with_output:
                    acol = jnp.broadcast_to(acs2[:, hl:hl + 1], (t, t))
                    dk = jnp.exp2(jnp.where(mask, acol - arow_dt2[hl:hl + 1, :], -jnp.inf))
                    lhs.append((cb * dk).astype(BF16))
                btw.append((bt * w_t[hl:hl + 1, :]).astype(BF16))
                decs.append(jnp.broadcast_to(dec[:, hl:hl + 1], (1, LANES)))
            if with_output:
                e_pair = jnp.dot(eacs, sel_ref[hl0 // 2], preferred_element_type=F32)
                y_diag = jnp.dot(jnp.concatenate(lhs, axis=1), x2, preferred_element_type=F32)
                y_parts.append(y_diag + e_pair * y_off[:, q * LANES:(q + 1) * LANES])
            ds = jnp.dot(jnp.concatenate(btw, axis=1), x2, preferred_element_type=F32)
            s_scr[bi, g, :, q * LANES:(q + 1) * LANES] = jnp.where(lo, decs[0], decs[1]) * sp + ds
        if with_output:
            yg = jnp.concatenate(y_parts, axis=1) if len(y_parts) > 1 else y_parts[0]
            sl = slice(g * gw, (g + 1) * gw)
            if epilogue:
                yg = yg + yf_ref[bi, :, sl].astype(F32)
                v = (yg + xbc_ref[bi, :, sl].astype(F32) * dsk_ref[:, sl]) * _silu(z_ref[bi, :, sl].astype(F32))
                ms = jnp.mean(v * v, axis=-1, keepdims=True)
                y_ref[bi, :, sl] = (v * lax.rsqrt(ms + RMS_EPS) * ng_ref[:, sl]).astype(y_ref.dtype)
            else:
                y_ref[bi, :, sl] = yg.astype(y_ref.dtype)

    decays = [chunk_decays(bi) for bi in range(nb)]
    for g in range(M_GROUPS):
        for bi in range(nb):
            group(bi, g, *decays[bi])

    @pl.when(c == pl.num_programs(1) - 1)
    def _():
        fin_ref[...] = s_scr[...]


def _pair_selectors():
    p = jnp.arange(LANES // 2)[:, None, None]
    h = jnp.arange(LANES)[None, :, None]
    lane = jnp.arange(LANES)[None, None, :]
    return (h == 2 * p + (lane >= M_HEADDIM)).astype(BF16)


def _ssd(xbc, dtp, dtb, alog, init, heads, reverse, with_output, extra=None):
    b, l, xw = xbc.shape
    t = M_CHUNK
    nc = l // t
    di = heads * M_HEADDIM
    gw = di // M_GROUPS
    epilogue = extra is not None
    cidx = (lambda c: nc - 1 - c) if reverse else (lambda c: c)
    nb = SSD_BATCH_ROWS if b % SSD_BATCH_ROWS == 0 else 1
    state_spec = pl.BlockSpec((nb, M_GROUPS, M_STATE, gw), lambda bb, c: (bb, 0, 0, 0))
    in_specs = [pl.BlockSpec((nb, t, xw), lambda bb, c: (bb, cidx(c), 0)),
                pl.BlockSpec((nb, t, LANES), lambda bb, c: (bb, cidx(c), 0)),
                pl.BlockSpec((1, LANES), lambda bb, c: (0, 0)),
                pl.BlockSpec((1, LANES), lambda bb, c: (0, 0)),
                state_spec]
    args = [xbc, dtp, dtb, alog, init]
    out_specs, out_shape = [], []
    if with_output:
        in_specs.append(pl.BlockSpec((LANES // 2, LANES, LANES), lambda bb, c: (0, 0, 0)))
        args.append(_pair_selectors())
    if epilogue:
        yf, p, z_col0, dsk, ng = extra
        in_specs += [pl.BlockSpec((nb, t, di), lambda bb, c: (bb, cidx(c), 0)),
                     pl.BlockSpec((nb, t, di), lambda bb, c: (bb, cidx(c), z_col0 // di)),
                     pl.BlockSpec((1, di), lambda bb, c: (0, 0)),
                     pl.BlockSpec((1, di), lambda bb, c: (0, 0))]
        args += [yf, p, dsk, ng]
    if with_output:
        out_specs.append(pl.BlockSpec((nb, t, di), lambda bb, c: (bb, cidx(c), 0)))
        out_shape.append(jax.ShapeDtypeStruct((b, l, di), BF16))
    out_specs.append(state_spec)
    out_shape.append(jax.ShapeDtypeStruct((b, M_GROUPS, M_STATE, gw), F32))
    outs = pl.pallas_call(
        functools.partial(_ssd_kernel, reverse=reverse, with_output=with_output, epilogue=epilogue, heads=heads),
        grid=(b // nb, nc),
        in_specs=in_specs,
        out_specs=out_specs,
        out_shape=out_shape,
        scratch_shapes=[pltpu.VMEM((nb, M_GROUPS, M_STATE, gw), F32)],
        compiler_params=_cparams("parallel", "arbitrary"),
        name="ssd_bwd" if reverse else "ssd_fwd",
    )(*args)
    return (outs[0], outs[1]) if with_output else (None, outs[0])


def _filt_kernel(f_ref, w1_ref, b1_ref, w2_ref, b2_ref, fr_ref, w3_ref, dl_ref, o_ref, ss_ref):
    f = f_ref[...]
    fr = fr_ref[...]
    h = jnp.sin(fr * (jnp.dot(f, w1_ref[...], preferred_element_type=F32, precision=HIGHEST) + b1_ref[...]))
    h = jnp.sin(fr * (jnp.dot(h, w2_ref[...], preferred_element_type=F32, precision=HIGHEST) + b2_ref[...]))
    filt = jnp.dot(h, w3_ref[...], preferred_element_type=F32, precision=HIGHEST)
    tcol = f[:, H_EMB:H_EMB + 1]
    mcol = f[:, H_EMB + 1:H_EMB + 2]
    out = filt * jnp.exp(-tcol * dl_ref[...]) * mcol
    o_ref[...] = out

    @pl.when(pl.program_id(1) == 0)
    def _():
        ss_ref[...] = jnp.zeros_like(ss_ref)

    ss_ref[...] += jnp.sum(out * out, axis=0, keepdims=True)


def _filter_features(l):
    t = jnp.linspace(0.0, 1.0, l, dtype=F32)[:, None]
    ang = (2.0 * math.pi / l) * jnp.arange(l, dtype=F32)[:, None]
    nb = (H_EMB - 1) // 2
    bands = jnp.linspace(1e-4, nb - 1, nb, dtype=F32)[None, :]
    feats = jnp.concatenate([t, jnp.cos(bands * ang), -jnp.sin(bands * ang), t, jnp.ones_like(t)], axis=-1)
    rev = jnp.concatenate([feats[0:1] * 0.0, jnp.flip(feats[1:], axis=0)], axis=0)
    full = jnp.concatenate([feats, rev], axis=0)
    return jnp.pad(full, ((0, 0), (0, FEAT_COLS - full.shape[1])))


def _hyena_filter(l, w1, b1, w2, b2, freq, w3):
    hid = w2.shape[0]
    d = w3.shape[1] // 2
    feats = _filter_features(l)
    w1p = jnp.pad(w1, ((0, FEAT_COLS - w1.shape[0]), (0, 0)))
    deltas = jnp.abs(jnp.linspace(math.log(H_DECAY_TARGET) / H_SLOW_PCT, math.log(H_DECAY_TARGET) / H_FAST_PCT,
                                  d, dtype=F32))[None, :]
    tl = min(l, 512)
    cw = d
    nb = d // cw
    nrow = l // tl
    small = lambda shape: pl.BlockSpec(shape, lambda c, i: (0, 0))
    return pl.pallas_call(
        _filt_kernel,
        grid=(nb, 2 * nrow),
        in_specs=[pl.BlockSpec((tl, FEAT_COLS), lambda c, i: (i, 0)),
                  small((FEAT_COLS, hid)), small((1, hid)), small((hid, hid)), small((1, hid)), small((1, hid)),
                  pl.BlockSpec((hid, cw), lambda c, i: (0, c + (i // nrow) * nb)),
                  pl.BlockSpec((1, cw), lambda c, i: (0, c))],
        out_specs=[pl.BlockSpec((tl, cw), lambda c, i: (i, c)),
                   pl.BlockSpec((1, cw), lambda c, i: (0, c))],
        out_shape=[jax.ShapeDtypeStruct((2 * l, d), F32), jax.ShapeDtypeStruct((1, d), F32)],
        compiler_params=_cparams("parallel", "arbitrary"),
        name="hyena_filter",
    )(feats, w1p, b1.reshape(1, hid), w2, b2.reshape(1, hid), freq.reshape(1, hid), w3, deltas)


def _fft_split(n):
    n1 = 1 << ((n.bit_length() - 1 + 1) // 2)
    return n1, n // n1


def _dft_consts(n1, n2):
    n = n1 * n2
    h = n1 // 2
    k1 = jnp.arange(n1, dtype=jnp.int32)
    t1 = jnp.arange(n1, dtype=jnp.int32)
    t2 = jnp.arange(n2, dtype=jnp.int32)
    idx = (k1[None, :, None] * (n2 * t1[None, None, :] + t2[:, None, None])) % n
    ang = idx.astype(F32) * (2.0 * math.pi / n)
    cs, sn = jnp.cos(ang), jnp.sin(ang)
    ch, sh = cs[..., :h], sn[..., :h]
    t_data = jnp.concatenate([jnp.concatenate([ch, sh], -1), jnp.concatenate([-sh, ch], -1)], axis=1)
    t_filt = jnp.concatenate([cs, -sn], axis=1)
    ct, st = jnp.swapaxes(ch, 1, 2), jnp.swapaxes(sh, 1, 2)
    t_inv = jnp.concatenate([jnp.concatenate([ct, -st], -1), jnp.concatenate([st, ct], -1)], axis=1) / n
    k2 = jnp.arange(n2, dtype=jnp.int32)
    ang2 = ((k2[:, None] * k2[None, :]) % n2).astype(F32) * (2.0 * math.pi / n2)
    c2, s2 = jnp.cos(ang2), jnp.sin(ang2)
    m_fwd = jnp.concatenate([jnp.concatenate([c2, s2], -1), jnp.concatenate([-s2, c2], -1)], axis=0)
    m_inv = jnp.concatenate([jnp.concatenate([c2, -s2], -1), jnp.concatenate([s2, c2], -1)], axis=0)
    return dict(t_data=t_data.astype(BF16), t_filt=t_filt.astype(BF16), t_inv=t_inv.astype(BF16),
                m_fwd=m_fwd.astype(BF16), m_inv=m_inv.astype(BF16))


def _fft1_kernel(x_ref, t_ref, o_ref):
    a = x_ref.shape[1]
    xs = [pltpu.einshape("rjc->jrc", x_ref[0, s].astype(F32)) for s in range(a)]
    n1 = o_ref.shape[2]
    for j in range(SUBLANES):
        x = jnp.concatenate([xs[s][j] for s in range(a)], axis=0).astype(BF16)
        o = jnp.dot(t_ref[j], x, preferred_element_type=F32)
        o_ref[0, j] = _pack_complex(o[:n1], o[n1:])


def _pack_complex(re, im):
    r = lax.bitcast_convert_type(re.astype(BF16).astype(F32), jnp.uint32)
    i = lax.bitcast_convert_type(im.astype(BF16).astype(F32), jnp.uint32)
    return (r >> 16) | i


def _unpack_complex(w):
    re = lax.bitcast_convert_type(w << 16, F32)
    im = lax.bitcast_convert_type(w & jnp.uint32(0xFFFF0000), F32)
    return re, im


def _fft_stage1(xv, tmat, n1, n2, d):
    p, a, r, _, _ = xv.shape
    cw = min(d, COL_TILE)
    return pl.pallas_call(
        _fft1_kernel,
        grid=(n2 // SUBLANES, d // cw, p),
        in_specs=[pl.BlockSpec((1, a, r, SUBLANES, cw), lambda j, c, pp: (pp, 0, 0, j, c)),
                  pl.BlockSpec((SUBLANES, 2 * n1, n1), lambda j, c, pp: (j, 0, 0))],
        out_specs=pl.BlockSpec((1, SUBLANES, n1, cw), lambda j, c, pp: (pp, j, 0, c)),
        out_shape=jax.ShapeDtypeStruct((p, n2, n1, d), jnp.uint32),
        compiler_params=_cparams("parallel", "parallel", "parallel"),
        name="fft_stage1",
    )(xv, tmat)


def _fft2_kernel(a_ref, af_ref, ss_ref, mf_ref, mi_ref, o_ref, h_scr):
    n2 = a_ref.shape[1]

    def stage(w):
        re, im = _unpack_complex(w)
        return jnp.dot(mf_ref[...], jnp.concatenate([re, im], axis=0).astype(BF16), preferred_element_type=F32)

    @pl.when(pl.program_id(2) == 0)
    def _():
        af = pltpu.einshape("tkc->ktc", af_ref[0])
        for k in range(SUBLANES):
            h_scr[k] = stage(af[k]) * lax.rsqrt(ss_ref[...] + RMS_EPS)

    a = pltpu.einshape("tkc->ktc", a_ref[0])
    outs = []
    for k in range(SUBLANES):
        xk = stage(a[k])
        xr, xi = xk[:n2], xk[n2:]
        hr, hi = h_scr[k, 0:n2, :], h_scr[k, n2:, :]
        z = jnp.concatenate([xr * hr - xi * hi, xr * hi + xi * hr], axis=0).astype(BF16)
        bk = jnp.dot(mi_ref[...], z, preferred_element_type=F32)
        outs.append(_pack_complex(bk[:n2], bk[n2:]))
    o_ref[0] = pltpu.einshape("ktc->tkc", jnp.stack(outs, axis=0))


def _fft_stage2(av, afv, ss, m_fwd, m_inv, n1, n2, d):
    p = av.shape[0]
    cw = min(d, COL_TILE)
    kb = SUBLANES
    return pl.pallas_call(
        _fft2_kernel,
        grid=(n1 // kb, d // cw, p),
        in_specs=[pl.BlockSpec((1, n2, kb, cw), lambda k, c, pp: (pp, 0, k, c)),
                  pl.BlockSpec((1, n2, kb, cw), lambda k, c, pp: (0, 0, k, c)),
                  pl.BlockSpec((1, cw), lambda k, c, pp: (0, c)),
                  pl.BlockSpec((2 * n2, 2 * n2), lambda k, c, pp: (0, 0)),
                  pl.BlockSpec((2 * n2, 2 * n2), lambda k, c, pp: (0, 0))],
        out_specs=pl.BlockSpec((1, n2, kb, cw), lambda k, c, pp: (pp, 0, k, c)),
        out_shape=jax.ShapeDtypeStruct(av.shape, jnp.uint32),
        scratch_shapes=[pltpu.VMEM((kb, 2 * n2, cw), F32)],
        compiler_params=_cparams("parallel", "parallel", "arbitrary"),
        name="fft_stage2",
    )(av, afv, ss, m_fwd, m_inv)


def _fft3_kernel(b_ref, t_ref, u_ref, x0_ref, bias_ref, o_ref):
    n1 = b_ref.shape[2]
    h = n1 // 2
    ys = []
    for j in range(SUBLANES):
        re, im = _unpack_complex(b_ref[0, j])
        bv = jnp.concatenate([re, im], axis=0).astype(BF16)
        ys.append(jnp.dot(t_ref[j], bv, preferred_element_type=F32))
    y = pltpu.einshape("jtc->tjc", jnp.stack(ys, axis=0))
    for s in range(2):
        conv = y[s * h:(s + 1) * h] + u_ref[0, s].astype(F32) * bias_ref[...]
        o_ref[0, s] = (x0_ref[0, s].astype(F32) * conv).astype(o_ref.dtype)


def _fft_stage3(bv, t_inv, uv, x0v, bias, n1, n2, d):
    p = bv.shape[0]
    h = n1 // 2
    cw = min(d, COL_TILE)
    io_spec = pl.BlockSpec((1, 2, h, SUBLANES, cw), lambda j, c, pp: (pp, 0, 0, j, c))
    return pl.pallas_call(
        _fft3_kernel,
        grid=(n2 // SUBLANES, d // cw, p),
        in_specs=[pl.BlockSpec((1, SUBLANES, n1, cw), lambda j, c, pp: (pp, j, 0, c)),
                  pl.BlockSpec((SUBLANES, n1, 2 * n1), lambda j, c, pp: (j, 0, 0)),
                  io_spec, io_spec,
                  pl.BlockSpec((1, cw), lambda j, c, pp: (0, c))],
        out_specs=io_spec,
        out_shape=jax.ShapeDtypeStruct(uv.shape, BF16),
        compiler_params=_cparams("parallel", "parallel", "parallel"),
        name="fft_stage3",
    )(bv, t_inv, uv, x0v, bias)


def _filter_spectrum(l, consts, lp):
    n1, n2 = _fft_split(2 * l)
    d = lp['h_bias'].shape[0]
    full, ss = _hyena_filter(l, lp['hf_w1'], lp['hf_b1'], lp['hf_w2'], lp['hf_b2'], lp['hf_freq'], lp['hf_w3'])
    return _fft_stage1(full.reshape(1, 1, n1, n2, d), consts['t_filt'], n1, n2, d), ss


def _long_conv(u, x0, hf, consts, bias):
    b, l, d = u.shape
    n1, n2 = _fft_split(2 * l)
    p = b // 2
    h = n1 // 2
    uv = u.reshape(p, 2, h, n2, d)
    a = _fft_stage1(uv, consts['t_data'], n1, n2, d)
    bv = _fft_stage2(a, hf[0], hf[1], consts['m_fwd'], consts['m_inv'], n1, n2, d)
    y = _fft_stage3(bv, consts['t_inv'], uv, x0.reshape(p, 2, h, n2, d), bias.reshape(1, d), n1, n2, d)
    return y.reshape(b, l, d)


def _merge_kernel(ym_ref, yh_ref, ga_ref, gb_ref, wm_ref, wh_ref, o_ref):
    a = jnp.dot(ym_ref[0], wm_ref[...], preferred_element_type=F32)
    bb = jnp.dot(yh_ref[0], wh_ref[...], preferred_element_type=F32)
    ga = jax.nn.sigmoid(ga_ref[0].astype(F32))
    gb = jax.nn.sigmoid(gb_ref[0].astype(F32))
    o_ref[0] = (ga * a + gb * bb).astype(o_ref.dtype)


def _merge(ym, yh, p, gate_col0, wm, wh):
    b, l, di = ym.shape
    d = yh.shape[2]
    tm = min(l, 1024)
    tn = min(d, COL_TILE)
    g0 = gate_col0 // tn
    nb = d // tn
    return pl.pallas_call(
        _merge_kernel,
        grid=(b, l // tm, nb),
        in_specs=[pl.BlockSpec((1, tm, di), lambda bb, i, j: (bb, i, 0)),
                  pl.BlockSpec((1, tm, d), lambda bb, i, j: (bb, i, 0)),
                  pl.BlockSpec((1, tm, tn), lambda bb, i, j: (bb, i, g0 + j)),
                  pl.BlockSpec((1, tm, tn), lambda bb, i, j: (bb, i, g0 + nb + j)),
                  pl.BlockSpec((di, tn), lambda bb, i, j: (0, j)),
                  pl.BlockSpec((d, tn), lambda bb, i, j: (0, j))],
        out_specs=pl.BlockSpec((1, tm, tn), lambda bb, i, j: (bb, i, j)),
        out_shape=jax.ShapeDtypeStruct((b, l, d), BF16),
        compiler_params=_cparams("parallel", "parallel", "arbitrary"),
        name="merge",
    )(ym, yh, p, p, wm, wh)


def _resid_mm_kernel(a_ref, w_ref, x_ref, g_ref, o_ref):
    o_ref[0] = x_ref[0] + g_ref[0] * jnp.dot(a_ref[0], w_ref[...], preferred_element_type=F32)


def _resid_mm(a, w, x, gate):
    b, l, k = a.shape
    d = w.shape[1]
    tm = min(l, 1024)
    tn = min(d, COL_TILE)
    return pl.pallas_call(
        _resid_mm_kernel,
        grid=(b, l // tm, d // tn),
        in_specs=[pl.BlockSpec((1, tm, k), lambda bb, i, j: (bb, i, 0)),
                  pl.BlockSpec((k, tn), lambda bb, i, j: (0, j)),
                  pl.BlockSpec((1, tm, tn), lambda bb, i, j: (bb, i, j)),
                  pl.BlockSpec((1, 1, tn), lambda bb, i, j: (bb, 0, j))],
        out_specs=pl.BlockSpec((1, tm, tn), lambda bb, i, j: (bb, i, j)),
        out_shape=jax.ShapeDtypeStruct((b, l, d), F32),
        compiler_params=_cparams("parallel", "parallel", "arbitrary"),
        name="resid_mm",
    )(a, w, x, gate)


def _ffn_kernel(x_ref, g_ref, sh_ref, sc_ref, gt_ref, wg_ref, wu_ref, wd_ref, fg_ref, o_ref, h_scr, *, final_norm):
    j = pl.program_id(2)

    @pl.when(j == 0)
    def _():
        h_scr[...] = _modnorm(x_ref[0], g_ref[...], sh_ref[0], sc_ref[0]).astype(BF16)
        o_ref[...] = jnp.zeros_like(o_ref)

    tm = h_scr.shape[0]
    rows_per_pass = min(tm, 512)
    for r0 in range(0, tm, rows_per_pass):
        h = h_scr[r0:r0 + rows_per_pass, :]
        gg = jnp.dot(h, wg_ref[...], preferred_element_type=F32)
        uu = jnp.dot(h, wu_ref[...], preferred_element_type=F32)
        o_ref[0, r0:r0 + rows_per_pass, :] += jnp.dot((_silu(gg) * uu).astype(BF16), wd_ref[...],
                                                      preferred_element_type=F32)

    @pl.when(j == pl.num_programs(2) - 1)
    def _():
        y = x_ref[0] + gt_ref[0] * o_ref[0]
        if final_norm:
            ms = jnp.mean(y * y, axis=-1, keepdims=True)
            y = y * lax.rsqrt(ms + RMS_EPS) * fg_ref[...]
        o_ref[0] = y


def _ffn(x, g, shift, scale, gate, w_gu, w_down, final_g=None):
    b, l, d = x.shape
    f = w_down.shape[0]
    tm = min(l, 1024)
    fg = jnp.ones((1, d), F32) if final_g is None else final_g.reshape(1, d)
    tf = 256
    nf = f // tf
    vec = pl.BlockSpec((1, 1, d), lambda bb, i, j: (bb, 0, 0))
    return pl.pallas_call(
        functools.partial(_ffn_kernel, final_norm=final_g is not None),
        grid=(b, l // tm, nf),
        in_specs=[pl.BlockSpec((1, tm, d), lambda bb, i, j: (bb, i, 0)),
                  pl.BlockSpec((1, d), lambda bb, i, j: (0, 0)),
                  vec, vec, vec,
                  pl.BlockSpec((d, tf), lambda bb, i, j: (0, j)),
                  pl.BlockSpec((d, tf), lambda bb, i, j: (0, j + nf)),
                  pl.BlockSpec((tf, d), lambda bb, i, j: (j, 0)),
                  pl.BlockSpec((1, d), lambda bb, i, j: (0, 0))],
        out_specs=pl.BlockSpec((1, tm, d), lambda bb, i, j: (bb, i, 0)),
        out_shape=jax.ShapeDtypeStruct((b, l, d), F32),
        scratch_shapes=[pltpu.VMEM((tm, d), BF16)],
        compiler_params=_cparams("parallel", "parallel", "arbitrary"),
        name="ffn",
    )(x, g.reshape(1, d), shift, scale, gate, w_gu, w_gu, w_down, fg)


def _grid_pos_embed(rows, d):
    r, col = jnp.meshgrid(jnp.arange(rows), jnp.arange(GRID_W), indexing='ij')
    quarter = d // 4
    omega = 1.0 / (POS_BASE ** (jnp.arange(quarter, dtype=F32) / quarter))

    def axis_embed(pos):
        ang = pos.reshape(-1)[:, None].astype(F32) * omega[None, :]
        return jnp.concatenate([jnp.sin(ang), jnp.cos(ang)], axis=-1)

    return jnp.concatenate([axis_embed(r), axis_embed(col)], axis=-1).astype(F32)


class _Cols:
    def __init__(self, d):
        self.d = d
        self.di = 2 * d
        self.heads = self.di // M_HEADDIM
        self.xbc = self.di + 2 * M_GROUPS * M_STATE
        self.z0 = 0
        self.gate0 = self.z0 + self.di
        self.hy0 = self.gate0 + 2 * d
        self.xbc0 = self.hy0 + 3 * d
        self.dt0 = self.xbc0 + self.xbc
        self.total = self.dt0 + COL_TILE
        assert 2 * self.heads <= LANES and self.heads % (2 * M_GROUPS) == 0
        for off in (self.gate0, self.hy0, self.xbc0, self.dt0):
            assert off % COL_TILE == 0


def _pack_w_in(w_in, cols):
    o_dt = cols.xbc
    o_z = o_dt + 2 * cols.heads
    o_hy = o_z + cols.di
    o_gate = o_hy + 3 * cols.d
    parts = [w_in[:, o_z:o_hy], w_in[:, o_gate:], w_in[:, o_hy:o_gate], w_in[:, :o_dt], w_in[:, o_dt:o_z]]
    w = jnp.concatenate(parts, axis=1)
    return jnp.pad(w, ((0, 0), (0, cols.total - w.shape[1]))).astype(BF16)


def _pad_lanes(v):
    v = v.reshape(1, -1)
    return jnp.pad(v, ((0, 0), (0, LANES - v.shape[1])))


def _mix_sublayer(x, mods, lp, cols, hf, consts, init_f, init_b):
    h = _prenorm(x, lp['norm1_g'], mods[0], mods[1])
    p, xbc, dtp = _inproj(h, lp['w_in'], 0, cols.xbc0 // COL_TILE, lp['m_conv_w'], lp['m_conv_b'])
    y_f, fin_f = _ssd(xbc, dtp, lp['dtb'], lp['alog'], init_f, cols.heads, False, True)
    ym, fin_b = _ssd(xbc, dtp, lp['dtb'], lp['alog'], init_b, cols.heads, True, True,
                     extra=(y_f, p, cols.z0, lp['dskip'], lp['m_norm_g']))
    x0, u = _hconv(p, cols.hy0, lp['h_conv_w'], lp['h_conv_b'], cols.d)
    yh = _long_conv(u, x0, hf, consts, lp['h_bias'])
    merged = _merge(ym, yh, p, cols.gate0, lp['m_w_out'], lp['h_w_out'])
    return _resid_mm(merged, lp['w_merge_out'], x, mods[2]), fin_f, fin_b


def _ctx_final_states(x, mods, lp, cols, init_f, init_b):
    h = _prenorm(x, lp['norm1_g'], mods[0], mods[1])
    _, xbc, dtp = _inproj(h, lp['w_in'], cols.xbc0, 0, lp['m_conv_w'], lp['m_conv_b'])
    _, fin_f = _ssd(xbc, dtp, lp['dtb'], lp['alog'], init_f, cols.heads, False, False)
    _, fin_b = _ssd(xbc, dtp, lp['dtb'], lp['alog'], init_b, cols.heads, True, False)
    return fin_f, fin_b


def kernel(x, c, ctx, c_ctx, ada_w, ada_b, norm1_g, w_in, m_conv_w, m_conv_b, m_dt_bias, m_a_log, m_d, m_norm_g,
           m_w_out, h_conv_w, h_conv_b, hf_w1, hf_b1, hf_w2, hf_b2, hf_freq, hf_w3, h_bias, h_w_out, w_merge_out,
           norm2_g, ffn_w_gu, ffn_w_down, final_g):
    b, l, d = x.shape
    lc = ctx.shape[1]
    depth = ada_w.shape[0]
    cols = _Cols(d)
    assert b % 2 == 0 and b + 1 <= SUBLANES and l % M_CHUNK == 0 and lc % M_CHUNK == 0

    cvecs = jnp.concatenate([c, c_ctx[None], jnp.zeros((SUBLANES - b - 1, d), F32)], axis=0)
    mods_all = _mods(cvecs, ada_w, ada_b)
    consts_l = _dft_consts(*_fft_split(2 * l))
    consts_c = _dft_consts(*_fft_split(2 * lc))

    x_l = _add_pos(x, _grid_pos_embed(l // GRID_W, d))
    x_c = ctx
    zero_state = jnp.zeros((b, M_GROUPS, M_STATE, cols.di // M_GROUPS), F32)
    for i in range(depth):
        lp = dict(norm1_g=norm1_g[i], w_in=_pack_w_in(w_in[i], cols), m_conv_w=m_conv_w[i], m_conv_b=m_conv_b[i],
                  dtb=_pad_lanes(m_dt_bias[i]), alog=_pad_lanes(m_a_log[i]),
                  dskip=jnp.repeat(m_d[i], M_HEADDIM).reshape(1, cols.di), m_norm_g=m_norm_g[i].reshape(1, cols.di),
                  m_w_out=m_w_out[i].astype(BF16), h_conv_w=h_conv_w[i], h_conv_b=h_conv_b[i],
                  hf_w1=hf_w1[i], hf_b1=hf_b1[i], hf_w2=hf_w2[i], hf_b2=hf_b2[i], hf_freq=hf_freq[i], hf_w3=hf_w3[i],
                  h_bias=h_bias[i], h_w_out=h_w_out[i].astype(BF16), w_merge_out=w_merge_out[i].astype(BF16),
                  norm2_g=norm2_g[i], ffn_w_gu=ffn_w_gu[i].astype(BF16), ffn_w_down=ffn_w_down[i].astype(BF16))
        m = mods_all[i].reshape(SUBLANES, ADA_CHUNKS, d)
        mods_l = [m[:b, k][:, None, :] for k in range(ADA_CHUNKS)]
        mods_c = [jnp.broadcast_to(m[b, k][None, None, :], (b, 1, d)) for k in range(ADA_CHUNKS)]
        if i < depth - 1:
            hf_c = _filter_spectrum(lc, consts_c, lp)
            x_c_mixed, fin_f, fin_b = _mix_sublayer(x_c, mods_c, lp, cols, hf_c, consts_c, zero_state, zero_state)
        else:
            fin_f, fin_b = _ctx_final_states(x_c, mods_c, lp, cols, zero_state, zero_state)
        hf_l = _filter_spectrum(l, consts_l, lp)
        x_l, _, _ = _mix_sublayer(x_l, mods_l, lp, cols, hf_l, consts_l, fin_f, fin_b)
        x_l = _ffn(x_l, lp['norm2_g'], mods_l[3], mods_l[4], mods_l[5], lp['ffn_w_gu'], lp['ffn_w_down'],
                   final_g=final_g if i == depth - 1 else None)
        if i < depth - 1:
            x_c = _ffn(x_c_mixed, lp['norm2_g'], mods_c[3], mods_c[4], mods_c[5], lp['ffn_w_gu'], lp['ffn_w_down'])
    return x_l
```

```python
import functools
import math

import jax
import jax.numpy as jnp
from jax import lax
from jax.experimental import pallas as pl
from jax.experimental.pallas import tpu as pltpu

F32 = jnp.float32
BF16 = jnp.bfloat16
HIGHEST = lax.Precision.HIGHEST

RMS_EPS = 1e-6
LOG2_E = 1.4426950408889634
GRID_W = 64
POS_BASE = 10000.0
ADA_CHUNKS = 6
M_HEADDIM = 64
M_GROUPS = 8
M_STATE = 128
M_CONV = 5
M_CHUNK = 128
H_SHORT = 3
H_EMB = 33
H_DECAY_TARGET = 1e-2
H_FAST_PCT = 0.3
H_SLOW_PCT = 1.5

LANES = 128
SUBLANES = 8
HALO = 16
VMEM_LIMIT_BYTES = 56 * 1024 * 1024

COL_TILE = 512
FEAT_COLS = 40
SSD_BATCH_ROWS = 2


def _cparams(*sem):
    return pltpu.CompilerParams(dimension_semantics=sem, vmem_limit_bytes=VMEM_LIMIT_BYTES)


def _silu_of_twice(h):
    return h + h * jnp.tanh(h)


def _silu(v):
    return _silu_of_twice(0.5 * v)


def _round_up(a, m):
    return -(-a // m) * m


def _mods_kernel(c_ref, w_ref, b_ref, o_ref):
    s = _silu(c_ref[...])
    o_ref[0] = jnp.dot(s, w_ref[0], preferred_element_type=F32, precision=HIGHEST) + b_ref[0]


def _mods(cvecs, ada_w, ada_b):
    depth, d, n = ada_w.shape
    tn = min(n, 1024)
    return pl.pallas_call(
        _mods_kernel,
        grid=(depth, n // tn),
        in_specs=[pl.BlockSpec((SUBLANES, d), lambda i, j: (0, 0)),
                  pl.BlockSpec((1, d, tn), lambda i, j: (i, 0, j)),
                  pl.BlockSpec((1, 1, tn), lambda i, j: (i, 0, j))],
        out_specs=pl.BlockSpec((1, SUBLANES, tn), lambda i, j: (i, 0, j)),
        out_shape=jax.ShapeDtypeStruct((depth, SUBLANES, n), F32),
        compiler_params=_cparams("parallel", "parallel"),
        name="mods",
    )(cvecs, ada_w, ada_b.reshape(depth, 1, n))


def _add_kernel(x_ref, p_ref, o_ref):
    o_ref[0] = x_ref[0] + p_ref[...]


def _add_pos(x, pos):
    b, l, d = x.shape
    tl = min(l, 1024)
    return pl.pallas_call(
        _add_kernel,
        grid=(l // tl, b),
        in_specs=[pl.BlockSpec((1, tl, d), lambda i, bb: (bb, i, 0)),
                  pl.BlockSpec((tl, d), lambda i, bb: (i, 0))],
        out_specs=pl.BlockSpec((1, tl, d), lambda i, bb: (bb, i, 0)),
        out_shape=jax.ShapeDtypeStruct(x.shape, F32),
        compiler_params=_cparams("parallel", "parallel"),
        name="add_pos",
    )(x, pos)


def _modnorm(x, g, shift, scale):
    ms = jnp.mean(x * x, axis=-1, keepdims=True)
    y = x * lax.rsqrt(ms + RMS_EPS) * g
    return y * (1.0 + scale) + shift


INPROJ_ROWS = 2048
CONV_ROWS = 256


def _prenorm_kernel(prev_ref, x_ref, next_ref, g_ref, sh_ref, sc_ref, o_ref):
    k = pl.program_id(2)
    rows = x_ref.shape[1]
    tl = o_ref.shape[2] - 2 * HALO
    norm = lambda v: _modnorm(v, g_ref[...], sh_ref[0], sc_ref[0]).astype(o_ref.dtype)
    o_ref[0, 0, pl.ds(pl.multiple_of(HALO + k * rows, HALO), rows), :] = norm(x_ref[0])

    @pl.when(k == 0)
    def _():
        o_ref[0, 0, 0:HALO, :] = norm(prev_ref[0])

    @pl.when(k == pl.num_programs(2) - 1)
    def _():
        o_ref[0, 0, HALO + tl:, :] = norm(next_ref[0])


def _prenorm(x, g, shift, scale):
    b, l, d = x.shape
    tl = min(l, INPROJ_ROWS)
    rows = min(tl, 512)
    nk = tl // rows
    rpb = tl // HALO
    last = l // HALO - 1
    vec = pl.BlockSpec((1, 1, d), lambda bb, i, k: (bb, 0, 0))
    return pl.pallas_call(
        _prenorm_kernel,
        grid=(b, l // tl, nk),
        in_specs=[pl.BlockSpec((1, HALO, d), lambda bb, i, k: (bb, jnp.maximum(i * rpb - 1, 0), 0)),
                  pl.BlockSpec((1, rows, d), lambda bb, i, k: (bb, i * nk + k, 0)),
                  pl.BlockSpec((1, HALO, d), lambda bb, i, k: (bb, jnp.minimum((i + 1) * rpb, last), 0)),
                  pl.BlockSpec((1, d), lambda bb, i, k: (0, 0)), vec, vec],
        out_specs=pl.BlockSpec((1, 1, tl + 2 * HALO, d), lambda bb, i, k: (bb, i, 0, 0)),
        out_shape=jax.ShapeDtypeStruct((b, l // tl, tl + 2 * HALO, d), BF16),
        compiler_params=_cparams("parallel", "parallel", "arbitrary"),
        name="prenorm",
    )(x, x, x, g.reshape(1, d), shift, scale)


def _inproj_kernel(*refs, n_plain, n_conv):
    if n_plain:
        h_ref, w_ref, cw_ref, cb_ref, o_ref, xbc_ref, dt_ref, acc_scr = refs
    else:
        h_ref, w_ref, cw_ref, cb_ref, xbc_ref, dt_ref, acc_scr = refs
    i = pl.program_id(1)
    k = pl.program_id(2) - n_plain
    tm = dt_ref.shape[1]
    project = lambda: jnp.dot(h_ref[0, 0], w_ref[...], preferred_element_type=F32)

    def project_into(dst):
        dst[...] = project()
        dst[0:HALO, :] = jnp.where(i > 0, dst[0:HALO, :], 0.0)
        dst[HALO + tm:, :] = jnp.where(i < pl.num_programs(1) - 1, dst[HALO + tm:, :], 0.0)

    def conv_from(src):
        pad = M_CONV // 2
        rows = min(tm, CONV_ROWS)
        w_half = 0.5 * cw_ref[...]
        b_half = 0.5 * cb_ref[...]
        for r0 in range(0, tm, rows):
            ext = src[r0:r0 + rows + 2 * HALO, :]
            acc = None
            for tap in range(M_CONV):
                shifted = ext if tap == pad else pltpu.roll(ext, (pad - tap) % (rows + 2 * HALO), axis=0)
                term = shifted[HALO:HALO + rows] * w_half[tap:tap + 1, :]
                acc = term if acc is None else acc + term
            xbc_ref[0, r0:r0 + rows, :] = _silu_of_twice(acc + b_half).astype(xbc_ref.dtype)

    if n_plain:
        @pl.when(k < 0)
        def _():
            o_ref[0] = project()[HALO:HALO + tm].astype(o_ref.dtype)

    @pl.when(jnp.logical_and(k >= 0, k < n_conv))
    def _():
        project_into(acc_scr)
        conv_from(acc_scr)

    @pl.when(k == n_conv)
    def _():
        dt_ref[0] = project()[HALO:HALO + tm, 0:LANES]


def _proj_kernel(h_ref, w_ref, o_ref):
    tm = o_ref.shape[1]
    acc = jnp.dot(h_ref[0, 0], w_ref[...], preferred_element_type=F32)
    o_ref[0] = acc[HALO:HALO + tm].astype(o_ref.dtype)


def _proj_plain(h, w, ncols):
    b, nt, tmh, d = h.shape
    tm = tmh - 2 * HALO
    tn = 2 * COL_TILE if ncols % (2 * COL_TILE) == 0 else COL_TILE
    return pl.pallas_call(
        _proj_kernel,
        grid=(b, nt, ncols // tn),
        in_specs=[pl.BlockSpec((1, 1, tmh, d), lambda bb, i, j: (bb, i, 0, 0)),
                  pl.BlockSpec((d, tn), lambda bb, i, j: (0, j))],
        out_specs=pl.BlockSpec((1, tm, tn), lambda bb, i, j: (bb, i, j)),
        out_shape=jax.ShapeDtypeStruct((b, nt * tm, ncols), BF16),
        compiler_params=_cparams("parallel", "parallel", "arbitrary"),
        name="proj_plain",
    )(h, w)


def _inproj(h, w, col0, n_plain, conv_w, conv_b):
    b, nt, tmh, d = h.shape
    tm = tmh - 2 * HALO
    l = nt * tm
    tn = COL_TILE
    j0 = col0 // tn
    n_conv = conv_w.shape[1] // tn
    nj = n_plain + n_conv + 1
    conv_idx = lambda j: jnp.clip(j - n_plain, 0, n_conv - 1)
    out_specs = [pl.BlockSpec((1, tm, tn), lambda bb, i, j: (bb, i, conv_idx(j))),
                 pl.BlockSpec((1, tm, LANES), lambda bb, i, j: (bb, i, 0))]
    out_shape = [jax.ShapeDtypeStruct((b, l, n_conv * tn), BF16), jax.ShapeDtypeStruct((b, l, LANES), F32)]
    if n_plain:
        out_specs.insert(0, pl.BlockSpec((1, tm, tn), lambda bb, i, j: (bb, i, jnp.minimum(j, n_plain - 1))))
        out_shape.insert(0, jax.ShapeDtypeStruct((b, l, n_plain * tn), BF16))
    outs = pl.pallas_call(
        functools.partial(_inproj_kernel, n_plain=n_plain, n_conv=n_conv),
        grid=(b, nt, nj),
        in_specs=[pl.BlockSpec((1, 1, tmh, d), lambda bb, i, j: (bb, i, 0, 0)),
                  pl.BlockSpec((d, tn), lambda bb, i, j: (0, j + j0)),
                  pl.BlockSpec((M_CONV, tn), lambda bb, i, j: (0, conv_idx(j))),
                  pl.BlockSpec((1, tn), lambda bb, i, j: (0, conv_idx(j)))],
        out_specs=out_specs,
        out_shape=out_shape,
        scratch_shapes=[pltpu.VMEM((tmh, tn), F32)],
        compiler_params=_cparams("parallel", "parallel", "arbitrary"),
        name="inproj",
    )(h, w, conv_w, conv_b.reshape(1, -1))
    return outs if n_plain else [None] + list(outs)


def _conv_taps(prev_ref, main_ref, next_ref, w_ref, b_ref, ktaps):
    i = pl.program_id(2)
    nblk = pl.num_programs(2)
    tl = main_ref.shape[1]
    pad = ktaps // 2
    prev = jnp.where(i > 0, prev_ref[0].astype(F32), 0.0)
    nxt = jnp.where(i < nblk - 1, next_ref[0].astype(F32), 0.0)
    ext = jnp.concatenate([prev, main_ref[0].astype(F32), nxt], axis=0)
    rows = tl + 2 * HALO
    acc = None
    for j in range(ktaps):
        shifted = ext if j == pad else pltpu.roll(ext, (pad - j) % rows, axis=0)
        term = shifted[HALO:HALO + tl] * w_ref[j:j + 1, :]
        acc = term if acc is None else acc + term
    return acc + b_ref[...]


def _conv_specs(tl, cw, l, col_block0):
    rpb = tl // HALO
    last = l // HALO - 1
    return [
        pl.BlockSpec((1, HALO, cw), lambda bb, c, i: (bb, jnp.maximum(i * rpb - 1, 0), c + col_block0)),
        pl.BlockSpec((1, tl, cw), lambda bb, c, i: (bb, i, c + col_block0)),
        pl.BlockSpec((1, HALO, cw), lambda bb, c, i: (bb, jnp.minimum((i + 1) * rpb, last), c + col_block0)),
    ]


def _hconv_kernel(p0, m0, n0, p1, m1, n1, p2, m2, n2, w0, b0, w1, b1, w2, b2, x0_ref, u_ref):
    x0_ref[0] = _conv_taps(p0, m0, n0, w0, b0, H_SHORT).astype(x0_ref.dtype)
    x1 = _conv_taps(p1, m1, n1, w1, b1, H_SHORT)
    v = _conv_taps(p2, m2, n2, w2, b2, H_SHORT)
    u_ref[0] = (x1 * v).astype(u_ref.dtype)


def _hconv(p, col0, w, bias, d):
    b, l, _ = p.shape
    tl = min(l, 512)
    cw = min(d, COL_TILE)
    nb = d // cw
    specs, wspecs = [], []
    for s in range(3):
        specs += _conv_specs(tl, cw, l, col0 // cw + s * nb)
        wspecs += [pl.BlockSpec((H_SHORT, cw), functools.partial(lambda bb, c, i, s: (0, c + s * nb), s=s)),
                   pl.BlockSpec((1, cw), functools.partial(lambda bb, c, i, s: (0, c + s * nb), s=s))]
    bias2 = bias.reshape(1, 3 * d)
    out_spec = pl.BlockSpec((1, tl, cw), lambda bb, c, i: (bb, i, c))
    return pl.pallas_call(
        _hconv_kernel,
        grid=(b, nb, l // tl),
        in_specs=specs + wspecs,
        out_specs=[out_spec, out_spec],
        out_shape=[jax.ShapeDtypeStruct((b, l, d), BF16)] * 2,
        compiler_params=_cparams("parallel", "parallel", "arbitrary"),
        name="hconv",
    )(p, p, p, p, p, p, p, p, p, w, bias2, w, bias2, w, bias2)


def _ssd_kernel(*refs, reverse, with_output, epilogue, heads):
    if epilogue:
        (xbc_ref, dt_ref, dtb_ref, alog_ref, init_ref, sel_ref, yf_ref, z_ref, dsk_ref, ng_ref,
         y_ref, fin_ref, s_scr) = refs
    elif with_output:
        xbc_ref, dt_ref, dtb_ref, alog_ref, init_ref, sel_ref, y_ref, fin_ref, s_scr = refs
    else:
        xbc_ref, dt_ref, dtb_ref, alog_ref, init_ref, fin_ref, s_scr = refs
    t = M_CHUNK
    n = M_STATE
    assert n == t
    di = heads * M_HEADDIM
    gw = di // M_GROUPS
    rpg = heads // M_GROUPS
    lane0 = heads if reverse else 0
    edge = 0 if reverse else t - 1
    c = pl.program_id(1)

    nb = xbc_ref.shape[0]

    @pl.when(c == 0)
    def _():
        s_scr[...] = init_ref[...]

    row = lax.broadcasted_iota(jnp.int32, (t, t), 0)
    col = lax.broadcasted_iota(jnp.int32, (t, t), 1)
    mask = (row <= col) if reverse else (row >= col)
    lane = lax.broadcasted_iota(jnp.int32, (1, LANES), 1)
    lo = lane < M_HEADDIM
    mlo = jnp.where(lo, 1.0, 0.0).astype(BF16)
    mhi = jnp.where(lo, 0.0, 1.0).astype(BF16)

    def chunk_decays(bi):
        x = dt_ref[bi] + dtb_ref[...]
        dt = jnp.maximum(x, 0.0) + jnp.log1p(jnp.exp(-jnp.abs(x)))
        dta = dt * (-jnp.exp(alog_ref[...]))
        acs = jnp.dot(mask.astype(F32), dta, preferred_element_type=F32, precision=HIGHEST)
        acs_t = acs.T
        dt_t = dt.T
        acs2 = acs * LOG2_E
        arow_dt2 = (acs_t - jnp.log(dt_t)) * LOG2_E
        w_t = jnp.exp(acs_t[:, edge:edge + 1] - acs_t) * dt_t
        dec = jnp.exp(acs[edge:edge + 1, :])
        eacs = jnp.exp(acs).astype(BF16) if with_output else None
        return acs2, arow_dt2, w_t, dec, eacs

    def group(bi, g, acs2, arow_dt2, w_t, dec, eacs):
        bt = xbc_ref[bi, :, di + g * n:di + (g + 1) * n].astype(F32).T
        if with_output:
            cm = xbc_ref[bi, :, di + M_GROUPS * n + g * n:di + M_GROUPS * n + (g + 1) * n]
            cb = jnp.dot(cm, bt.astype(BF16), preferred_element_type=F32)
            y_off = jnp.dot(cm, s_scr[bi, g].astype(BF16), preferred_element_type=F32)
        y_parts = []
        for q in range(rpg // 2):
            c0 = g * gw + q * LANES
            xp = xbc_ref[bi, :, c0:c0 + LANES]
            sp = s_scr[bi, g, :, q * LANES:(q + 1) * LANES]
            x2 = jnp.concatenate([xp * mlo, xp * mhi], axis=0)
            hl0 = lane0 + g * rpg + 2 * q
            lhs, btw, decs = [], [], []
            for r in range(2):
                hl = hl0 + r
                if with_output:
                    acol = jnp.broadcast_to(acs2[:, hl:hl + 1], (t, t))
                    dk = jnp.exp2(jnp.where(mask, acol - arow_dt2[hl:hl + 1, :], -jnp.inf))
                    lhs.append((cb * dk).astype(BF16))
                btw.append((bt * w_t[hl:hl + 1, :]).astype(BF16))
                decs.append(jnp.broadcast_to(dec[:, hl:hl + 1], (1, LANES)))
            if with_output:
                e_pair = jnp.dot(eacs, sel_ref[hl0 // 2], preferred_element_type=F32)
                y_diag = jnp.dot(jnp.concatenate(lhs, axis=1), x2, preferred_element_type=F32)
                y_parts.append(y_diag + e_pair * y_off[:, q * LANES:(q + 1) * LANES])
            ds = jnp.dot(jnp.concatenate(btw, axis=1), x2, preferred_element_type=F32)
            s_scr[bi, g, :, q * LANES:(q + 1) * LANES] = jnp.where(lo, decs[0], decs[1]) * sp + ds
        if with_output:
            yg = jnp.concatenate(y_parts, axis=1) if len(y_parts) > 1 else y_parts[0]
            sl = slice(g * gw, (g + 1) * gw)
            if epilogue:
                yg = yg + yf_ref[bi, :, sl].astype(F32)
                v = (yg + xbc_ref[bi, :, sl].astype(F32) * dsk_ref[:, sl]) * _silu(z_ref[bi, :, sl].astype(F32))
                ms = jnp.mean(v * v, axis=-1, keepdims=True)
                y_ref[bi, :, sl] = (v * lax.rsqrt(ms + RMS_EPS) * ng_ref[:, sl]).astype(y_ref.dtype)
            else:
                y_ref[bi, :, sl] = yg.astype(y_ref.dtype)

    decays = [chunk_decays(bi) for bi in range(nb)]
    for g in range(M_GROUPS):
        for bi in range(nb):
            group(bi, g, *decays[bi])

    @pl.when(c == pl.num_programs(1) - 1)
    def _():
        fin_ref[...] = s_scr[...]


def _pair_selectors():
    p = jnp.arange(LANES // 2)[:, None, None]
    h = jnp.arange(LANES)[None, :, None]
    lane = jnp.arange(LANES)[None, None, :]
    return (h == 2 * p + (lane >= M_HEADDIM)).astype(BF16)


def _ssd(xbc, dtp, dtb, alog, init, heads, reverse, with_output, extra=None):
    b, l, xw = xbc.shape
    t = M_CHUNK
    nc = l // t
    di = heads * M_HEADDIM
    gw = di // M_GROUPS
    epilogue = extra is not None
    cidx = (lambda c: nc - 1 - c) if reverse else (lambda c: c)
    nb = SSD_BATCH_ROWS if b % SSD_BATCH_ROWS == 0 else 1
    state_spec = pl.BlockSpec((nb, M_GROUPS, M_STATE, gw), lambda bb, c: (bb, 0, 0, 0))
    in_specs = [pl.BlockSpec((nb, t, xw), lambda bb, c: (bb, cidx(c), 0)),
                pl.BlockSpec((nb, t, LANES), lambda bb, c: (bb, cidx(c), 0)),
                pl.BlockSpec((1, LANES), lambda bb, c: (0, 0)),
                pl.BlockSpec((1, LANES), lambda bb, c: (0, 0)),
                state_spec]
    args = [xbc, dtp, dtb, alog, init]
    out_specs, out_shape = [], []
    if with_output:
        in_specs.append(pl.BlockSpec((LANES // 2, LANES, LANES), lambda bb, c: (0, 0, 0)))
        args.append(_pair_selectors())
    if epilogue:
        yf, p, z_col0, dsk, ng = extra
        in_specs += [pl.BlockSpec((nb, t, di), lambda bb, c: (bb, cidx(c), 0)),
                     pl.BlockSpec((nb, t, di), lambda bb, c: (bb, cidx(c), z_col0 // di)),
                     pl.BlockSpec((1, di), lambda bb, c: (0, 0)),
                     pl.BlockSpec((1, di), lambda bb, c: (0, 0))]
        args += [yf, p, dsk, ng]
    if with_output:
        out_specs.append(pl.BlockSpec((nb, t, di), lambda bb, c: (bb, cidx(c), 0)))
        out_shape.append(jax.ShapeDtypeStruct((b, l, di), BF16))
    out_specs.append(state_spec)
    out_shape.append(jax.ShapeDtypeStruct((b, M_GROUPS, M_STATE, gw), F32))
    outs = pl.pallas_call(
        functools.partial(_ssd_kernel, reverse=reverse, with_output=with_output, epilogue=epilogue, heads=heads),
        grid=(b // nb, nc),
        in_specs=in_specs,
        out_specs=out_specs,
        out_shape=out_shape,
        scratch_shapes=[pltpu.VMEM((nb, M_GROUPS, M_STATE, gw), F32)],
        compiler_params=_cparams("parallel", "arbitrary"),
        name="ssd_bwd" if reverse else "ssd_fwd",
    )(*args)
    return (outs[0], outs[1]) if with_output else (None, outs[0])


def _filt_kernel(f_ref, w1_ref, b1_ref, w2_ref, b2_ref, fr_ref, w3_ref, dl_ref, o_ref, ss_ref):
    f = f_ref[...]
    fr = fr_ref[...]
    h = jnp.sin(fr * (jnp.dot(f, w1_ref[...], preferred_element_type=F32, precision=HIGHEST) + b1_ref[...]))
    h = jnp.sin(fr * (jnp.dot(h, w2_ref[...], preferred_element_type=F32, precision=HIGHEST) + b2_ref[...]))
    filt = jnp.dot(h, w3_ref[...], preferred_element_type=F32, precision=HIGHEST)
    tcol = f[:, H_EMB:H_EMB + 1]
    mcol = f[:, H_EMB + 1:H_EMB + 2]
    out = filt * jnp.exp(-tcol * dl_ref[...]) * mcol
    o_ref[...] = out

    @pl.when(pl.program_id(1) == 0)
    def _():
        ss_ref[...] = jnp.zeros_like(ss_ref)

    ss_ref[...] += jnp.sum(out * out, axis=0, keepdims=True)


def _filter_features(l):
    t = jnp.linspace(0.0, 1.0, l, dtype=F32)[:, None]
    ang = (2.0 * math.pi / l) * jnp.arange(l, dtype=F32)[:, None]
    nb = (H_EMB - 1) // 2
    bands = jnp.linspace(1e-4, nb - 1, nb, dtype=F32)[None, :]
    feats = jnp.concatenate([t, jnp.cos(bands * ang), -jnp.sin(bands * ang), t, jnp.ones_like(t)], axis=-1)
    rev = jnp.concatenate([feats[0:1] * 0.0, jnp.flip(feats[1:], axis=0)], axis=0)
    full = jnp.concatenate([feats, rev], axis=0)
    return jnp.pad(full, ((0, 0), (0, FEAT_COLS - full.shape[1])))


def _hyena_filter(l, w1, b1, w2, b2, freq, w3):
    hid = w2.shape[0]
    d = w3.shape[1] // 2
    feats = _filter_features(l)
    w1p = jnp.pad(w1, ((0, FEAT_COLS - w1.shape[0]), (0, 0)))
    deltas = jnp.abs(jnp.linspace(math.log(H_DECAY_TARGET) / H_SLOW_PCT, math.log(H_DECAY_TARGET) / H_FAST_PCT,
                                  d, dtype=F32))[None, :]
    tl = min(l, 512)
    cw = d
    nb = d // cw
    nrow = l // tl
    small = lambda shape: pl.BlockSpec(shape, lambda c, i: (0, 0))
    return pl.pallas_call(
        _filt_kernel,
        grid=(nb, 2 * nrow),
        in_specs=[pl.BlockSpec((tl, FEAT_COLS), lambda c, i: (i, 0)),
                  small((FEAT_COLS, hid)), small((1, hid)), small((hid, hid)), small((1, hid)), small((1, hid)),
                  pl.BlockSpec((hid, cw), lambda c, i: (0, c + (i // nrow) * nb)),
                  pl.BlockSpec((1, cw), lambda c, i: (0, c))],
        out_specs=[pl.BlockSpec((tl, cw), lambda c, i: (i, c)),
                   pl.BlockSpec((1, cw), lambda c, i: (0, c))],
        out_shape=[jax.ShapeDtypeStruct((2 * l, d), F32), jax.ShapeDtypeStruct((1, d), F32)],
        compiler_params=_cparams("parallel", "arbitrary"),
        name="hyena_filter",
    )(feats, w1p, b1.reshape(1, hid), w2, b2.reshape(1, hid), freq.reshape(1, hid), w3, deltas)


def _fft_split(n):
    n1 = 1 << ((n.bit_length() - 1 + 1) // 2)
    return n1, n // n1


def _dft_consts(n1, n2):
    n = n1 * n2
    h = n1 // 2
    k1 = jnp.arange(n1, dtype=jnp.int32)
    t1 = jnp.arange(n1, dtype=jnp.int32)
    t2 = jnp.arange(n2, dtype=jnp.int32)
    idx = (k1[None, :, None] * (n2 * t1[None, None, :] + t2[:, None, None])) % n
    ang = idx.astype(F32) * (2.0 * math.pi / n)
    cs, sn = jnp.cos(ang), jnp.sin(ang)
    ch, sh = cs[..., :h], sn[..., :h]
    t_data = jnp.concatenate([jnp.concatenate([ch, sh], -1), jnp.concatenate([-sh, ch], -1)], axis=1)
    t_filt = jnp.concatenate([cs, -sn], axis=1)
    ct, st = jnp.swapaxes(ch, 1, 2), jnp.swapaxes(sh, 1, 2)
    t_inv = jnp.concatenate([jnp.concatenate([ct, -st], -1), jnp.concatenate([st, ct], -1)], axis=1) / n
    k2 = jnp.arange(n2, dtype=jnp.int32)
    ang2 = ((k2[:, None] * k2[None, :]) % n2).astype(F32) * (2.0 * math.pi / n2)
    c2, s2 = jnp.cos(ang2), jnp.sin(ang2)
    m_fwd = jnp.concatenate([jnp.concatenate([c2, s2], -1), jnp.concatenate([-s2, c2], -1)], axis=0)
    m_inv = jnp.concatenate([jnp.concatenate([c2, -s2], -1), jnp.concatenate([s2, c2], -1)], axis=0)
    return dict(t_data=t_data.astype(BF16), t_filt=t_filt.astype(BF16), t_inv=t_inv.astype(BF16),
                m_fwd=m_fwd.astype(BF16), m_inv=m_inv.astype(BF16))


def _fft1_kernel(x_ref, t_ref, o_ref):
    a = x_ref.shape[1]
    xs = [pltpu.einshape("rjc->jrc", x_ref[0, s].astype(F32)) for s in range(a)]
    n1 = o_ref.shape[2]
    for j in range(SUBLANES):
        x = jnp.concatenate([xs[s][j] for s in range(a)], axis=0).astype(BF16)
        o = jnp.dot(t_ref[j], x, preferred_element_type=F32)
        o_ref[0, j] = _pack_complex(o[:n1], o[n1:])


def _pack_complex(re, im):
    r = lax.bitcast_convert_type(re.astype(BF16).astype(F32), jnp.uint32)
    i = lax.bitcast_convert_type(im.astype(BF16).astype(F32), jnp.uint32)
    return (r >> 16) | i


def _unpack_complex(w):
    re = lax.bitcast_convert_type(w << 16, F32)
    im = lax.bitcast_convert_type(w & jnp.uint32(0xFFFF0000), F32)
    return re, im


def _fft_stage1(xv, tmat, n1, n2, d):
    p, a, r, _, _ = xv.shape
    cw = min(d, COL_TILE)
    return pl.pallas_call(
        _fft1_kernel,
        grid=(n2 // SUBLANES, d // cw, p),
        in_specs=[pl.BlockSpec((1, a, r, SUBLANES, cw), lambda j, c, pp: (pp, 0, 0, j, c)),
                  pl.BlockSpec((SUBLANES, 2 * n1, n1), lambda j, c, pp: (j, 0, 0))],
        out_specs=pl.BlockSpec((1, SUBLANES, n1, cw), lambda j, c, pp: (pp, j, 0, c)),
        out_shape=jax.ShapeDtypeStruct((p, n2, n1, d), jnp.uint32),
        compiler_params=_cparams("parallel", "parallel", "parallel"),
        name="fft_stage1",
    )(xv, tmat)


def _fft2_kernel(a_ref, af_ref, ss_ref, mf_ref, mi_ref, o_ref, h_scr):
    n2 = a_ref.shape[1]

    def stage(w):
        re, im = _unpack_complex(w)
        return jnp.dot(mf_ref[...], jnp.concatenate([re, im], axis=0).astype(BF16), preferred_element_type=F32)

    @pl.when(pl.program_id(2) == 0)
    def _():
        af = pltpu.einshape("tkc->ktc", af_ref[0])
        for k in range(SUBLANES):
            h_scr[k] = stage(af[k]) * lax.rsqrt(ss_ref[...] + RMS_EPS)

    cw = a_ref.shape[3]
    half = max(cw // 2, LANES)
    for c0 in range(0, cw, half):
        a = pltpu.einshape("tkc->ktc", a_ref[0, :, :, c0:c0 + half])
        outs = []
        for k in range(SUBLANES):
            xk = stage(a[k])
            xr, xi = xk[:n2], xk[n2:]
            hr, hi = h_scr[k, 0:n2, c0:c0 + half], h_scr[k, n2:, c0:c0 + half]
            z = jnp.concatenate([xr * hr - xi * hi, xr * hi + xi * hr], axis=0).astype(BF16)
            bk = jnp.dot(mi_ref[...], z, preferred_element_type=F32)
            outs.append(_pack_complex(bk[:n2], bk[n2:]))
        o_ref[0, :, :, c0:c0 + half] = pltpu.einshape("ktc->tkc", jnp.stack(outs, axis=0))


def _fft_stage2(av, afv, ss, m_fwd, m_inv, n1, n2, d):
    p = av.shape[0]
    cw = min(d, COL_TILE)
    kb = SUBLANES
    return pl.pallas_call(
        _fft2_kernel,
        grid=(n1 // kb, d // cw, p),
        in_specs=[pl.BlockSpec((1, n2, kb, cw), lambda k, c, pp: (pp, 0, k, c)),
                  pl.BlockSpec((1, n2, kb, cw), lambda k, c, pp: (0, 0, k, c)),
                  pl.BlockSpec((1, cw), lambda k, c, pp: (0, c)),
                  pl.BlockSpec((2 * n2, 2 * n2), lambda k, c, pp: (0, 0)),
                  pl.BlockSpec((2 * n2, 2 * n2), lambda k, c, pp: (0, 0))],
        out_specs=pl.BlockSpec((1, n2, kb, cw), lambda k, c, pp: (pp, 0, k, c)),
        out_shape=jax.ShapeDtypeStruct(av.shape, jnp.uint32),
        scratch_shapes=[pltpu.VMEM((kb, 2 * n2, cw), F32)],
        compiler_params=_cparams("parallel", "parallel", "arbitrary"),
        name="fft_stage2",
    )(av, afv, ss, m_fwd, m_inv)


def _fft3_kernel(b_ref, t_ref, u_ref, x0_ref, bias_ref, o_ref):
    n1 = b_ref.shape[2]
    h = n1 // 2
    ys = []
    for j in range(SUBLANES):
        re, im = _unpack_complex(b_ref[0, j])
        bv = jnp.concatenate([re, im], axis=0).astype(BF16)
        ys.append(jnp.dot(t_ref[j], bv, preferred_element_type=F32))
    y = pltpu.einshape("jtc->tjc", jnp.stack(ys, axis=0))
    for s in range(2):
        conv = y[s * h:(s + 1) * h] + u_ref[0, s].astype(F32) * bias_ref[...]
        o_ref[0, s] = (x0_ref[0, s].astype(F32) * conv).astype(o_ref.dtype)


def _fft_stage3(bv, t_inv, uv, x0v, bias, n1, n2, d):
    p = bv.shape[0]
    h = n1 // 2
    cw = min(d, COL_TILE)
    io_spec = pl.BlockSpec((1, 2, h, SUBLANES, cw), lambda j, c, pp: (pp, 0, 0, j, c))
    return pl.pallas_call(
        _fft3_kernel,
        grid=(n2 // SUBLANES, d // cw, p),
        in_specs=[pl.BlockSpec((1, SUBLANES, n1, cw), lambda j, c, pp: (pp, j, 0, c)),
                  pl.BlockSpec((SUBLANES, n1, 2 * n1), lambda j, c, pp: (j, 0, 0)),
                  io_spec, io_spec,
                  pl.BlockSpec((1, cw), lambda j, c, pp: (0, c))],
        out_specs=io_spec,
        out_shape=jax.ShapeDtypeStruct(uv.shape, BF16),
        compiler_params=_cparams("parallel", "parallel", "parallel"),
        name="fft_stage3",
    )(bv, t_inv, uv, x0v, bias)


def _filter_spectrum(l, consts, lp):
    n1, n2 = _fft_split(2 * l)
    d = lp['h_bias'].shape[0]
    full, ss = _hyena_filter(l, lp['hf_w1'], lp['hf_b1'], lp['hf_w2'], lp['hf_b2'], lp['hf_freq'], lp['hf_w3'])
    return _fft_stage1(full.reshape(1, 1, n1, n2, d), consts['t_filt'], n1, n2, d), ss


def _long_conv(u, x0, hf, consts, bias):
    b, l, d = u.shape
    n1, n2 = _fft_split(2 * l)
    p = b // 2
    h = n1 // 2
    uv = u.reshape(p, 2, h, n2, d)
    a = _fft_stage1(uv, consts['t_data'], n1, n2, d)
    bv = _fft_stage2(a, hf[0], hf[1], consts['m_fwd'], consts['m_inv'], n1, n2, d)
    y = _fft_stage3(bv, consts['t_inv'], uv, x0.reshape(p, 2, h, n2, d), bias.reshape(1, d), n1, n2, d)
    return y.reshape(b, l, d)


def _merge_kernel(ym_ref, yh_ref, ga_ref, gb_ref, wm_ref, wh_ref, o_ref):
    a = jnp.dot(ym_ref[0], wm_ref[...], preferred_element_type=F32)
    bb = jnp.dot(yh_ref[0], wh_ref[...], preferred_element_type=F32)
    ga = jax.nn.sigmoid(ga_ref[0].astype(F32))
    gb = jax.nn.sigmoid(gb_ref[0].astype(F32))
    o_ref[0] = (ga * a + gb * bb).astype(o_ref.dtype)


def _merge(ym, yh, p, gate_col0, wm, wh):
    b, l, di = ym.shape
    d = yh.shape[2]
    tm = min(l, 1024)
    tn = min(d, COL_TILE)
    g0 = gate_col0 // tn
    nb = d // tn
    return pl.pallas_call(
        _merge_kernel,
        grid=(b, l // tm, nb),
        in_specs=[pl.BlockSpec((1, tm, di), lambda bb, i, j: (bb, i, 0)),
                  pl.BlockSpec((1, tm, d), lambda bb, i, j: (bb, i, 0)),
                  pl.BlockSpec((1, tm, tn), lambda bb, i, j: (bb, i, g0 + j)),
                  pl.BlockSpec((1, tm, tn), lambda bb, i, j: (bb, i, g0 + nb + j)),
                  pl.BlockSpec((di, tn), lambda bb, i, j: (0, j)),
                  pl.BlockSpec((d, tn), lambda bb, i, j: (0, j))],
        out_specs=pl.BlockSpec((1, tm, tn), lambda bb, i, j: (bb, i, j)),
        out_shape=jax.ShapeDtypeStruct((b, l, d), BF16),
        compiler_params=_cparams("parallel", "parallel", "arbitrary"),
        name="merge",
    )(ym, yh, p, p, wm, wh)


def _resid_mm_kernel(a_ref, w_ref, x_ref, g_ref, o_ref):
    o_ref[0] = x_ref[0] + g_ref[0] * jnp.dot(a_ref[0], w_ref[...], preferred_element_type=F32)


def _resid_mm(a, w, x, gate):
    b, l, k = a.shape
    d = w.shape[1]
    tm = min(l, 1024)
    tn = min(d, COL_TILE)
    return pl.pallas_call(
        _resid_mm_kernel,
        grid=(b, l // tm, d // tn),
        in_specs=[pl.BlockSpec((1, tm, k), lambda bb, i, j: (bb, i, 0)),
                  pl.BlockSpec((k, tn), lambda bb, i, j: (0, j)),
                  pl.BlockSpec((1, tm, tn), lambda bb, i, j: (bb, i, j)),
                  pl.BlockSpec((1, 1, tn), lambda bb, i, j: (bb, 0, j))],
        out_specs=pl.BlockSpec((1, tm, tn), lambda bb, i, j: (bb, i, j)),
        out_shape=jax.ShapeDtypeStruct((b, l, d), F32),
        compiler_params=_cparams("parallel", "parallel", "arbitrary"),
        name="resid_mm",
    )(a, w, x, gate)


def _ffn_kernel(x_ref, g_ref, sh_ref, sc_ref, gt_ref, wg_ref, wu_ref, wd_ref, fg_ref, o_ref, h_scr, *, final_norm):
    j = pl.program_id(2)

    @pl.when(j == 0)
    def _():
        h_scr[...] = _modnorm(x_ref[0], g_ref[...], sh_ref[0], sc_ref[0]).astype(BF16)
        o_ref[...] = jnp.zeros_like(o_ref)

    tm = h_scr.shape[0]
    rows_per_pass = min(tm, 512)
    for r0 in range(0, tm, rows_per_pass):
        h = h_scr[r0:r0 + rows_per_pass, :]
        gg = jnp.dot(h, wg_ref[...], preferred_element_type=F32)
        uu = jnp.dot(h, wu_ref[...], preferred_element_type=F32)
        o_ref[0, r0:r0 + rows_per_pass, :] += jnp.dot((_silu(gg) * uu).astype(BF16), wd_ref[...],
                                                      preferred_element_type=F32)

    @pl.when(j == pl.num_programs(2) - 1)
    def _():
        y = x_ref[0] + gt_ref[0] * o_ref[0]
        if final_norm:
            ms = jnp.mean(y * y, axis=-1, keepdims=True)
            y = y * lax.rsqrt(ms + RMS_EPS) * fg_ref[...]
        o_ref[0] = y


def _ffn(x, g, shift, scale, gate, w_gu, w_down, final_g=None):
    b, l, d = x.shape
    f = w_down.shape[0]
    tm = min(l, 1024)
    fg = jnp.ones((1, d), F32) if final_g is None else final_g.reshape(1, d)
    tf = 256
    nf = f // tf
    vec = pl.BlockSpec((1, 1, d), lambda bb, i, j: (bb, 0, 0))
    return pl.pallas_call(
        functools.partial(_ffn_kernel, final_norm=final_g is not None),
        grid=(b, l // tm, nf),
        in_specs=[pl.BlockSpec((1, tm, d), lambda bb, i, j: (bb, i, 0)),
                  pl.BlockSpec((1, d), lambda bb, i, j: (0, 0)),
                  vec, vec, vec,
                  pl.BlockSpec((d, tf), lambda bb, i, j: (0, j)),
                  pl.BlockSpec((d, tf), lambda bb, i, j: (0, j + nf)),
                  pl.BlockSpec((tf, d), lambda bb, i, j: (j, 0)),
                  pl.BlockSpec((1, d), lambda bb, i, j: (0, 0))],
        out_specs=pl.BlockSpec((1, tm, d), lambda bb, i, j: (bb, i, 0)),
        out_shape=jax.ShapeDtypeStruct((b, l, d), F32),
        scratch_shapes=[pltpu.VMEM((tm, d), BF16)],
        compiler_params=_cparams("parallel", "parallel", "arbitrary"),
        name="ffn",
    )(x, g.reshape(1, d), shift, scale, gate, w_gu, w_gu, w_down, fg)


def _grid_pos_embed(rows, d):
    r, col = jnp.meshgrid(jnp.arange(rows), jnp.arange(GRID_W), indexing='ij')
    quarter = d // 4
    omega = 1.0 / (POS_BASE ** (jnp.arange(quarter, dtype=F32) / quarter))

    def axis_embed(pos):
        ang = pos.reshape(-1)[:, None].astype(F32) * omega[None, :]
        return jnp.concatenate([jnp.sin(ang), jnp.cos(ang)], axis=-1)

    return jnp.concatenate([axis_embed(r), axis_embed(col)], axis=-1).astype(F32)


class _Cols:
    def __init__(self, d):
        self.d = d
        self.di = 2 * d
        self.heads = self.di // M_HEADDIM
        self.xbc = self.di + 2 * M_GROUPS * M_STATE
        self.z0 = 0
        self.gate0 = self.z0 + self.di
        self.hy0 = self.gate0 + 2 * d
        self.xbc0 = self.hy0 + 3 * d
        self.dt0 = self.xbc0 + self.xbc
        self.total = self.dt0 + COL_TILE
        assert 2 * self.heads <= LANES and self.heads % (2 * M_GROUPS) == 0
        for off in (self.gate0, self.hy0, self.xbc0, self.dt0):
            assert off % COL_TILE == 0


def _pack_w_in(w_in, cols):
    o_dt = cols.xbc
    o_z = o_dt + 2 * cols.heads
    o_hy = o_z + cols.di
    o_gate = o_hy + 3 * cols.d
    parts = [w_in[:, o_z:o_hy], w_in[:, o_gate:], w_in[:, o_hy:o_gate], w_in[:, :o_dt], w_in[:, o_dt:o_z]]
    w = jnp.concatenate(parts, axis=1)
    return jnp.pad(w, ((0, 0), (0, cols.total - w.shape[1]))).astype(BF16)


def _pad_lanes(v):
    v = v.reshape(1, -1)
    return jnp.pad(v, ((0, 0), (0, LANES - v.shape[1])))


def _mix_sublayer(x, mods, lp, cols, hf, consts, init_f, init_b):
    h = _prenorm(x, lp['norm1_g'], mods[0], mods[1])
    p = _proj_plain(h, lp['w_in'], cols.xbc0)
    _, xbc, dtp = _inproj(h, lp['w_in'], cols.xbc0, 0, lp['m_conv_w'], lp['m_conv_b'])
    y_f, fin_f = _ssd(xbc, dtp, lp['dtb'], lp['alog'], init_f, cols.heads, False, True)
    ym, fin_b = _ssd(xbc, dtp, lp['dtb'], lp['alog'], init_b, cols.heads, True, True,
                     extra=(y_f, p, cols.z0, lp['dskip'], lp['m_norm_g']))
    x0, u = _hconv(p, cols.hy0, lp['h_conv_w'], lp['h_conv_b'], cols.d)
    yh = _long_conv(u, x0, hf, consts, lp['h_bias'])
    merged = _merge(ym, yh, p, cols.gate0, lp['m_w_out'], lp['h_w_out'])
    return _resid_mm(merged, lp['w_merge_out'], x, mods[2]), fin_f, fin_b


def _ctx_final_states(x, mods, lp, cols, init_f, init_b):
    h = _prenorm(x, lp['norm1_g'], mods[0], mods[1])
    _, xbc, dtp = _inproj(h, lp['w_in'], cols.xbc0, 0, lp['m_conv_w'], lp['m_conv_b'])
    _, fin_f = _ssd(xbc, dtp, lp['dtb'], lp['alog'], init_f, cols.heads, False, False)
    _, fin_b = _ssd(xbc, dtp, lp['dtb'], lp['alog'], init_b, cols.heads, True, False)
    return fin_f, fin_b


def kernel(x, c, ctx, c_ctx, ada_w, ada_b, norm1_g, w_in, m_conv_w, m_conv_b, m_dt_bias, m_a_log, m_d, m_norm_g,
           m_w_out, h_conv_w, h_conv_b, hf_w1, hf_b1, hf_w2, hf_b2, hf_freq, hf_w3, h_bias, h_w_out, w_merge_out,
           norm2_g, ffn_w_gu, ffn_w_down, final_g):
    b, l, d = x.shape
    lc = ctx.shape[1]
    depth = ada_w.shape[0]
    cols = _Cols(d)
    assert b % 2 == 0 and b + 1 <= SUBLANES and l % M_CHUNK == 0 and lc % M_CHUNK == 0

    cvecs = jnp.concatenate([c, c_ctx[None], jnp.zeros((SUBLANES - b - 1, d), F32)], axis=0)
    mods_all = _mods(cvecs, ada_w, ada_b)
    consts_l = _dft_consts(*_fft_split(2 * l))
    consts_c = _dft_consts(*_fft_split(2 * lc))

    x_l = _add_pos(x, _grid_pos_embed(l // GRID_W, d))
    x_c = ctx
    zero_state = jnp.zeros((b, M_GROUPS, M_STATE, cols.di // M_GROUPS), F32)
    for i in range(depth):
        lp = dict(norm1_g=norm1_g[i], w_in=_pack_w_in(w_in[i], cols), m_conv_w=m_conv_w[i], m_conv_b=m_conv_b[i],
                  dtb=_pad_lanes(m_dt_bias[i]), alog=_pad_lanes(m_a_log[i]),
                  dskip=jnp.repeat(m_d[i], M_HEADDIM).reshape(1, cols.di), m_norm_g=m_norm_g[i].reshape(1, cols.di),
                  m_w_out=m_w_out[i].astype(BF16), h_conv_w=h_conv_w[i], h_conv_b=h_conv_b[i],
                  hf_w1=hf_w1[i], hf_b1=hf_b1[i], hf_w2=hf_w2[i], hf_b2=hf_b2[i], hf_freq=hf_freq[i], hf_w3=hf_w3[i],
                  h_bias=h_bias[i], h_w_out=h_w_out[i].astype(BF16), w_merge_out=w_merge_out[i].astype(BF16),
                  norm2_g=norm2_g[i], ffn_w_gu=ffn_w_gu[i].astype(BF16), ffn_w_down=ffn_w_down[i].astype(BF16))
        m = mods_all[i].reshape(SUBLANES, ADA_CHUNKS, d)
        mods_l = [m[:b, k][:, None, :] for k in range(ADA_CHUNKS)]
        mods_c = [jnp.broadcast_to(m[b, k][None, None, :], (b, 1, d)) for k in range(ADA_CHUNKS)]
        if i < depth - 1:
            hf_c = _filter_spectrum(lc, consts_c, lp)
            x_c_mixed, fin_f, fin_b = _mix_sublayer(x_c, mods_c, lp, cols, hf_c, consts_c, zero_state, zero_state)
        else:
            fin_f, fin_b = _ctx_final_states(x_c, mods_c, lp, cols, zero_state, zero_state)
        hf_l = _filter_spectrum(l, consts_l, lp)
        x_l, _, _ = _mix_sublayer(x_l, mods_l, lp, cols, hf_l, consts_l, fin_f, fin_b)
        x_l = _ffn(x_l, lp['norm2_g'], mods_l[3], mods_l[4], mods_l[5], lp['ffn_w_gu'], lp['ffn_w_down'],
                   final_g=final_g if i == depth - 1 else None)
        if i < depth - 1:
            x_c = _ffn(x_c_mixed, lp['norm2_g'], mods_c[3], mods_c[4], mods_c[5], lp['ffn_w_gu'], lp['ffn_w_down'])
    return x_l
```

```python
import functools
import math

import jax
import jax.numpy as jnp
from jax import lax
from jax.experimental import pallas as pl
from jax.experimental.pallas import tpu as pltpu

F32 = jnp.float32
BF16 = jnp.bfloat16
HIGHEST = lax.Precision.HIGHEST

RMS_EPS = 1e-6
LOG2_E = 1.4426950408889634
GRID_W = 64
POS_BASE = 10000.0
ADA_CHUNKS = 6
M_HEADDIM = 64
M_GROUPS = 8
M_STATE = 128
M_CONV = 5
M_CHUNK = 128
H_SHORT = 3
H_EMB = 33
H_DECAY_TARGET = 1e-2
H_FAST_PCT = 0.3
H_SLOW_PCT = 1.5

LANES = 128
SUBLANES = 8
HALO = 16
VMEM_LIMIT_BYTES = 56 * 1024 * 1024

COL_TILE = 512
FEAT_COLS = 40
SSD_BATCH_ROWS = 2


def _cparams(*sem):
    return pltpu.CompilerParams(dimension_semantics=sem, vmem_limit_bytes=VMEM_LIMIT_BYTES)


def _silu_of_twice(h):
    return h + h * jnp.tanh(h)


def _silu(v):
    return _silu_of_twice(0.5 * v)


def _round_up(a, m):
    return -(-a // m) * m


def _mods_kernel(c_ref, w_ref, b_ref, o_ref):
    s = _silu(c_ref[...])
    o_ref[0] = jnp.dot(s, w_ref[0], preferred_element_type=F32, precision=HIGHEST) + b_ref[0]


def _mods(cvecs, ada_w, ada_b):
    depth, d, n = ada_w.shape
    tn = min(n, 1024)
    return pl.pallas_call(
        _mods_kernel,
        grid=(depth, n // tn),
        in_specs=[pl.BlockSpec((SUBLANES, d), lambda i, j: (0, 0)),
                  pl.BlockSpec((1, d, tn), lambda i, j: (i, 0, j)),
                  pl.BlockSpec((1, 1, tn), lambda i, j: (i, 0, j))],
        out_specs=pl.BlockSpec((1, SUBLANES, tn), lambda i, j: (i, 0, j)),
        out_shape=jax.ShapeDtypeStruct((depth, SUBLANES, n), F32),
        compiler_params=_cparams("parallel", "parallel"),
        name="mods",
    )(cvecs, ada_w, ada_b.reshape(depth, 1, n))


def _add_kernel(x_ref, p_ref, o_ref):
    o_ref[0] = x_ref[0] + p_ref[...]


def _add_pos(x, pos):
    b, l, d = x.shape
    tl = min(l, 1024)
    return pl.pallas_call(
        _add_kernel,
        grid=(l // tl, b),
        in_specs=[pl.BlockSpec((1, tl, d), lambda i, bb: (bb, i, 0)),
                  pl.BlockSpec((tl, d), lambda i, bb: (i, 0))],
        out_specs=pl.BlockSpec((1, tl, d), lambda i, bb: (bb, i, 0)),
        out_shape=jax.ShapeDtypeStruct(x.shape, F32),
        compiler_params=_cparams("parallel", "parallel"),
        name="add_pos",
    )(x, pos)


def _modnorm(x, g, shift, scale):
    ms = jnp.mean(x * x, axis=-1, keepdims=True)
    y = x * lax.rsqrt(ms + RMS_EPS) * g
    return y * (1.0 + scale) + shift


INPROJ_ROWS = 2048
CONV_ROWS = 256


def _prenorm_kernel(prev_ref, x_ref, next_ref, g_ref, sh_ref, sc_ref, o_ref):
    k = pl.program_id(2)
    rows = x_ref.shape[1]
    tl = o_ref.shape[2] - 2 * HALO
    norm = lambda v: _modnorm(v, g_ref[...], sh_ref[0], sc_ref[0]).astype(o_ref.dtype)
    o_ref[0, 0, pl.ds(pl.multiple_of(HALO + k * rows, HALO), rows), :] = norm(x_ref[0])

    @pl.when(k == 0)
    def _():
        o_ref[0, 0, 0:HALO, :] = norm(prev_ref[0])

    @pl.when(k == pl.num_programs(2) - 1)
    def _():
        o_ref[0, 0, HALO + tl:, :] = norm(next_ref[0])


def _prenorm(x, g, shift, scale):
    b, l, d = x.shape
    tl = min(l, INPROJ_ROWS)
    rows = min(tl, 512)
    nk = tl // rows
    rpb = tl // HALO
    last = l // HALO - 1
    vec = pl.BlockSpec((1, 1, d), lambda bb, i, k: (bb, 0, 0))
    return pl.pallas_call(
        _prenorm_kernel,
        grid=(b, l // tl, nk),
        in_specs=[pl.BlockSpec((1, HALO, d), lambda bb, i, k: (bb, jnp.maximum(i * rpb - 1, 0), 0)),
                  pl.BlockSpec((1, rows, d), lambda bb, i, k: (bb, i * nk + k, 0)),
                  pl.BlockSpec((1, HALO, d), lambda bb, i, k: (bb, jnp.minimum((i + 1) * rpb, last), 0)),
                  pl.BlockSpec((1, d), lambda bb, i, k: (0, 0)), vec, vec],
        out_specs=pl.BlockSpec((1, 1, tl + 2 * HALO, d), lambda bb, i, k: (bb, i, 0, 0)),
        out_shape=jax.ShapeDtypeStruct((b, l // tl, tl + 2 * HALO, d), BF16),
        compiler_params=_cparams("parallel", "parallel", "arbitrary"),
        name="prenorm",
    )(x, x, x, g.reshape(1, d), shift, scale)


def _inproj_kernel(*refs, n_plain, n_conv):
    if n_plain:
        h_ref, w_ref, cw_ref, cb_ref, o_ref, xbc_ref, dt_ref, acc_scr = refs
    else:
        h_ref, w_ref, cw_ref, cb_ref, xbc_ref, dt_ref, acc_scr = refs
    i = pl.program_id(1)
    k = pl.program_id(2) - n_plain
    tm = dt_ref.shape[1]
    project = lambda: jnp.dot(h_ref[0, 0], w_ref[...], preferred_element_type=F32)

    def project_into(dst):
        dst[...] = project()
        dst[0:HALO, :] = jnp.where(i > 0, dst[0:HALO, :], 0.0)
        dst[HALO + tm:, :] = jnp.where(i < pl.num_programs(1) - 1, dst[HALO + tm:, :], 0.0)

    def conv_from(src):
        pad = M_CONV // 2
        rows = min(tm, CONV_ROWS)
        w_half = 0.5 * cw_ref[...]
        b_half = 0.5 * cb_ref[...]
        for r0 in range(0, tm, rows):
            ext = src[r0:r0 + rows + 2 * HALO, :]
            acc = None
            for tap in range(M_CONV):
                shifted = ext if tap == pad else pltpu.roll(ext, (pad - tap) % (rows + 2 * HALO), axis=0)
                term = shifted[HALO:HALO + rows] * w_half[tap:tap + 1, :]
                acc = term if acc is None else acc + term
            xbc_ref[0, r0:r0 + rows, :] = _silu_of_twice(acc + b_half).astype(xbc_ref.dtype)

    if n_plain:
        @pl.when(k < 0)
        def _():
            o_ref[0] = project()[HALO:HALO + tm].astype(o_ref.dtype)

    @pl.when(jnp.logical_and(k >= 0, k < n_conv))
    def _():
        project_into(acc_scr)
        conv_from(acc_scr)

    @pl.when(k == n_conv)
    def _():
        dt_ref[0] = project()[HALO:HALO + tm, 0:LANES]


def _proj_kernel(h_ref, w_ref, o_ref):
    tm = o_ref.shape[1]
    acc = jnp.dot(h_ref[0, 0], w_ref[...], preferred_element_type=F32)
    o_ref[0] = acc[HALO:HALO + tm].astype(o_ref.dtype)


def _proj_plain(h, w, ncols):
    b, nt, tmh, d = h.shape
    tm = tmh - 2 * HALO
    tn = 2 * COL_TILE if ncols % (2 * COL_TILE) == 0 else COL_TILE
    return pl.pallas_call(
        _proj_kernel,
        grid=(b, nt, ncols // tn),
        in_specs=[pl.BlockSpec((1, 1, tmh, d), lambda bb, i, j: (bb, i, 0, 0)),
                  pl.BlockSpec((d, tn), lambda bb, i, j: (0, j))],
        out_specs=pl.BlockSpec((1, tm, tn), lambda bb, i, j: (bb, i, j)),
        out_shape=jax.ShapeDtypeStruct((b, nt * tm, ncols), BF16),
        compiler_params=_cparams("parallel", "parallel", "arbitrary"),
        name="proj_plain",
    )(h, w)


def _inproj(h, w, col0, n_plain, conv_w, conv_b):
    b, nt, tmh, d = h.shape
    tm = tmh - 2 * HALO
    l = nt * tm
    tn = COL_TILE
    j0 = col0 // tn
    n_conv = conv_w.shape[1] // tn
    nj = n_plain + n_conv + 1
    conv_idx = lambda j: jnp.clip(j - n_plain, 0, n_conv - 1)
    out_specs = [pl.BlockSpec((1, tm, tn), lambda bb, i, j: (bb, i, conv_idx(j))),
                 pl.BlockSpec((1, tm, LANES), lambda bb, i, j: (bb, i, 0))]
    out_shape = [jax.ShapeDtypeStruct((b, l, n_conv * tn), BF16), jax.ShapeDtypeStruct((b, l, LANES), F32)]
    if n_plain:
        out_specs.insert(0, pl.BlockSpec((1, tm, tn), lambda bb, i, j: (bb, i, jnp.minimum(j, n_plain - 1))))
        out_shape.insert(0, jax.ShapeDtypeStruct((b, l, n_plain * tn), BF16))
    outs = pl.pallas_call(
        functools.partial(_inproj_kernel, n_plain=n_plain, n_conv=n_conv),
        grid=(b, nt, nj),
        in_specs=[pl.BlockSpec((1, 1, tmh, d), lambda bb, i, j: (bb, i, 0, 0)),
                  pl.BlockSpec((d, tn), lambda bb, i, j: (0, j + j0)),
                  pl.BlockSpec((M_CONV, tn), lambda bb, i, j: (0, conv_idx(j))),
                  pl.BlockSpec((1, tn), lambda bb, i, j: (0, conv_idx(j)))],
        out_specs=out_specs,
        out_shape=out_shape,
        scratch_shapes=[pltpu.VMEM((tmh, tn), F32)],
        compiler_params=_cparams("parallel", "parallel", "arbitrary"),
        name="inproj",
    )(h, w, conv_w, conv_b.reshape(1, -1))
    return outs if n_plain else [None] + list(outs)


def _conv_taps(prev_ref, main_ref, next_ref, w_ref, b_ref, ktaps):
    i = pl.program_id(2)
    nblk = pl.num_programs(2)
    tl = main_ref.shape[1]
    pad = ktaps // 2
    prev = jnp.where(i > 0, prev_ref[0].astype(F32), 0.0)
    nxt = jnp.where(i < nblk - 1, next_ref[0].astype(F32), 0.0)
    ext = jnp.concatenate([prev, main_ref[0].astype(F32), nxt], axis=0)
    rows = tl + 2 * HALO
    acc = None
    for j in range(ktaps):
        shifted = ext if j == pad else pltpu.roll(ext, (pad - j) % rows, axis=0)
        term = shifted[HALO:HALO + tl] * w_ref[j:j + 1, :]
        acc = term if acc is None else acc + term
    return acc + b_ref[...]


def _conv_specs(tl, cw, l, col_block0):
    rpb = tl // HALO
    last = l // HALO - 1
    return [
        pl.BlockSpec((1, HALO, cw), lambda bb, c, i: (bb, jnp.maximum(i * rpb - 1, 0), c + col_block0)),
        pl.BlockSpec((1, tl, cw), lambda bb, c, i: (bb, i, c + col_block0)),
        pl.BlockSpec((1, HALO, cw), lambda bb, c, i: (bb, jnp.minimum((i + 1) * rpb, last), c + col_block0)),
    ]


def _hconv_kernel(p0, m0, n0, p1, m1, n1, p2, m2, n2, w0, b0, w1, b1, w2, b2, x0_ref, u_ref):
    x0_ref[0] = _conv_taps(p0, m0, n0, w0, b0, H_SHORT).astype(x0_ref.dtype)
    x1 = _conv_taps(p1, m1, n1, w1, b1, H_SHORT)
    v = _conv_taps(p2, m2, n2, w2, b2, H_SHORT)
    u_ref[0] = (x1 * v).astype(u_ref.dtype)


def _hconv(p, col0, w, bias, d):
    b, l, _ = p.shape
    tl = min(l, 1024)
    cw = min(d, COL_TILE)
    nb = d // cw
    specs, wspecs = [], []
    for s in range(3):
        specs += _conv_specs(tl, cw, l, col0 // cw + s * nb)
        wspecs += [pl.BlockSpec((H_SHORT, cw), functools.partial(lambda bb, c, i, s: (0, c + s * nb), s=s)),
                   pl.BlockSpec((1, cw), functools.partial(lambda bb, c, i, s: (0, c + s * nb), s=s))]
    bias2 = bias.reshape(1, 3 * d)
    out_spec = pl.BlockSpec((1, tl, cw), lambda bb, c, i: (bb, i, c))
    return pl.pallas_call(
        _hconv_kernel,
        grid=(b, nb, l // tl),
        in_specs=specs + wspecs,
        out_specs=[out_spec, out_spec],
        out_shape=[jax.ShapeDtypeStruct((b, l, d), BF16)] * 2,
        compiler_params=_cparams("parallel", "parallel", "arbitrary"),
        name="hconv",
    )(p, p, p, p, p, p, p, p, p, w, bias2, w, bias2, w, bias2)


def _ssd_kernel(*refs, reverse, with_output, epilogue, heads):
    if epilogue:
        (xbc_ref, dt_ref, dtb_ref, alog_ref, init_ref, sel_ref, yf_ref, z_ref, dsk_ref, ng_ref,
         y_ref, fin_ref, s_scr) = refs
    elif with_output:
        xbc_ref, dt_ref, dtb_ref, alog_ref, init_ref, sel_ref, y_ref, fin_ref, s_scr = refs
    else:
        xbc_ref, dt_ref, dtb_ref, alog_ref, init_ref, fin_ref, s_scr = refs
    t = M_CHUNK
    n = M_STATE
    assert n == t
    di = heads * M_HEADDIM
    gw = di // M_GROUPS
    rpg = heads // M_GROUPS
    lane0 = heads if reverse else 0
    edge = 0 if reverse else t - 1
    c = pl.program_id(1)

    nb = xbc_ref.shape[0]

    @pl.when(c == 0)
    def _():
        s_scr[...] = init_ref[...]

    row = lax.broadcasted_iota(jnp.int32, (t, t), 0)
    col = lax.broadcasted_iota(jnp.int32, (t, t), 1)
    mask = (row <= col) if reverse else (row >= col)
    lane = lax.broadcasted_iota(jnp.int32, (1, LANES), 1)
    lo = lane < M_HEADDIM
    mlo = jnp.where(lo, 1.0, 0.0).astype(BF16)
    mhi = jnp.where(lo, 0.0, 1.0).astype(BF16)

    def chunk_decays(bi):
        x = dt_ref[bi] + dtb_ref[...]
        dt = jnp.maximum(x, 0.0) + jnp.log1p(jnp.exp(-jnp.abs(x)))
        dta = dt * (-jnp.exp(alog_ref[...]))
        acs = jnp.dot(mask.astype(F32), dta, preferred_element_type=F32, precision=HIGHEST)
        acs_t = acs.T
        dt_t = dt.T
        acs2 = acs * LOG2_E
        arow_dt2 = (acs_t - jnp.log(dt_t)) * LOG2_E
        w_t = jnp.exp(acs_t[:, edge:edge + 1] - acs_t) * dt_t
        dec = jnp.exp(acs[edge:edge + 1, :])
        eacs = jnp.exp(acs).astype(BF16) if with_output else None
        return acs2, arow_dt2, w_t, dec, eacs

    def group(bi, g, acs2, arow_dt2, w_t, dec, eacs):
        bt = xbc_ref[bi, :, di + g * n:di + (g + 1) * n].astype(F32).T
        if with_output:
            cm = xbc_ref[bi, :, di + M_GROUPS * n + g * n:di + M_GROUPS * n + (g + 1) * n]
            cb = jnp.dot(cm, bt.astype(BF16), preferred_element_type=F32)
            y_off = jnp.dot(cm, s_scr[bi, g].astype(BF16), preferred_element_type=F32)
        y_parts = []
        for q in range(rpg // 2):
            c0 = g * gw + q * LANES
            xp = xbc_ref[bi, :, c0:c0 + LANES]
            sp = s_scr[bi, g, :, q * LANES:(q + 1) * LANES]
            x2 = jnp.concatenate([xp * mlo, xp * mhi], axis=0)
            hl0 = lane0 + g * rpg + 2 * q
            lhs, btw, decs = [], [], []
            for r in range(2):
                hl = hl0 + r
                if with_output:
                    acol = jnp.broadcast_to(acs2[:, hl:hl + 1], (t, t))
                    dk = jnp.exp2(jnp.where(mask, acol - arow_dt2[hl:hl + 1, :], -jnp.inf))
                    lhs.append((cb * dk).astype(BF16))
                btw.append((bt * w_t[hl:hl + 1, :]).astype(BF16))
                decs.append(jnp.broadcast_to(dec[:, hl:hl + 1], (1, LANES)))
            if with_output:
                e_pair = jnp.dot(eacs, sel_ref[hl0 // 2], preferred_element_type=F32)
                y_diag = jnp.dot(jnp.concatenate(lhs, axis=1), x2, preferred_element_type=F32)
                y_parts.append(y_diag + e_pair * y_off[:, q * LANES:(q + 1) * LANES])
            ds = jnp.dot(jnp.concatenate(btw, axis=1), x2, preferred_element_type=F32)
            s_scr[bi, g, :, q * LANES:(q + 1) * LANES] = jnp.where(lo, decs[0], decs[1]) * sp + ds
        if with_output:
            yg = jnp.concatenate(y_parts, axis=1) if len(y_parts) > 1 else y_parts[0]
            sl = slice(g * gw, (g + 1) * gw)
            if epilogue:
                yg = yg + yf_ref[bi, :, sl].astype(F32)
                v = (yg + xbc_ref[bi, :, sl].astype(F32) * dsk_ref[:, sl]) * _silu(z_ref[bi, :, sl].astype(F32))
                ms = jnp.mean(v * v, axis=-1, keepdims=True)
                y_ref[bi, :, sl] = (v * lax.rsqrt(ms + RMS_EPS) * ng_ref[:, sl]).astype(y_ref.dtype)
            else:
                y_ref[bi, :, sl] = yg.astype(y_ref.dtype)

    decays = [chunk_decays(bi) for bi in range(nb)]
    for g in range(M_GROUPS):
        for bi in range(nb):
            group(bi, g, *decays[bi])

    @pl.when(c == pl.num_programs(1) - 1)
    def _():
        fin_ref[...] = s_scr[...]


def _pair_selectors():
    p = jnp.arange(LANES // 2)[:, None, None]
    h = jnp.arange(LANES)[None, :, None]
    lane = jnp.arange(LANES)[None, None, :]
    return (h == 2 * p + (lane >= M_HEADDIM)).astype(BF16)


def _ssd(xbc, dtp, dtb, alog, init, heads, reverse, with_output, extra=None):
    b, l, xw = xbc.shape
    t = M_CHUNK
    nc = l // t
    di = heads * M_HEADDIM
    gw = di // M_GROUPS
    epilogue = extra is not None
    cidx = (lambda c: nc - 1 - c) if reverse else (lambda c: c)
    nb = SSD_BATCH_ROWS if b % SSD_BATCH_ROWS == 0 else 1
    state_spec = pl.BlockSpec((nb, M_GROUPS, M_STATE, gw), lambda bb, c: (bb, 0, 0, 0))
    in_specs = [pl.BlockSpec((nb, t, xw), lambda bb, c: (bb, cidx(c), 0)),
                pl.BlockSpec((nb, t, LANES), lambda bb, c: (bb, cidx(c), 0)),
                pl.BlockSpec((1, LANES), lambda bb, c: (0, 0)),
                pl.BlockSpec((1, LANES), lambda bb, c: (0, 0)),
                state_spec]
    args = [xbc, dtp, dtb, alog, init]
    out_specs, out_shape = [], []
    if with_output:
        in_specs.append(pl.BlockSpec((LANES // 2, LANES, LANES), lambda bb, c: (0, 0, 0)))
        args.append(_pair_selectors())
    if epilogue:
        yf, p, z_col0, dsk, ng = extra
        in_specs += [pl.BlockSpec((nb, t, di), lambda bb, c: (bb, cidx(c), 0)),
                     pl.BlockSpec((nb, t, di), lambda bb, c: (bb, cidx(c), z_col0 // di)),
                     pl.BlockSpec((1, di), lambda bb, c: (0, 0)),
                     pl.BlockSpec((1, di), lambda bb, c: (0, 0))]
        args += [yf, p, dsk, ng]
    if with_output:
        out_specs.append(pl.BlockSpec((nb, t, di), lambda bb, c: (bb, cidx(c), 0)))
        out_shape.append(jax.ShapeDtypeStruct((b, l, di), BF16))
    out_specs.append(state_spec)
    out_shape.append(jax.ShapeDtypeStruct((b, M_GROUPS, M_STATE, gw), F32))
    outs = pl.pallas_call(
        functools.partial(_ssd_kernel, reverse=reverse, with_output=with_output, epilogue=epilogue, heads=heads),
        grid=(b // nb, nc),
        in_specs=in_specs,
        out_specs=out_specs,
        out_shape=out_shape,
        scratch_shapes=[pltpu.VMEM((nb, M_GROUPS, M_STATE, gw), F32)],
        compiler_params=_cparams("parallel", "arbitrary"),
        name="ssd_bwd" if reverse else "ssd_fwd",
    )(*args)
    return (outs[0], outs[1]) if with_output else (None, outs[0])


def _filt_kernel(f_ref, w1_ref, b1_ref, w2_ref, b2_ref, fr_ref, w3_ref, dl_ref, o_ref, ss_ref):
    f = f_ref[...]
    fr = fr_ref[...]
    h = jnp.sin(fr * (jnp.dot(f, w1_ref[...], preferred_element_type=F32, precision=HIGHEST) + b1_ref[...]))
    h = jnp.sin(fr * (jnp.dot(h, w2_ref[...], preferred_element_type=F32, precision=HIGHEST) + b2_ref[...]))
    filt = jnp.dot(h, w3_ref[...], preferred_element_type=F32, precision=HIGHEST)
    tcol = f[:, H_EMB:H_EMB + 1]
    mcol = f[:, H_EMB + 1:H_EMB + 2]
    out = filt * jnp.exp(-tcol * dl_ref[...]) * mcol
    o_ref[...] = out

    @pl.when(pl.program_id(1) == 0)
    def _():
        ss_ref[...] = jnp.zeros_like(ss_ref)

    ss_ref[...] += jnp.sum(out * out, axis=0, keepdims=True)


def _filter_features(l):
    t = jnp.linspace(0.0, 1.0, l, dtype=F32)[:, None]
    ang = (2.0 * math.pi / l) * jnp.arange(l, dtype=F32)[:, None]
    nb = (H_EMB - 1) // 2
    bands = jnp.linspace(1e-4, nb - 1, nb, dtype=F32)[None, :]
    feats = jnp.concatenate([t, jnp.cos(bands * ang), -jnp.sin(bands * ang), t, jnp.ones_like(t)], axis=-1)
    rev = jnp.concatenate([feats[0:1] * 0.0, jnp.flip(feats[1:], axis=0)], axis=0)
    full = jnp.concatenate([feats, rev], axis=0)
    return jnp.pad(full, ((0, 0), (0, FEAT_COLS - full.shape[1])))


def _hyena_filter(l, w1, b1, w2, b2, freq, w3):
    hid = w2.shape[0]
    d = w3.shape[1] // 2
    feats = _filter_features(l)
    w1p = jnp.pad(w1, ((0, FEAT_COLS - w1.shape[0]), (0, 0)))
    deltas = jnp.abs(jnp.linspace(math.log(H_DECAY_TARGET) / H_SLOW_PCT, math.log(H_DECAY_TARGET) / H_FAST_PCT,
                                  d, dtype=F32))[None, :]
    tl = min(l, 512)
    cw = d
    nb = d // cw
    nrow = l // tl
    small = lambda shape: pl.BlockSpec(shape, lambda c, i: (0, 0))
    return pl.pallas_call(
        _filt_kernel,
        grid=(nb, 2 * nrow),
        in_specs=[pl.BlockSpec((tl, FEAT_COLS), lambda c, i: (i, 0)),
                  small((FEAT_COLS, hid)), small((1, hid)), small((hid, hid)), small((1, hid)), small((1, hid)),
                  pl.BlockSpec((hid, cw), lambda c, i: (0, c + (i // nrow) * nb)),
                  pl.BlockSpec((1, cw), lambda c, i: (0, c))],
        out_specs=[pl.BlockSpec((tl, cw), lambda c, i: (i, c)),
                   pl.BlockSpec((1, cw), lambda c, i: (0, c))],
        out_shape=[jax.ShapeDtypeStruct((2 * l, d), F32), jax.ShapeDtypeStruct((1, d), F32)],
        compiler_params=_cparams("parallel", "arbitrary"),
        name="hyena_filter",
    )(feats, w1p, b1.reshape(1, hid), w2, b2.reshape(1, hid), freq.reshape(1, hid), w3, deltas)


def _fft_split(n):
    n1 = 1 << ((n.bit_length() - 1 + 1) // 2)
    return n1, n // n1


def _dft_consts(n1, n2):
    n = n1 * n2
    h = n1 // 2
    k1 = jnp.arange(n1, dtype=jnp.int32)
    t1 = jnp.arange(n1, dtype=jnp.int32)
    t2 = jnp.arange(n2, dtype=jnp.int32)
    idx = (k1[None, :, None] * (n2 * t1[None, None, :] + t2[:, None, None])) % n
    ang = idx.astype(F32) * (2.0 * math.pi / n)
    cs, sn = jnp.cos(ang), jnp.sin(ang)
    ch, sh = cs[..., :h], sn[..., :h]
    t_data = jnp.concatenate([jnp.concatenate([ch, sh], -1), jnp.concatenate([-sh, ch], -1)], axis=1)
    t_filt = jnp.concatenate([cs, -sn], axis=1)
    ct, st = jnp.swapaxes(ch, 1, 2), jnp.swapaxes(sh, 1, 2)
    t_inv = jnp.concatenate([jnp.concatenate([ct, -st], -1), jnp.concatenate([st, ct], -1)], axis=1) / n
    k2 = jnp.arange(n2, dtype=jnp.int32)
    ang2 = ((k2[:, None] * k2[None, :]) % n2).astype(F32) * (2.0 * math.pi / n2)
    c2, s2 = jnp.cos(ang2), jnp.sin(ang2)
    m_fwd = jnp.concatenate([jnp.concatenate([c2, s2], -1), jnp.concatenate([-s2, c2], -1)], axis=0)
    m_inv = jnp.concatenate([jnp.concatenate([c2, -s2], -1), jnp.concatenate([s2, c2], -1)], axis=0)
    return dict(t_data=t_data.astype(BF16), t_filt=t_filt.astype(BF16), t_inv=t_inv.astype(BF16),
                m_fwd=m_fwd.astype(BF16), m_inv=m_inv.astype(BF16))


def _fft1_kernel(x_ref, t_ref, o_ref):
    a = x_ref.shape[1]
    xs = [pltpu.einshape("rjc->jrc", x_ref[0, s].astype(F32)) for s in range(a)]
    n1 = o_ref.shape[2]
    for j in range(SUBLANES):
        x = jnp.concatenate([xs[s][j] for s in range(a)], axis=0).astype(BF16)
        o = jnp.dot(t_ref[j], x, preferred_element_type=F32)
        o_ref[0, j] = _pack_complex(o[:n1], o[n1:])


def _pack_complex(re, im):
    r = lax.bitcast_convert_type(re.astype(BF16).astype(F32), jnp.uint32)
    i = lax.bitcast_convert_type(im.astype(BF16).astype(F32), jnp.uint32)
    return (r >> 16) | i


def _unpack_complex(w):
    re = lax.bitcast_convert_type(w << 16, F32)
    im = lax.bitcast_convert_type(w & jnp.uint32(0xFFFF0000), F32)
    return re, im


def _fft_stage1(xv, tmat, n1, n2, d):
    p, a, r, _, _ = xv.shape
    cw = min(d, 2 * COL_TILE)
    return pl.pallas_call(
        _fft1_kernel,
        grid=(n2 // SUBLANES, d // cw, p),
        in_specs=[pl.BlockSpec((1, a, r, SUBLANES, cw), lambda j, c, pp: (pp, 0, 0, j, c)),
                  pl.BlockSpec((SUBLANES, 2 * n1, n1), lambda j, c, pp: (j, 0, 0))],
        out_specs=pl.BlockSpec((1, SUBLANES, n1, cw), lambda j, c, pp: (pp, j, 0, c)),
        out_shape=jax.ShapeDtypeStruct((p, n2, n1, d), jnp.uint32),
        compiler_params=_cparams("parallel", "parallel", "parallel"),
        name="fft_stage1",
    )(xv, tmat)


def _fft2_kernel(a_ref, af_ref, ss_ref, mf_ref, mi_ref, o_ref, h_scr):
    n2 = a_ref.shape[1]

    def stage(w):
        re, im = _unpack_complex(w)
        return jnp.dot(mf_ref[...], jnp.concatenate([re, im], axis=0).astype(BF16), preferred_element_type=F32)

    @pl.when(pl.program_id(2) == 0)
    def _():
        af = pltpu.einshape("tkc->ktc", af_ref[0])
        for k in range(SUBLANES):
            h_scr[k] = stage(af[k]) * lax.rsqrt(ss_ref[...] + RMS_EPS)

    cw = a_ref.shape[3]
    half = max(cw // 2, LANES)
    for c0 in range(0, cw, half):
        a = pltpu.einshape("tkc->ktc", a_ref[0, :, :, c0:c0 + half])
        outs = []
        for k in range(SUBLANES):
            xk = stage(a[k])
            xr, xi = xk[:n2], xk[n2:]
            hr, hi = h_scr[k, 0:n2, c0:c0 + half], h_scr[k, n2:, c0:c0 + half]
            z = jnp.concatenate([xr * hr - xi * hi, xr * hi + xi * hr], axis=0).astype(BF16)
            bk = jnp.dot(mi_ref[...], z, preferred_element_type=F32)
            outs.append(_pack_complex(bk[:n2], bk[n2:]))
        o_ref[0, :, :, c0:c0 + half] = pltpu.einshape("ktc->tkc", jnp.stack(outs, axis=0))


def _fft_stage2(av, afv, ss, m_fwd, m_inv, n1, n2, d):
    p = av.shape[0]
    cw = min(d, COL_TILE)
    kb = SUBLANES
    return pl.pallas_call(
        _fft2_kernel,
        grid=(n1 // kb, d // cw, p),
        in_specs=[pl.BlockSpec((1, n2, kb, cw), lambda k, c, pp: (pp, 0, k, c)),
                  pl.BlockSpec((1, n2, kb, cw), lambda k, c, pp: (0, 0, k, c)),
                  pl.BlockSpec((1, cw), lambda k, c, pp: (0, c)),
                  pl.BlockSpec((2 * n2, 2 * n2), lambda k, c, pp: (0, 0)),
                  pl.BlockSpec((2 * n2, 2 * n2), lambda k, c, pp: (0, 0))],
        out_specs=pl.BlockSpec((1, n2, kb, cw), lambda k, c, pp: (pp, 0, k, c)),
        out_shape=jax.ShapeDtypeStruct(av.shape, jnp.uint32),
        scratch_shapes=[pltpu.VMEM((kb, 2 * n2, cw), F32)],
        compiler_params=_cparams("parallel", "parallel", "arbitrary"),
        name="fft_stage2",
    )(av, afv, ss, m_fwd, m_inv)


def _fft3_kernel(b_ref, t_ref, u_ref, x0_ref, bias_ref, o_ref):
    n1 = b_ref.shape[2]
    h = n1 // 2
    ys = []
    for j in range(SUBLANES):
        re, im = _unpack_complex(b_ref[0, j])
        bv = jnp.concatenate([re, im], axis=0).astype(BF16)
        ys.append(jnp.dot(t_ref[j], bv, preferred_element_type=F32))
    y = pltpu.einshape("jtc->tjc", jnp.stack(ys, axis=0))
    for s in range(2):
        conv = y[s * h:(s + 1) * h] + u_ref[0, s].astype(F32) * bias_ref[...]
        o_ref[0, s] = (x0_ref[0, s].astype(F32) * conv).astype(o_ref.dtype)


def _fft_stage3(bv, t_inv, uv, x0v, bias, n1, n2, d):
    p = bv.shape[0]
    h = n1 // 2
    cw = min(d, 2 * COL_TILE)
    io_spec = pl.BlockSpec((1, 2, h, SUBLANES, cw), lambda j, c, pp: (pp, 0, 0, j, c))
    return pl.pallas_call(
        _fft3_kernel,
        grid=(n2 // SUBLANES, d // cw, p),
        in_specs=[pl.BlockSpec((1, SUBLANES, n1, cw), lambda j, c, pp: (pp, j, 0, c)),
                  pl.BlockSpec((SUBLANES, n1, 2 * n1), lambda j, c, pp: (j, 0, 0)),
                  io_spec, io_spec,
                  pl.BlockSpec((1, cw), lambda j, c, pp: (0, c))],
        out_specs=io_spec,
        out_shape=jax.ShapeDtypeStruct(uv.shape, BF16),
        compiler_params=_cparams("parallel", "parallel", "parallel"),
        name="fft_stage3",
    )(bv, t_inv, uv, x0v, bias)


def _filter_spectrum(l, consts, lp):
    n1, n2 = _fft_split(2 * l)
    d = lp['h_bias'].shape[0]
    full, ss = _hyena_filter(l, lp['hf_w1'], lp['hf_b1'], lp['hf_w2'], lp['hf_b2'], lp['hf_freq'], lp['hf_w3'])
    return _fft_stage1(full.reshape(1, 1, n1, n2, d), consts['t_filt'], n1, n2, d), ss


def _long_conv(u, x0, hf, consts, bias):
    b, l, d = u.shape
    n1, n2 = _fft_split(2 * l)
    p = b // 2
    h = n1 // 2
    uv = u.reshape(p, 2, h, n2, d)
    a = _fft_stage1(uv, consts['t_data'], n1, n2, d)
    bv = _fft_stage2(a, hf[0], hf[1], consts['m_fwd'], consts['m_inv'], n1, n2, d)
    y = _fft_stage3(bv, consts['t_inv'], uv, x0.reshape(p, 2, h, n2, d), bias.reshape(1, d), n1, n2, d)
    return y.reshape(b, l, d)


def _merge_kernel(ym_ref, yh_ref, ga_ref, gb_ref, wm_ref, wh_ref, o_ref):
    a = jnp.dot(ym_ref[0], wm_ref[...], preferred_element_type=F32)
    bb = jnp.dot(yh_ref[0], wh_ref[...], preferred_element_type=F32)
    ga = jax.nn.sigmoid(ga_ref[0].astype(F32))
    gb = jax.nn.sigmoid(gb_ref[0].astype(F32))
    o_ref[0] = (ga * a + gb * bb).astype(o_ref.dtype)


def _merge(ym, yh, p, gate_col0, wm, wh):
    b, l, di = ym.shape
    d = yh.shape[2]
    tm = min(l, 1024)
    tn = min(d, COL_TILE)
    g0 = gate_col0 // tn
    nb = d // tn
    return pl.pallas_call(
        _merge_kernel,
        grid=(b, l // tm, nb),
        in_specs=[pl.BlockSpec((1, tm, di), lambda bb, i, j: (bb, i, 0)),
                  pl.BlockSpec((1, tm, d), lambda bb, i, j: (bb, i, 0)),
                  pl.BlockSpec((1, tm, tn), lambda bb, i, j: (bb, i, g0 + j)),
                  pl.BlockSpec((1, tm, tn), lambda bb, i, j: (bb, i, g0 + nb + j)),
                  pl.BlockSpec((di, tn), lambda bb, i, j: (0, j)),
                  pl.BlockSpec((d, tn), lambda bb, i, j: (0, j))],
        out_specs=pl.BlockSpec((1, tm, tn), lambda bb, i, j: (bb, i, j)),
        out_shape=jax.ShapeDtypeStruct((b, l, d), BF16),
        compiler_params=_cparams("parallel", "parallel", "arbitrary"),
        name="merge",
    )(ym, yh, p, p, wm, wh)


def _resid_mm_kernel(a_ref, w_ref, x_ref, g_ref, o_ref):
    o_ref[0] = x_ref[0] + g_ref[0] * jnp.dot(a_ref[0], w_ref[...], preferred_element_type=F32)


def _resid_mm(a, w, x, gate):
    b, l, k = a.shape
    d = w.shape[1]
    tm = min(l, 1024)
    tn = min(d, 2 * COL_TILE)
    return pl.pallas_call(
        _resid_mm_kernel,
        grid=(b, l // tm, d // tn),
        in_specs=[pl.BlockSpec((1, tm, k), lambda bb, i, j: (bb, i, 0)),
                  pl.BlockSpec((k, tn), lambda bb, i, j: (0, j)),
                  pl.BlockSpec((1, tm, tn), lambda bb, i, j: (bb, i, j)),
                  pl.BlockSpec((1, 1, tn), lambda bb, i, j: (bb, 0, j))],
        out_specs=pl.BlockSpec((1, tm, tn), lambda bb, i, j: (bb, i, j)),
        out_shape=jax.ShapeDtypeStruct((b, l, d), F32),
        compiler_params=_cparams("parallel", "parallel", "arbitrary"),
        name="resid_mm",
    )(a, w, x, gate)


def _ffn_kernel(x_ref, g_ref, sh_ref, sc_ref, gt_ref, wg_ref, wu_ref, wd_ref, fg_ref, o_ref, h_scr, *, final_norm):
    j = pl.program_id(2)

    @pl.when(j == 0)
    def _():
        h_scr[...] = _modnorm(x_ref[0], g_ref[...], sh_ref[0], sc_ref[0]).astype(BF16)
        o_ref[...] = jnp.zeros_like(o_ref)

    tm = h_scr.shape[0]
    rows_per_pass = min(tm, 512)
    for r0 in range(0, tm, rows_per_pass):
        h = h_scr[r0:r0 + rows_per_pass, :]
        gg = jnp.dot(h, wg_ref[...], preferred_element_type=F32)
        uu = jnp.dot(h, wu_ref[...], preferred_element_type=F32)
        o_ref[0, r0:r0 + rows_per_pass, :] += jnp.dot((_silu(gg) * uu).astype(BF16), wd_ref[...],
                                                      preferred_element_type=F32)

    @pl.when(j == pl.num_programs(2) - 1)
    def _():
        y = x_ref[0] + gt_ref[0] * o_ref[0]
        if final_norm:
            ms = jnp.mean(y * y, axis=-1, keepdims=True)
            y = y * lax.rsqrt(ms + RMS_EPS) * fg_ref[...]
        o_ref[0] = y


def _ffn(x, g, shift, scale, gate, w_gu, w_down, final_g=None):
    b, l, d = x.shape
    f = w_down.shape[0]
    tm = min(l, 1024)
    fg = jnp.ones((1, d), F32) if final_g is None else final_g.reshape(1, d)
    tf = 256
    nf = f // tf
    vec = pl.BlockSpec((1, 1, d), lambda bb, i, j: (bb, 0, 0))
    return pl.pallas_call(
        functools.partial(_ffn_kernel, final_norm=final_g is not None),
        grid=(b, l // tm, nf),
        in_specs=[pl.BlockSpec((1, tm, d), lambda bb, i, j: (bb, i, 0)),
                  pl.BlockSpec((1, d), lambda bb, i, j: (0, 0)),
                  vec, vec, vec,
                  pl.BlockSpec((d, tf), lambda bb, i, j: (0, j)),
                  pl.BlockSpec((d, tf), lambda bb, i, j: (0, j + nf)),
                  pl.BlockSpec((tf, d), lambda bb, i, j: (j, 0)),
                  pl.BlockSpec((1, d), lambda bb, i, j: (0, 0))],
        out_specs=pl.BlockSpec((1, tm, d), lambda bb, i, j: (bb, i, 0)),
        out_shape=jax.ShapeDtypeStruct((b, l, d), F32),
        scratch_shapes=[pltpu.VMEM((tm, d), BF16)],
        compiler_params=_cparams("parallel", "parallel", "arbitrary"),
        name="ffn",
    )(x, g.reshape(1, d), shift, scale, gate, w_gu, w_gu, w_down, fg)


def _grid_pos_embed(rows, d):
    r, col = jnp.meshgrid(jnp.arange(rows), jnp.arange(GRID_W), indexing='ij')
    quarter = d // 4
    omega = 1.0 / (POS_BASE ** (jnp.arange(quarter, dtype=F32) / quarter))

    def axis_embed(pos):
        ang = pos.reshape(-1)[:, None].astype(F32) * omega[None, :]
        return jnp.concatenate([jnp.sin(ang), jnp.cos(ang)], axis=-1)

    return jnp.concatenate([axis_embed(r), axis_embed(col)], axis=-1).astype(F32)


class _Cols:
    def __init__(self, d):
        self.d = d
        self.di = 2 * d
        self.heads = self.di // M_HEADDIM
        self.xbc = self.di + 2 * M_GROUPS * M_STATE
        self.z0 = 0
        self.gate0 = self.z0 + self.di
        self.hy0 = self.gate0 + 2 * d
        self.xbc0 = self.hy0 + 3 * d
        self.dt0 = self.xbc0 + self.xbc
        self.total = self.dt0 + COL_TILE
        assert 2 * self.heads <= LANES and self.heads % (2 * M_GROUPS) == 0
        for off in (self.gate0, self.hy0, self.xbc0, self.dt0):
            assert off % COL_TILE == 0


def _pack_w_in(w_in, cols):
    o_dt = cols.xbc
    o_z = o_dt + 2 * cols.heads
    o_hy = o_z + cols.di
    o_gate = o_hy + 3 * cols.d
    parts = [w_in[:, o_z:o_hy], w_in[:, o_gate:], w_in[:, o_hy:o_gate], w_in[:, :o_dt], w_in[:, o_dt:o_z]]
    w = jnp.concatenate(parts, axis=1)
    return jnp.pad(w, ((0, 0), (0, cols.total - w.shape[1]))).astype(BF16)


def _pad_lanes(v):
    v = v.reshape(1, -1)
    return jnp.pad(v, ((0, 0), (0, LANES - v.shape[1])))


def _mix_sublayer(x, mods, lp, cols, hf, consts, init_f, init_b):
    h = _prenorm(x, lp['norm1_g'], mods[0], mods[1])
    p = _proj_plain(h, lp['w_in'], cols.xbc0)
    _, xbc, dtp = _inproj(h, lp['w_in'], cols.xbc0, 0, lp['m_conv_w'], lp['m_conv_b'])
    y_f, fin_f = _ssd(xbc, dtp, lp['dtb'], lp['alog'], init_f, cols.heads, False, True)
    ym, fin_b = _ssd(xbc, dtp, lp['dtb'], lp['alog'], init_b, cols.heads, True, True,
                     extra=(y_f, p, cols.z0, lp['dskip'], lp['m_norm_g']))
    x0, u = _hconv(p, cols.hy0, lp['h_conv_w'], lp['h_conv_b'], cols.d)
    yh = _long_conv(u, x0, hf, consts, lp['h_bias'])
    merged = _merge(ym, yh, p, cols.gate0, lp['m_w_out'], lp['h_w_out'])
    return _resid_mm(merged, lp['w_merge_out'], x, mods[2]), fin_f, fin_b


def _ctx_final_states(x, mods, lp, cols, init_f, init_b):
    h = _prenorm(x, lp['norm1_g'], mods[0], mods[1])
    _, xbc, dtp = _inproj(h, lp['w_in'], cols.xbc0, 0, lp['m_conv_w'], lp['m_conv_b'])
    _, fin_f = _ssd(xbc, dtp, lp['dtb'], lp['alog'], init_f, cols.heads, False, False)
    _, fin_b = _ssd(xbc, dtp, lp['dtb'], lp['alog'], init_b, cols.heads, True, False)
    return fin_f, fin_b


def kernel(x, c, ctx, c_ctx, ada_w, ada_b, norm1_g, w_in, m_conv_w, m_conv_b, m_dt_bias, m_a_log, m_d, m_norm_g,
           m_w_out, h_conv_w, h_conv_b, hf_w1, hf_b1, hf_w2, hf_b2, hf_freq, hf_w3, h_bias, h_w_out, w_merge_out,
           norm2_g, ffn_w_gu, ffn_w_down, final_g):
    b, l, d = x.shape
    lc = ctx.shape[1]
    depth = ada_w.shape[0]
    cols = _Cols(d)
    assert b % 2 == 0 and b + 1 <= SUBLANES and l % M_CHUNK == 0 and lc % M_CHUNK == 0

    cvecs = jnp.concatenate([c, c_ctx[None], jnp.zeros((SUBLANES - b - 1, d), F32)], axis=0)
    mods_all = _mods(cvecs, ada_w, ada_b)
    consts_l = _dft_consts(*_fft_split(2 * l))
    consts_c = _dft_consts(*_fft_split(2 * lc))

    x_l = _add_pos(x, _grid_pos_embed(l // GRID_W, d))
    x_c = ctx
    zero_state = jnp.zeros((b, M_GROUPS, M_STATE, cols.di // M_GROUPS), F32)
    for i in range(depth):
        lp = dict(norm1_g=norm1_g[i], w_in=_pack_w_in(w_in[i], cols), m_conv_w=m_conv_w[i], m_conv_b=m_conv_b[i],
                  dtb=_pad_lanes(m_dt_bias[i]), alog=_pad_lanes(m_a_log[i]),
                  dskip=jnp.repeat(m_d[i], M_HEADDIM).reshape(1, cols.di), m_norm_g=m_norm_g[i].reshape(1, cols.di),
                  m_w_out=m_w_out[i].astype(BF16), h_conv_w=h_conv_w[i], h_conv_b=h_conv_b[i],
                  hf_w1=hf_w1[i], hf_b1=hf_b1[i], hf_w2=hf_w2[i], hf_b2=hf_b2[i], hf_freq=hf_freq[i], hf_w3=hf_w3[i],
                  h_bias=h_bias[i], h_w_out=h_w_out[i].astype(BF16), w_merge_out=w_merge_out[i].astype(BF16),
                  norm2_g=norm2_g[i], ffn_w_gu=ffn_w_gu[i].astype(BF16), ffn_w_down=ffn_w_down[i].astype(BF16))
        m = mods_all[i].reshape(SUBLANES, ADA_CHUNKS, d)
        mods_l = [m[:b, k][:, None, :] for k in range(ADA_CHUNKS)]
        mods_c = [jnp.broadcast_to(m[b, k][None, None, :], (b, 1, d)) for k in range(ADA_CHUNKS)]
        if i < depth - 1:
            hf_c = _filter_spectrum(lc, consts_c, lp)
            x_c_mixed, fin_f, fin_b = _mix_sublayer(x_c, mods_c, lp, cols, hf_c, consts_c, zero_state, zero_state)
        else:
            fin_f, fin_b = _ctx_final_states(x_c, mods_c, lp, cols, zero_state, zero_state)
        hf_l = _filter_spectrum(l, consts_l, lp)
        x_l, _, _ = _mix_sublayer(x_l, mods_l, lp, cols, hf_l, consts_l, fin_f, fin_b)
        x_l = _ffn(x_l, lp['norm2_g'], mods_l[3], mods_l[4], mods_l[5], lp['ffn_w_gu'], lp['ffn_w_down'],
                   final_g=final_g if i == depth - 1 else None)
        if i < depth - 1:
            x_c = _ffn(x_c_mixed, lp['norm2_g'], mods_c[3], mods_c[4], mods_c[5], lp['ffn_w_gu'], lp['ffn_w_down'])
    return x_l
```

```python
import functools
import math

import jax
import jax.numpy as jnp
from jax import lax
from jax.experimental import pallas as pl
from jax.experimental.pallas import tpu as pltpu

F32 = jnp.float32
BF16 = jnp.bfloat16
HIGHEST = lax.Precision.HIGHEST

RMS_EPS = 1e-6
LOG2_E = 1.4426950408889634
GRID_W = 64
POS_BASE = 10000.0
ADA_CHUNKS = 6
M_HEADDIM = 64
M_GROUPS = 8
M_STATE = 128
M_CONV = 5
M_CHUNK = 128
H_SHORT = 3
H_EMB = 33
H_DECAY_TARGET = 1e-2
H_FAST_PCT = 0.3
H_SLOW_PCT = 1.5

LANES = 128
SUBLANES = 8
HALO = 16
VMEM_LIMIT_BYTES = 56 * 1024 * 1024

COL_TILE = 512
FEAT_COLS = 40
SSD_BATCH_ROWS = 2


def _cparams(*sem):
    return pltpu.CompilerParams(dimension_semantics=sem, vmem_limit_bytes=VMEM_LIMIT_BYTES)


def _silu_of_twice(h):
    return h + h * jnp.tanh(h)


def _silu(v):
    return _silu_of_twice(0.5 * v)


def _round_up(a, m):
    return -(-a // m) * m


def _mods_kernel(c_ref, w_ref, b_ref, o_ref):
    s = _silu(c_ref[...])
    o_ref[0] = jnp.dot(s, w_ref[0], preferred_element_type=F32, precision=HIGHEST) + b_ref[0]


def _mods(cvecs, ada_w, ada_b):
    depth, d, n = ada_w.shape
    tn = min(n, 1024)
    return pl.pallas_call(
        _mods_kernel,
        grid=(depth, n // tn),
        in_specs=[pl.BlockSpec((SUBLANES, d), lambda i, j: (0, 0)),
                  pl.BlockSpec((1, d, tn), lambda i, j: (i, 0, j)),
                  pl.BlockSpec((1, 1, tn), lambda i, j: (i, 0, j))],
        out_specs=pl.BlockSpec((1, SUBLANES, tn), lambda i, j: (i, 0, j)),
        out_shape=jax.ShapeDtypeStruct((depth, SUBLANES, n), F32),
        compiler_params=_cparams("parallel", "parallel"),
        name="mods",
    )(cvecs, ada_w, ada_b.reshape(depth, 1, n))


def _add_kernel(x_ref, p_ref, o_ref):
    o_ref[0] = x_ref[0] + p_ref[...]


def _add_pos(x, pos):
    b, l, d = x.shape
    tl = min(l, 1024)
    return pl.pallas_call(
        _add_kernel,
        grid=(l // tl, b),
        in_specs=[pl.BlockSpec((1, tl, d), lambda i, bb: (bb, i, 0)),
                  pl.BlockSpec((tl, d), lambda i, bb: (i, 0))],
        out_specs=pl.BlockSpec((1, tl, d), lambda i, bb: (bb, i, 0)),
        out_shape=jax.ShapeDtypeStruct(x.shape, F32),
        compiler_params=_cparams("parallel", "parallel"),
        name="add_pos",
    )(x, pos)


def _modnorm(x, g, shift, scale):
    ms = jnp.mean(x * x, axis=-1, keepdims=True)
    y = x * lax.rsqrt(ms + RMS_EPS) * g
    return y * (1.0 + scale) + shift


INPROJ_ROWS = 2048
CONV_ROWS = 256


def _modnorm_two_pass(x_ref, ms_scr, g, shift, scale):
    x = x_ref[0]
    ms_scr[...] = jnp.mean(x * x, axis=-1, keepdims=True)
    y = x_ref[0] * lax.rsqrt(ms_scr[...] + RMS_EPS) * g
    return y * (1.0 + scale) + shift


def _prenorm_kernel(prev_ref, x_ref, next_ref, g_ref, sh_ref, sc_ref, o_ref, ms_scr):
    k = pl.program_id(2)
    rows = x_ref.shape[1]
    tl = o_ref.shape[2] - 2 * HALO
    norm = lambda v: _modnorm(v, g_ref[...], sh_ref[0], sc_ref[0]).astype(o_ref.dtype)
    o_ref[0, 0, pl.ds(pl.multiple_of(HALO + k * rows, HALO), rows), :] = _modnorm_two_pass(
        x_ref, ms_scr, g_ref[...], sh_ref[0], sc_ref[0]).astype(o_ref.dtype)

    @pl.when(k == 0)
    def _():
        o_ref[0, 0, 0:HALO, :] = norm(prev_ref[0])

    @pl.when(k == pl.num_programs(2) - 1)
    def _():
        o_ref[0, 0, HALO + tl:, :] = norm(next_ref[0])


def _prenorm(x, g, shift, scale):
    b, l, d = x.shape
    tl = min(l, INPROJ_ROWS)
    rows = min(tl, 512)
    nk = tl // rows
    rpb = tl // HALO
    last = l // HALO - 1
    vec = pl.BlockSpec((1, 1, d), lambda bb, i, k: (bb, 0, 0))
    return pl.pallas_call(
        _prenorm_kernel,
        grid=(b, l // tl, nk),
        in_specs=[pl.BlockSpec((1, HALO, d), lambda bb, i, k: (bb, jnp.maximum(i * rpb - 1, 0), 0)),
                  pl.BlockSpec((1, rows, d), lambda bb, i, k: (bb, i * nk + k, 0)),
                  pl.BlockSpec((1, HALO, d), lambda bb, i, k: (bb, jnp.minimum((i + 1) * rpb, last), 0)),
                  pl.BlockSpec((1, d), lambda bb, i, k: (0, 0)), vec, vec],
        out_specs=pl.BlockSpec((1, 1, tl + 2 * HALO, d), lambda bb, i, k: (bb, i, 0, 0)),
        out_shape=jax.ShapeDtypeStruct((b, l // tl, tl + 2 * HALO, d), BF16),
        scratch_shapes=[pltpu.VMEM((rows, 1), F32)],
        compiler_params=_cparams("parallel", "parallel", "arbitrary"),
        name="prenorm",
    )(x, x, x, g.reshape(1, d), shift, scale)


def _inproj_kernel(h_ref, w_ref, cw_ref, cb_ref, xbc_ref, dt_ref, acc_scr):
    i = pl.program_id(1)
    k = pl.program_id(2)
    n_conv = pl.num_programs(2) - 1
    tm = dt_ref.shape[1]
    project = lambda: jnp.dot(h_ref[0, 0], w_ref[...], preferred_element_type=F32)

    def project_into(dst):
        dst[...] = project()
        dst[0:HALO, :] = jnp.where(i > 0, dst[0:HALO, :], 0.0)
        dst[HALO + tm:, :] = jnp.where(i < pl.num_programs(1) - 1, dst[HALO + tm:, :], 0.0)

    def conv_from(src):
        pad = M_CONV // 2
        rows = min(tm, CONV_ROWS)
        w_half = 0.5 * cw_ref[...]
        b_half = 0.5 * cb_ref[...]
        for r0 in range(0, tm, rows):
            ext = src[r0:r0 + rows + 2 * HALO, :]
            acc = None
            for tap in range(M_CONV):
                shifted = ext if tap == pad else pltpu.roll(ext, (pad - tap) % (rows + 2 * HALO), axis=0)
                term = shifted[HALO:HALO + rows] * w_half[tap:tap + 1, :]
                acc = term if acc is None else acc + term
            xbc_ref[0, r0:r0 + rows, :] = _silu_of_twice(acc + b_half).astype(xbc_ref.dtype)

    @pl.when(k < n_conv)
    def _():
        project_into(acc_scr)
        conv_from(acc_scr)

    @pl.when(k == n_conv)
    def _():
        dt_ref[0] = project()[HALO:HALO + tm, 0:LANES]


def _proj_kernel(h_ref, w_ref, o_ref):
    tm = o_ref.shape[1]
    acc = jnp.dot(h_ref[0, 0], w_ref[...], preferred_element_type=F32)
    o_ref[0] = acc[HALO:HALO + tm].astype(o_ref.dtype)


def _proj_plain(h, w, ncols):
    b, nt, tmh, d = h.shape
    tm = tmh - 2 * HALO
    tn = 2 * COL_TILE if ncols % (2 * COL_TILE) == 0 else COL_TILE
    return pl.pallas_call(
        _proj_kernel,
        grid=(b, nt, ncols // tn),
        in_specs=[pl.BlockSpec((1, 1, tmh, d), lambda bb, i, j: (bb, i, 0, 0)),
                  pl.BlockSpec((d, tn), lambda bb, i, j: (0, j))],
        out_specs=pl.BlockSpec((1, tm, tn), lambda bb, i, j: (bb, i, j)),
        out_shape=jax.ShapeDtypeStruct((b, nt * tm, ncols), BF16),
        compiler_params=_cparams("parallel", "parallel", "arbitrary"),
        name="proj_plain",
    )(h, w)


def _inproj(h, w, col0, conv_w, conv_b):
    b, nt, tmh, d = h.shape
    tm = tmh - 2 * HALO
    l = nt * tm
    tn = COL_TILE
    j0 = col0 // tn
    n_conv = conv_w.shape[1] // tn
    conv_idx = lambda j: jnp.minimum(j, n_conv - 1)
    return pl.pallas_call(
        _inproj_kernel,
        grid=(b, nt, n_conv + 1),
        in_specs=[pl.BlockSpec((1, 1, tmh, d), lambda bb, i, j: (bb, i, 0, 0)),
                  pl.BlockSpec((d, tn), lambda bb, i, j: (0, j + j0)),
                  pl.BlockSpec((M_CONV, tn), lambda bb, i, j: (0, conv_idx(j))),
                  pl.BlockSpec((1, tn), lambda bb, i, j: (0, conv_idx(j)))],
        out_specs=[pl.BlockSpec((1, tm, tn), lambda bb, i, j: (bb, i, conv_idx(j))),
                   pl.BlockSpec((1, tm, LANES), lambda bb, i, j: (bb, i, 0))],
        out_shape=[jax.ShapeDtypeStruct((b, l, n_conv * tn), BF16), jax.ShapeDtypeStruct((b, l, LANES), F32)],
        scratch_shapes=[pltpu.VMEM((tmh, tn), F32)],
        compiler_params=_cparams("parallel", "parallel", "arbitrary"),
        name="inproj",
    )(h, w, conv_w, conv_b.reshape(1, -1))


def _conv_taps(prev_ref, main_ref, next_ref, w_ref, b_ref, ktaps):
    i = pl.program_id(2)
    nblk = pl.num_programs(2)
    tl = main_ref.shape[1]
    pad = ktaps // 2
    prev = jnp.where(i > 0, prev_ref[0].astype(F32), 0.0)
    nxt = jnp.where(i < nblk - 1, next_ref[0].astype(F32), 0.0)
    ext = jnp.concatenate([prev, main_ref[0].astype(F32), nxt], axis=0)
    rows = tl + 2 * HALO
    acc = None
    for j in range(ktaps):
        shifted = ext if j == pad else pltpu.roll(ext, (pad - j) % rows, axis=0)
        term = shifted[HALO:HALO + tl] * w_ref[j:j + 1, :]
        acc = term if acc is None else acc + term
    return acc + b_ref[...]


def _conv_specs(tl, cw, l, col_block0):
    rpb = tl // HALO
    last = l // HALO - 1
    return [
        pl.BlockSpec((1, HALO, cw), lambda bb, c, i: (bb, jnp.maximum(i * rpb - 1, 0), c + col_block0)),
        pl.BlockSpec((1, tl, cw), lambda bb, c, i: (bb, i, c + col_block0)),
        pl.BlockSpec((1, HALO, cw), lambda bb, c, i: (bb, jnp.minimum((i + 1) * rpb, last), c + col_block0)),
    ]


def _hconv_kernel(p0, m0, n0, p1, m1, n1, p2, m2, n2, w0, b0, w1, b1, w2, b2, x0_ref, u_ref):
    x0_ref[0] = _conv_taps(p0, m0, n0, w0, b0, H_SHORT).astype(x0_ref.dtype)
    x1 = _conv_taps(p1, m1, n1, w1, b1, H_SHORT)
    v = _conv_taps(p2, m2, n2, w2, b2, H_SHORT)
    u_ref[0] = (x1 * v).astype(u_ref.dtype)


def _hconv(p, col0, w, bias, d):
    b, l, _ = p.shape
    tl = min(l, 1024)
    cw = min(d, COL_TILE)
    nb = d // cw
    specs, wspecs = [], []
    for s in range(3):
        specs += _conv_specs(tl, cw, l, col0 // cw + s * nb)
        wspecs += [pl.BlockSpec((H_SHORT, cw), functools.partial(lambda bb, c, i, s: (0, c + s * nb), s=s)),
                   pl.BlockSpec((1, cw), functools.partial(lambda bb, c, i, s: (0, c + s * nb), s=s))]
    bias2 = bias.reshape(1, 3 * d)
    out_spec = pl.BlockSpec((1, tl, cw), lambda bb, c, i: (bb, i, c))
    return pl.pallas_call(
        _hconv_kernel,
        grid=(b, nb, l // tl),
        in_specs=specs + wspecs,
        out_specs=[out_spec, out_spec],
        out_shape=[jax.ShapeDtypeStruct((b, l, d), BF16)] * 2,
        compiler_params=_cparams("parallel", "parallel", "arbitrary"),
        name="hconv",
    )(p, p, p, p, p, p, p, p, p, w, bias2, w, bias2, w, bias2)


def _ssd_kernel(*refs, reverse, with_output, epilogue, heads):
    if epilogue:
        (xbc_ref, dt_ref, dtb_ref, alog_ref, init_ref, sel_ref, yf_ref, z_ref, dsk_ref, ng_ref,
         y_ref, fin_ref, s_scr) = refs
    elif with_output:
        xbc_ref, dt_ref, dtb_ref, alog_ref, init_ref, sel_ref, y_ref, fin_ref, s_scr = refs
    else:
        xbc_ref, dt_ref, dtb_ref, alog_ref, init_ref, fin_ref, s_scr = refs
    t = M_CHUNK
    n = M_STATE
    assert n == t
    di = heads * M_HEADDIM
    gw = di // M_GROUPS
    rpg = heads // M_GROUPS
    lane0 = heads if reverse else 0
    edge = 0 if reverse else t - 1
    c = pl.program_id(1)

    nb = xbc_ref.shape[0]

    @pl.when(c == 0)
    def _():
        s_scr[...] = init_ref[...]

    row = lax.broadcasted_iota(jnp.int32, (t, t), 0)
    col = lax.broadcasted_iota(jnp.int32, (t, t), 1)
    mask = (row <= col) if reverse else (row >= col)
    lane = lax.broadcasted_iota(jnp.int32, (1, LANES), 1)
    lo = lane < M_HEADDIM
    mlo = jnp.where(lo, 1.0, 0.0).astype(BF16)
    mhi = jnp.where(lo, 0.0, 1.0).astype(BF16)

    def chunk_decays(bi):
        x = dt_ref[bi] + dtb_ref[...]
        dt = jnp.maximum(x, 0.0) + jnp.log1p(jnp.exp(-jnp.abs(x)))
        dta = dt * (-jnp.exp(alog_ref[...]))
        acs = jnp.dot(mask.astype(F32), dta, preferred_element_type=F32, precision=HIGHEST)
        acs_t = acs.T
        dt_t = dt.T
        acs2 = acs * LOG2_E
        arow_dt2 = (acs_t - jnp.log(dt_t)) * LOG2_E
        w_t = jnp.exp(acs_t[:, edge:edge + 1] - acs_t) * dt_t
        dec = jnp.exp(acs[edge:edge + 1, :])
        eacs = jnp.exp(acs).astype(BF16) if with_output else None
        return acs2, arow_dt2, w_t, dec, eacs

    def group(bi, g, acs2, arow_dt2, w_t, dec, eacs):
        bt = xbc_ref[bi, :, di + g * n:di + (g + 1) * n].astype(F32).T
        if with_output:
            cm = xbc_ref[bi, :, di + M_GROUPS * n + g * n:di + M_GROUPS * n + (g + 1) * n]
            cb = jnp.dot(cm, bt.astype(BF16), preferred_element_type=F32)
            y_off = jnp.dot(cm, s_scr[bi, g].astype(BF16), preferred_element_type=F32)
        y_parts = []
        for q in range(rpg // 2):
            c0 = g * gw + q * LANES
            xp = xbc_ref[bi, :, c0:c0 + LANES]
            sp = s_scr[bi, g, :, q * LANES:(q + 1) * LANES]
            x2 = jnp.concatenate([xp * mlo, xp * mhi], axis=0)
            hl0 = lane0 + g * rpg + 2 * q
            lhs, btw, decs = [], [], []
            for r in range(2):
                hl = hl0 + r
                if with_output:
                    acol = jnp.broadcast_to(acs2[:, hl:hl + 1], (t, t))
                    dk = jnp.exp2(jnp.where(mask, acol - arow_dt2[hl:hl + 1, :], -jnp.inf))
                    lhs.append((cb * dk).astype(BF16))
                btw.append((bt * w_t[hl:hl + 1, :]).astype(BF16))
                decs.append(jnp.broadcast_to(dec[:, hl:hl + 1], (1, LANES)))
            if with_output:
                e_pair = jnp.dot(eacs, sel_ref[hl0 // 2], preferred_element_type=F32)
                y_diag = jnp.dot(jnp.concatenate(lhs, axis=1), x2, preferred_element_type=F32)
                y_parts.append(y_diag + e_pair * y_off[:, q * LANES:(q + 1) * LANES])
            ds = jnp.dot(jnp.concatenate(btw, axis=1), x2, preferred_element_type=F32)
            s_scr[bi, g, :, q * LANES:(q + 1) * LANES] = jnp.where(lo, decs[0], decs[1]) * sp + ds
        if with_output:
            yg = jnp.concatenate(y_parts, axis=1) if len(y_parts) > 1 else y_parts[0]
            sl = slice(g * gw, (g + 1) * gw)
            if epilogue:
                yg = yg + yf_ref[bi, :, sl].astype(F32)
                v = (yg + xbc_ref[bi, :, sl].astype(F32) * dsk_ref[:, sl]) * _silu(z_ref[bi, :, sl].astype(F32))
                ms = jnp.mean(v * v, axis=-1, keepdims=True)
                y_ref[bi, :, sl] = (v * lax.rsqrt(ms + RMS_EPS) * ng_ref[:, sl]).astype(y_ref.dtype)
            else:
                y_ref[bi, :, sl] = yg.astype(y_ref.dtype)

    decays = [chunk_decays(bi) for bi in range(nb)]
    for g in range(M_GROUPS):
        for bi in range(nb):
            group(bi, g, *decays[bi])

    @pl.when(c == pl.num_programs(1) - 1)
    def _():
        fin_ref[...] = s_scr[...]


def _pair_selectors():
    p = jnp.arange(LANES // 2)[:, None, None]
    h = jnp.arange(LANES)[None, :, None]
    lane = jnp.arange(LANES)[None, None, :]
    return (h == 2 * p + (lane >= M_HEADDIM)).astype(BF16)


def _ssd(xbc, dtp, dtb, alog, init, heads, reverse, with_output, extra=None):
    b, l, xw = xbc.shape
    t = M_CHUNK
    nc = l // t
    di = heads * M_HEADDIM
    gw = di // M_GROUPS
    epilogue = extra is not None
    cidx = (lambda c: nc - 1 - c) if reverse else (lambda c: c)
    nb = SSD_BATCH_ROWS if b % SSD_BATCH_ROWS == 0 else 1
    state_spec = pl.BlockSpec((nb, M_GROUPS, M_STATE, gw), lambda bb, c: (bb, 0, 0, 0))
    in_specs = [pl.BlockSpec((nb, t, xw), lambda bb, c: (bb, cidx(c), 0)),
                pl.BlockSpec((nb, t, LANES), lambda bb, c: (bb, cidx(c), 0)),
                pl.BlockSpec((1, LANES), lambda bb, c: (0, 0)),
                pl.BlockSpec((1, LANES), lambda bb, c: (0, 0)),
                state_spec]
    args = [xbc, dtp, dtb, alog, init]
    out_specs, out_shape = [], []
    if with_output:
        in_specs.append(pl.BlockSpec((LANES // 2, LANES, LANES), lambda bb, c: (0, 0, 0)))
        args.append(_pair_selectors())
    if epilogue:
        yf, p, z_col0, dsk, ng = extra
        in_specs += [pl.BlockSpec((nb, t, di), lambda bb, c: (bb, cidx(c), 0)),
                     pl.BlockSpec((nb, t, di), lambda bb, c: (bb, cidx(c), z_col0 // di)),
                     pl.BlockSpec((1, di), lambda bb, c: (0, 0)),
                     pl.BlockSpec((1, di), lambda bb, c: (0, 0))]
        args += [yf, p, dsk, ng]
    if with_output:
        out_specs.append(pl.BlockSpec((nb, t, di), lambda bb, c: (bb, cidx(c), 0)))
        out_shape.append(jax.ShapeDtypeStruct((b, l, di), BF16))
    out_specs.append(state_spec)
    out_shape.append(jax.ShapeDtypeStruct((b, M_GROUPS, M_STATE, gw), F32))
    outs = pl.pallas_call(
        functools.partial(_ssd_kernel, reverse=reverse, with_output=with_output, epilogue=epilogue, heads=heads),
        grid=(b // nb, nc),
        in_specs=in_specs,
        out_specs=out_specs,
        out_shape=out_shape,
        scratch_shapes=[pltpu.VMEM((nb, M_GROUPS, M_STATE, gw), F32)],
        compiler_params=_cparams("parallel", "arbitrary"),
        name="ssd_bwd" if reverse else "ssd_fwd",
    )(*args)
    return (outs[0], outs[1]) if with_output else (None, outs[0])


def _filt_kernel(f_ref, w1_ref, b1_ref, w2_ref, b2_ref, fr_ref, w3_ref, dl_ref, o_ref, ss_ref):
    f = f_ref[...]
    fr = fr_ref[...]
    h = jnp.sin(fr * (jnp.dot(f, w1_ref[...], preferred_element_type=F32, precision=HIGHEST) + b1_ref[...]))
    h = jnp.sin(fr * (jnp.dot(h, w2_ref[...], preferred_element_type=F32, precision=HIGHEST) + b2_ref[...]))
    filt = jnp.dot(h, w3_ref[...], preferred_element_type=F32, precision=HIGHEST)
    tcol = f[:, H_EMB:H_EMB + 1]
    mcol = f[:, H_EMB + 1:H_EMB + 2]
    out = filt * jnp.exp(-tcol * dl_ref[...]) * mcol
    o_ref[...] = out

    @pl.when(pl.program_id(1) == 0)
    def _():
        ss_ref[...] = jnp.zeros_like(ss_ref)

    ss_ref[...] += jnp.sum(out * out, axis=0, keepdims=True)


def _filter_features(l):
    t = jnp.linspace(0.0, 1.0, l, dtype=F32)[:, None]
    ang = (2.0 * math.pi / l) * jnp.arange(l, dtype=F32)[:, None]
    nb = (H_EMB - 1) // 2
    bands = jnp.linspace(1e-4, nb - 1, nb, dtype=F32)[None, :]
    feats = jnp.concatenate([t, jnp.cos(bands * ang), -jnp.sin(bands * ang), t, jnp.ones_like(t)], axis=-1)
    rev = jnp.concatenate([feats[0:1] * 0.0, jnp.flip(feats[1:], axis=0)], axis=0)
    full = jnp.concatenate([feats, rev], axis=0)
    return jnp.pad(full, ((0, 0), (0, FEAT_COLS - full.shape[1])))


def _hyena_filter(l, w1, b1, w2, b2, freq, w3):
    hid = w2.shape[0]
    d = w3.shape[1] // 2
    feats = _filter_features(l)
    w1p = jnp.pad(w1, ((0, FEAT_COLS - w1.shape[0]), (0, 0)))
    deltas = jnp.abs(jnp.linspace(math.log(H_DECAY_TARGET) / H_SLOW_PCT, math.log(H_DECAY_TARGET) / H_FAST_PCT,
                                  d, dtype=F32))[None, :]
    tl = min(l, 512)
    cw = d
    nb = d // cw
    nrow = l // tl
    small = lambda shape: pl.BlockSpec(shape, lambda c, i: (0, 0))
    return pl.pallas_call(
        _filt_kernel,
        grid=(nb, 2 * nrow),
        in_specs=[pl.BlockSpec((tl, FEAT_COLS), lambda c, i: (i, 0)),
                  small((FEAT_COLS, hid)), small((1, hid)), small((hid, hid)), small((1, hid)), small((1, hid)),
                  pl.BlockSpec((hid, cw), lambda c, i: (0, c + (i // nrow) * nb)),
                  pl.BlockSpec((1, cw), lambda c, i: (0, c))],
        out_specs=[pl.BlockSpec((tl, cw), lambda c, i: (i, c)),
                   pl.BlockSpec((1, cw), lambda c, i: (0, c))],
        out_shape=[jax.ShapeDtypeStruct((2 * l, d), F32), jax.ShapeDtypeStruct((1, d), F32)],
        compiler_params=_cparams("parallel", "arbitrary"),
        name="hyena_filter",
    )(feats, w1p, b1.reshape(1, hid), w2, b2.reshape(1, hid), freq.reshape(1, hid), w3, deltas)


def _fft_split(n):
    n1 = 1 << ((n.bit_length() - 1 + 1) // 2)
    return n1, n // n1


def _dft_consts(n1, n2):
    n = n1 * n2
    h = n1 // 2
    k1 = jnp.arange(n1, dtype=jnp.int32)
    t1 = jnp.arange(n1, dtype=jnp.int32)
    t2 = jnp.arange(n2, dtype=jnp.int32)
    idx = (k1[None, :, None] * (n2 * t1[None, None, :] + t2[:, None, None])) % n
    ang = idx.astype(F32) * (2.0 * math.pi / n)
    cs, sn = jnp.cos(ang), jnp.sin(ang)
    ch, sh = cs[..., :h], sn[..., :h]
    t_data = jnp.concatenate([jnp.concatenate([ch, sh], -1), jnp.concatenate([-sh, ch], -1)], axis=1)
    t_filt = jnp.concatenate([cs, -sn], axis=1)
    ct, st = jnp.swapaxes(ch, 1, 2), jnp.swapaxes(sh, 1, 2)
    t_inv = jnp.concatenate([jnp.concatenate([ct, -st], -1), jnp.concatenate([st, ct], -1)], axis=1) / n
    k2 = jnp.arange(n2, dtype=jnp.int32)
    ang2 = ((k2[:, None] * k2[None, :]) % n2).astype(F32) * (2.0 * math.pi / n2)
    c2, s2 = jnp.cos(ang2), jnp.sin(ang2)
    m_fwd = jnp.concatenate([jnp.concatenate([c2, s2], -1), jnp.concatenate([-s2, c2], -1)], axis=0)
    m_inv = jnp.concatenate([jnp.concatenate([c2, -s2], -1), jnp.concatenate([s2, c2], -1)], axis=0)
    return dict(t_data=t_data.astype(BF16), t_filt=t_filt.astype(BF16), t_inv=t_inv.astype(BF16),
                m_fwd=m_fwd.astype(BF16), m_inv=m_inv.astype(BF16))


def _fft1_kernel(x_ref, t_ref, o_ref):
    a = x_ref.shape[1]
    xs = [pltpu.einshape("rjc->jrc", x_ref[0, s].astype(F32)) for s in range(a)]
    n1 = o_ref.shape[2]
    for j in range(SUBLANES):
        x = jnp.concatenate([xs[s][j] for s in range(a)], axis=0).astype(BF16)
        o = jnp.dot(t_ref[j], x, preferred_element_type=F32)
        o_ref[0, j] = _pack_complex(o[:n1], o[n1:])


def _pack_complex(re, im):
    r = lax.bitcast_convert_type(re.astype(BF16).astype(F32), jnp.uint32)
    i = lax.bitcast_convert_type(im.astype(BF16).astype(F32), jnp.uint32)
    return (r >> 16) | i


def _unpack_complex(w):
    re = lax.bitcast_convert_type(w << 16, F32)
    im = lax.bitcast_convert_type(w & jnp.uint32(0xFFFF0000), F32)
    return re, im


def _fft_stage1(xv, tmat, n1, n2, d):
    p, a, r, _, _ = xv.shape
    cw = min(d, 2 * COL_TILE)
    return pl.pallas_call(
        _fft1_kernel,
        grid=(n2 // SUBLANES, d // cw, p),
        in_specs=[pl.BlockSpec((1, a, r, SUBLANES, cw), lambda j, c, pp: (pp, 0, 0, j, c)),
                  pl.BlockSpec((SUBLANES, 2 * n1, n1), lambda j, c, pp: (j, 0, 0))],
        out_specs=pl.BlockSpec((1, SUBLANES, n1, cw), lambda j, c, pp: (pp, j, 0, c)),
        out_shape=jax.ShapeDtypeStruct((p, n2, n1, d), jnp.uint32),
        compiler_params=_cparams("parallel", "parallel", "parallel"),
        name="fft_stage1",
    )(xv, tmat)


def _fft2_kernel(a_ref, af_ref, ss_ref, mf_ref, mi_ref, o_ref, h_scr):
    n2 = a_ref.shape[1]

    def stage(w):
        re, im = _unpack_complex(w)
        return jnp.dot(mf_ref[...], jnp.concatenate([re, im], axis=0).astype(BF16), preferred_element_type=F32)

    @pl.when(pl.program_id(2) == 0)
    def _():
        af = pltpu.einshape("tkc->ktc", af_ref[0])
        for k in range(SUBLANES):
            h_scr[k] = stage(af[k]) * lax.rsqrt(ss_ref[...] + RMS_EPS)

    cw = a_ref.shape[3]
    half = max(cw // 2, LANES)
    for c0 in range(0, cw, half):
        a = pltpu.einshape("tkc->ktc", a_ref[0, :, :, c0:c0 + half])
        outs = []
        for k in range(SUBLANES):
            xk = stage(a[k])
            xr, xi = xk[:n2], xk[n2:]
            hr, hi = h_scr[k, 0:n2, c0:c0 + half], h_scr[k, n2:, c0:c0 + half]
            z = jnp.concatenate([xr * hr - xi * hi, xr * hi + xi * hr], axis=0).astype(BF16)
            bk = jnp.dot(mi_ref[...], z, preferred_element_type=F32)
            outs.append(_pack_complex(bk[:n2], bk[n2:]))
        o_ref[0, :, :, c0:c0 + half] = pltpu.einshape("ktc->tkc", jnp.stack(outs, axis=0))


def _fft_stage2(av, afv, ss, m_fwd, m_inv, n1, n2, d):
    p = av.shape[0]
    cw = min(d, COL_TILE)
    kb = SUBLANES
    return pl.pallas_call(
        _fft2_kernel,
        grid=(n1 // kb, d // cw, p),
        in_specs=[pl.BlockSpec((1, n2, kb, cw), lambda k, c, pp: (pp, 0, k, c)),
                  pl.BlockSpec((1, n2, kb, cw), lambda k, c, pp: (0, 0, k, c)),
                  pl.BlockSpec((1, cw), lambda k, c, pp: (0, c)),
                  pl.BlockSpec((2 * n2, 2 * n2), lambda k, c, pp: (0, 0)),
                  pl.BlockSpec((2 * n2, 2 * n2), lambda k, c, pp: (0, 0))],
        out_specs=pl.BlockSpec((1, n2, kb, cw), lambda k, c, pp: (pp, 0, k, c)),
        out_shape=jax.ShapeDtypeStruct(av.shape, jnp.uint32),
        scratch_shapes=[pltpu.VMEM((kb, 2 * n2, cw), F32)],
        compiler_params=_cparams("parallel", "parallel", "arbitrary"),
        name="fft_stage2",
    )(av, afv, ss, m_fwd, m_inv)


def _fft3_kernel(b_ref, t_ref, u_ref, x0_ref, bias_ref, o_ref):
    n1 = b_ref.shape[2]
    h = n1 // 2
    ys = []
    for j in range(SUBLANES):
        re, im = _unpack_complex(b_ref[0, j])
        bv = jnp.concatenate([re, im], axis=0).astype(BF16)
        ys.append(jnp.dot(t_ref[j], bv, preferred_element_type=F32))
    y = pltpu.einshape("jtc->tjc", jnp.stack(ys, axis=0))
    for s in range(2):
        conv = y[s * h:(s + 1) * h] + u_ref[0, s].astype(F32) * bias_ref[...]
        o_ref[0, s] = (x0_ref[0, s].astype(F32) * conv).astype(o_ref.dtype)


def _fft_stage3(bv, t_inv, uv, x0v, bias, n1, n2, d):
    p = bv.shape[0]
    h = n1 // 2
    cw = min(d, 2 * COL_TILE)
    io_spec = pl.BlockSpec((1, 2, h, SUBLANES, cw), lambda j, c, pp: (pp, 0, 0, j, c))
    return pl.pallas_call(
        _fft3_kernel,
        grid=(n2 // SUBLANES, d // cw, p),
        in_specs=[pl.BlockSpec((1, SUBLANES, n1, cw), lambda j, c, pp: (pp, j, 0, c)),
                  pl.BlockSpec((SUBLANES, n1, 2 * n1), lambda j, c, pp: (j, 0, 0)),
                  io_spec, io_spec,
                  pl.BlockSpec((1, cw), lambda j, c, pp: (0, c))],
        out_specs=io_spec,
        out_shape=jax.ShapeDtypeStruct(uv.shape, BF16),
        compiler_params=_cparams("parallel", "parallel", "parallel"),
        name="fft_stage3",
    )(bv, t_inv, uv, x0v, bias)


def _filter_spectrum(l, consts, lp):
    n1, n2 = _fft_split(2 * l)
    d = lp['h_bias'].shape[0]
    full, ss = _hyena_filter(l, lp['hf_w1'], lp['hf_b1'], lp['hf_w2'], lp['hf_b2'], lp['hf_freq'], lp['hf_w3'])
    return _fft_stage1(full.reshape(1, 1, n1, n2, d), consts['t_filt'], n1, n2, d), ss


def _long_conv(u, x0, hf, consts, bias):
    b, l, d = u.shape
    n1, n2 = _fft_split(2 * l)
    p = b // 2
    h = n1 // 2
    uv = u.reshape(p, 2, h, n2, d)
    a = _fft_stage1(uv, consts['t_data'], n1, n2, d)
    bv = _fft_stage2(a, hf[0], hf[1], consts['m_fwd'], consts['m_inv'], n1, n2, d)
    y = _fft_stage3(bv, consts['t_inv'], uv, x0.reshape(p, 2, h, n2, d), bias.reshape(1, d), n1, n2, d)
    return y.reshape(b, l, d)


def _merge_kernel(ym_ref, yh_ref, ga_ref, gb_ref, wm_ref, wh_ref, o_ref):
    a = jnp.dot(ym_ref[0], wm_ref[...], preferred_element_type=F32)
    bb = jnp.dot(yh_ref[0], wh_ref[...], preferred_element_type=F32)
    ga = jax.nn.sigmoid(ga_ref[0].astype(F32))
    gb = jax.nn.sigmoid(gb_ref[0].astype(F32))
    o_ref[0] = (ga * a + gb * bb).astype(o_ref.dtype)


def _merge(ym, yh, p, gate_col0, wm, wh):
    b, l, di = ym.shape
    d = yh.shape[2]
    tm = min(l, 1024)
    tn = min(d, COL_TILE)
    g0 = gate_col0 // tn
    nb = d // tn
    return pl.pallas_call(
        _merge_kernel,
        grid=(b, l // tm, nb),
        in_specs=[pl.BlockSpec((1, tm, di), lambda bb, i, j: (bb, i, 0)),
                  pl.BlockSpec((1, tm, d), lambda bb, i, j: (bb, i, 0)),
                  pl.BlockSpec((1, tm, tn), lambda bb, i, j: (bb, i, g0 + j)),
                  pl.BlockSpec((1, tm, tn), lambda bb, i, j: (bb, i, g0 + nb + j)),
                  pl.BlockSpec((di, tn), lambda bb, i, j: (0, j)),
                  pl.BlockSpec((d, tn), lambda bb, i, j: (0, j))],
        out_specs=pl.BlockSpec((1, tm, tn), lambda bb, i, j: (bb, i, j)),
        out_shape=jax.ShapeDtypeStruct((b, l, d), BF16),
        compiler_params=_cparams("parallel", "parallel", "arbitrary"),
        name="merge",
    )(ym, yh, p, p, wm, wh)


def _resid_mm_kernel(a_ref, w_ref, x_ref, g_ref, o_ref):
    o_ref[0] = x_ref[0] + g_ref[0] * jnp.dot(a_ref[0], w_ref[...], preferred_element_type=F32)


def _resid_mm(a, w, x, gate):
    b, l, k = a.shape
    d = w.shape[1]
    tm = min(l, 1024)
    tn = min(d, 2 * COL_TILE)
    return pl.pallas_call(
        _resid_mm_kernel,
        grid=(b, l // tm, d // tn),
        in_specs=[pl.BlockSpec((1, tm, k), lambda bb, i, j: (bb, i, 0)),
                  pl.BlockSpec((k, tn), lambda bb, i, j: (0, j)),
                  pl.BlockSpec((1, tm, tn), lambda bb, i, j: (bb, i, j)),
                  pl.BlockSpec((1, 1, tn), lambda bb, i, j: (bb, 0, j))],
        out_specs=pl.BlockSpec((1, tm, tn), lambda bb, i, j: (bb, i, j)),
        out_shape=jax.ShapeDtypeStruct((b, l, d), F32),
        compiler_params=_cparams("parallel", "parallel", "arbitrary"),
        name="resid_mm",
    )(a, w, x, gate)


def _ffn_kernel(x_ref, g_ref, sh_ref, sc_ref, gt_ref, wg_ref, wu_ref, wd_ref, fg_ref, o_ref, h_scr, ms_scr, *,
                final_norm):
    j = pl.program_id(2)

    @pl.when(j == 0)
    def _():
        h_scr[...] = _modnorm_two_pass(x_ref, ms_scr, g_ref[...], sh_ref[0], sc_ref[0]).astype(BF16)
        o_ref[...] = jnp.zeros_like(o_ref)

    tm = h_scr.shape[0]
    rows_per_pass = min(tm, 512)
    for r0 in range(0, tm, rows_per_pass):
        h = h_scr[r0:r0 + rows_per_pass, :]
        gg = jnp.dot(h, wg_ref[...], preferred_element_type=F32)
        uu = jnp.dot(h, wu_ref[...], preferred_element_type=F32)
        o_ref[0, r0:r0 + rows_per_pass, :] += jnp.dot((_silu(gg) * uu).astype(BF16), wd_ref[...],
                                                      preferred_element_type=F32)

    @pl.when(j == pl.num_programs(2) - 1)
    def _():
        y = x_ref[0] + gt_ref[0] * o_ref[0]
        if final_norm:
            ms = jnp.mean(y * y, axis=-1, keepdims=True)
            y = y * lax.rsqrt(ms + RMS_EPS) * fg_ref[...]
        o_ref[0] = y


def _ffn(x, g, shift, scale, gate, w_gu, w_down, final_g=None):
    b, l, d = x.shape
    f = w_down.shape[0]
    tm = min(l, 1024)
    fg = jnp.ones((1, d), F32) if final_g is None else final_g.reshape(1, d)
    tf = 256
    nf = f // tf
    vec = pl.BlockSpec((1, 1, d), lambda bb, i, j: (bb, 0, 0))
    return pl.pallas_call(
        functools.partial(_ffn_kernel, final_norm=final_g is not None),
        grid=(b, l // tm, nf),
        in_specs=[pl.BlockSpec((1, tm, d), lambda bb, i, j: (bb, i, 0)),
                  pl.BlockSpec((1, d), lambda bb, i, j: (0, 0)),
                  vec, vec, vec,
                  pl.BlockSpec((d, tf), lambda bb, i, j: (0, j)),
                  pl.BlockSpec((d, tf), lambda bb, i, j: (0, j + nf)),
                  pl.BlockSpec((tf, d), lambda bb, i, j: (j, 0)),
                  pl.BlockSpec((1, d), lambda bb, i, j: (0, 0))],
        out_specs=pl.BlockSpec((1, tm, d), lambda bb, i, j: (bb, i, 0)),
        out_shape=jax.ShapeDtypeStruct((b, l, d), F32),
        scratch_shapes=[pltpu.VMEM((tm, d), BF16), pltpu.VMEM((tm, 1), F32)],
        compiler_params=_cparams("parallel", "parallel", "arbitrary"),
        name="ffn",
    )(x, g.reshape(1, d), shift, scale, gate, w_gu, w_gu, w_down, fg)


def _grid_pos_embed(rows, d):
    r, col = jnp.meshgrid(jnp.arange(rows), jnp.arange(GRID_W), indexing='ij')
    quarter = d // 4
    omega = 1.0 / (POS_BASE ** (jnp.arange(quarter, dtype=F32) / quarter))

    def axis_embed(pos):
        ang = pos.reshape(-1)[:, None].astype(F32) * omega[None, :]
        return jnp.concatenate([jnp.sin(ang), jnp.cos(ang)], axis=-1)

    return jnp.concatenate([axis_embed(r), axis_embed(col)], axis=-1).astype(F32)


class _Cols:
    def __init__(self, d):
        self.d = d
        self.di = 2 * d
        self.heads = self.di // M_HEADDIM
        self.xbc = self.di + 2 * M_GROUPS * M_STATE
        self.z0 = 0
        self.gate0 = self.z0 + self.di
        self.hy0 = self.gate0 + 2 * d
        self.xbc0 = self.hy0 + 3 * d
        self.dt0 = self.xbc0 + self.xbc
        self.total = self.dt0 + COL_TILE
        assert 2 * self.heads <= LANES and self.heads % (2 * M_GROUPS) == 0
        for off in (self.gate0, self.hy0, self.xbc0, self.dt0):
            assert off % COL_TILE == 0


def _pack_w_in(w_in, cols):
    o_dt = cols.xbc
    o_z = o_dt + 2 * cols.heads
    o_hy = o_z + cols.di
    o_gate = o_hy + 3 * cols.d
    parts = [w_in[:, o_z:o_hy], w_in[:, o_gate:], w_in[:, o_hy:o_gate], w_in[:, :o_dt], w_in[:, o_dt:o_z]]
    w = jnp.concatenate(parts, axis=1)
    return jnp.pad(w, ((0, 0), (0, cols.total - w.shape[1]))).astype(BF16)


def _pad_lanes(v):
    v = v.reshape(1, -1)
    return jnp.pad(v, ((0, 0), (0, LANES - v.shape[1])))


def _mix_sublayer(x, mods, lp, cols, hf, consts, init_f, init_b):
    h = _prenorm(x, lp['norm1_g'], mods[0], mods[1])
    p = _proj_plain(h, lp['w_in'], cols.xbc0)
    xbc, dtp = _inproj(h, lp['w_in'], cols.xbc0, lp['m_conv_w'], lp['m_conv_b'])
    y_f, fin_f = _ssd(xbc, dtp, lp['dtb'], lp['alog'], init_f, cols.heads, False, True)
    ym, fin_b = _ssd(xbc, dtp, lp['dtb'], lp['alog'], init_b, cols.heads, True, True,
                     extra=(y_f, p, cols.z0, lp['dskip'], lp['m_norm_g']))
    x0, u = _hconv(p, cols.hy0, lp['h_conv_w'], lp['h_conv_b'], cols.d)
    yh = _long_conv(u, x0, hf, consts, lp['h_bias'])
    merged = _merge(ym, yh, p, cols.gate0, lp['m_w_out'], lp['h_w_out'])
    return _resid_mm(merged, lp['w_merge_out'], x, mods[2]), fin_f, fin_b


def _ctx_final_states(x, mods, lp, cols, init_f, init_b):
    h = _prenorm(x, lp['norm1_g'], mods[0], mods[1])
    xbc, dtp = _inproj(h, lp['w_in'], cols.xbc0, lp['m_conv_w'], lp['m_conv_b'])
    _, fin_f = _ssd(xbc, dtp, lp['dtb'], lp['alog'], init_f, cols.heads, False, False)
    _, fin_b = _ssd(xbc, dtp, lp['dtb'], lp['alog'], init_b, cols.heads, True, False)
    return fin_f, fin_b


def kernel(x, c, ctx, c_ctx, ada_w, ada_b, norm1_g, w_in, m_conv_w, m_conv_b, m_dt_bias, m_a_log, m_d, m_norm_g,
           m_w_out, h_conv_w, h_conv_b, hf_w1, hf_b1, hf_w2, hf_b2, hf_freq, hf_w3, h_bias, h_w_out, w_merge_out,
           norm2_g, ffn_w_gu, ffn_w_down, final_g):
    b, l, d = x.shape
    lc = ctx.shape[1]
    depth = ada_w.shape[0]
    cols = _Cols(d)
    assert b % 2 == 0 and b + 1 <= SUBLANES and l % M_CHUNK == 0 and lc % M_CHUNK == 0

    cvecs = jnp.concatenate([c, c_ctx[None], jnp.zeros((SUBLANES - b - 1, d), F32)], axis=0)
    mods_all = _mods(cvecs, ada_w, ada_b)
    consts_l = _dft_consts(*_fft_split(2 * l))
    consts_c = _dft_consts(*_fft_split(2 * lc))

    x_l = _add_pos(x, _grid_pos_embed(l // GRID_W, d))
    x_c = ctx
    zero_state = jnp.zeros((b, M_GROUPS, M_STATE, cols.di // M_GROUPS), F32)
    for i in range(depth):
        lp = dict(norm1_g=norm1_g[i], w_in=_pack_w_in(w_in[i], cols), m_conv_w=m_conv_w[i], m_conv_b=m_conv_b[i],
                  dtb=_pad_lanes(m_dt_bias[i]), alog=_pad_lanes(m_a_log[i]),
                  dskip=jnp.repeat(m_d[i], M_HEADDIM).reshape(1, cols.di), m_norm_g=m_norm_g[i].reshape(1, cols.di),
                  m_w_out=m_w_out[i].astype(BF16), h_conv_w=h_conv_w[i], h_conv_b=h_conv_b[i],
                  hf_w1=hf_w1[i], hf_b1=hf_b1[i], hf_w2=hf_w2[i], hf_b2=hf_b2[i], hf_freq=hf_freq[i], hf_w3=hf_w3[i],
                  h_bias=h_bias[i], h_w_out=h_w_out[i].astype(BF16), w_merge_out=w_merge_out[i].astype(BF16),
                  norm2_g=norm2_g[i], ffn_w_gu=ffn_w_gu[i].astype(BF16), ffn_w_down=ffn_w_down[i].astype(BF16))
        m = mods_all[i].reshape(SUBLANES, ADA_CHUNKS, d)
        mods_l = [m[:b, k][:, None, :] for k in range(ADA_CHUNKS)]
        mods_c = [jnp.broadcast_to(m[b, k][None, None, :], (b, 1, d)) for k in range(ADA_CHUNKS)]
        if i < depth - 1:
            hf_c = _filter_spectrum(lc, consts_c, lp)
            x_c_mixed, fin_f, fin_b = _mix_sublayer(x_c, mods_c, lp, cols, hf_c, consts_c, zero_state, zero_state)
        else:
            fin_f, fin_b = _ctx_final_states(x_c, mods_c, lp, cols, zero_state, zero_state)
        hf_l = _filter_spectrum(l, consts_l, lp)
        x_l, _, _ = _mix_sublayer(x_l, mods_l, lp, cols, hf_l, consts_l, fin_f, fin_b)
        x_l = _ffn(x_l, lp['norm2_g'], mods_l[3], mods_l[4], mods_l[5], lp['ffn_w_gu'], lp['ffn_w_down'],
                   final_g=final_g if i == depth - 1 else None)
        if i < depth - 1:
            x_c = _ffn(x_c_mixed, lp['norm2_g'], mods_c[3], mods_c[4], mods_c[5], lp['ffn_w_gu'], lp['ffn_w_down'])
    return x_l
```

```python
import functools
import math

import jax
import jax.numpy as jnp
from jax import lax
from jax.experimental import pallas as pl
from jax.experimental.pallas import tpu as pltpu

F32 = jnp.float32
BF16 = jnp.bfloat16
HIGHEST = lax.Precision.HIGHEST

RMS_EPS = 1e-6
LOG2_E = 1.4426950408889634
GRID_W = 64
POS_BASE = 10000.0
ADA_CHUNKS = 6
M_HEADDIM = 64
M_GROUPS = 8
M_STATE = 128
M_CONV = 5
M_CHUNK = 128
H_SHORT = 3
H_EMB = 33
H_DECAY_TARGET = 1e-2
H_FAST_PCT = 0.3
H_SLOW_PCT = 1.5

LANES = 128
SUBLANES = 8
HALO = 16
VMEM_LIMIT_BYTES = 56 * 1024 * 1024

COL_TILE = 512
FEAT_COLS = 40
SSD_BATCH_ROWS = 2


def _cparams(*sem):
    return pltpu.CompilerParams(dimension_semantics=sem, vmem_limit_bytes=VMEM_LIMIT_BYTES)


def _silu_of_twice(h):
    return h + h * jnp.tanh(h)


def _silu(v):
    return _silu_of_twice(0.5 * v)


def _round_up(a, m):
    return -(-a // m) * m


def _dot_split(a, b):
    a_hi = a.astype(BF16)
    a_lo = (a - a_hi.astype(F32)).astype(BF16)
    b_hi = b.astype(BF16)
    b_lo = (b - b_hi.astype(F32)).astype(BF16)
    dot = functools.partial(jnp.dot, preferred_element_type=F32)
    return dot(a_hi, b_hi) + (dot(a_hi, b_lo) + dot(a_lo, b_hi))


def _mods_kernel(c_ref, w_ref, b_ref, o_ref):
    s = _silu(c_ref[...])
    o_ref[0] = _dot_split(s, w_ref[0]) + b_ref[0]


def _mods(cvecs, ada_w, ada_b):
    depth, d, n = ada_w.shape
    tn = min(n, 1024)
    return pl.pallas_call(
        _mods_kernel,
        grid=(depth, n // tn),
        in_specs=[pl.BlockSpec((SUBLANES, d), lambda i, j: (0, 0)),
                  pl.BlockSpec((1, d, tn), lambda i, j: (i, 0, j)),
                  pl.BlockSpec((1, 1, tn), lambda i, j: (i, 0, j))],
        out_specs=pl.BlockSpec((1, SUBLANES, tn), lambda i, j: (i, 0, j)),
        out_shape=jax.ShapeDtypeStruct((depth, SUBLANES, n), F32),
        compiler_params=_cparams("parallel", "parallel"),
        name="mods",
    )(cvecs, ada_w, ada_b.reshape(depth, 1, n))


def _add_kernel(x_ref, p_ref, o_ref):
    o_ref[0] = x_ref[0] + p_ref[...]


def _add_pos(x, pos):
    b, l, d = x.shape
    tl = min(l, 1024)
    return pl.pallas_call(
        _add_kernel,
        grid=(l // tl, b),
        in_specs=[pl.BlockSpec((1, tl, d), lambda i, bb: (bb, i, 0)),
                  pl.BlockSpec((tl, d), lambda i, bb: (i, 0))],
        out_specs=pl.BlockSpec((1, tl, d), lambda i, bb: (bb, i, 0)),
        out_shape=jax.ShapeDtypeStruct(x.shape, F32),
        compiler_params=_cparams("parallel", "parallel"),
        name="add_pos",
    )(x, pos)


def _modnorm(x, g, shift, scale):
    ms = jnp.mean(x * x, axis=-1, keepdims=True)
    y = x * lax.rsqrt(ms + RMS_EPS) * g
    return y * (1.0 + scale) + shift


INPROJ_ROWS = 2048
CONV_ROWS = 256


def _modnorm_two_pass(x_ref, ms_scr, g, shift, scale):
    x = x_ref[0]
    ms_scr[...] = jnp.mean(x * x, axis=-1, keepdims=True)
    y = x_ref[0] * lax.rsqrt(ms_scr[...] + RMS_EPS) * g
    return y * (1.0 + scale) + shift


def _prenorm_kernel(prev_ref, x_ref, next_ref, g_ref, sh_ref, sc_ref, o_ref, ms_scr):
    k = pl.program_id(2)
    rows = x_ref.shape[1]
    tl = o_ref.shape[2] - 2 * HALO
    norm = lambda v: _modnorm(v, g_ref[...], sh_ref[0], sc_ref[0]).astype(o_ref.dtype)
    o_ref[0, 0, pl.ds(pl.multiple_of(HALO + k * rows, HALO), rows), :] = _modnorm_two_pass(
        x_ref, ms_scr, g_ref[...], sh_ref[0], sc_ref[0]).astype(o_ref.dtype)

    @pl.when(k == 0)
    def _():
        o_ref[0, 0, 0:HALO, :] = norm(prev_ref[0])

    @pl.when(k == pl.num_programs(2) - 1)
    def _():
        o_ref[0, 0, HALO + tl:, :] = norm(next_ref[0])


def _prenorm(x, g, shift, scale):
    b, l, d = x.shape
    tl = min(l, INPROJ_ROWS)
    rows = min(tl, 512)
    nk = tl // rows
    rpb = tl // HALO
    last = l // HALO - 1
    vec = pl.BlockSpec((1, 1, d), lambda bb, i, k: (bb, 0, 0))
    return pl.pallas_call(
        _prenorm_kernel,
        grid=(b, l // tl, nk),
        in_specs=[pl.BlockSpec((1, HALO, d), lambda bb, i, k: (bb, jnp.maximum(i * rpb - 1, 0), 0)),
                  pl.BlockSpec((1, rows, d), lambda bb, i, k: (bb, i * nk + k, 0)),
                  pl.BlockSpec((1, HALO, d), lambda bb, i, k: (bb, jnp.minimum((i + 1) * rpb, last), 0)),
                  pl.BlockSpec((1, d), lambda bb, i, k: (0, 0)), vec, vec],
        out_specs=pl.BlockSpec((1, 1, tl + 2 * HALO, d), lambda bb, i, k: (bb, i, 0, 0)),
        out_shape=jax.ShapeDtypeStruct((b, l // tl, tl + 2 * HALO, d), BF16),
        scratch_shapes=[pltpu.VMEM((rows, 1), F32)],
        compiler_params=_cparams("parallel", "parallel", "arbitrary"),
        name="prenorm",
    )(x, x, x, g.reshape(1, d), shift, scale)


def _inproj_kernel(h_ref, w_ref, cw_ref, cb_ref, xbc_ref, dt_ref, acc_scr):
    i = pl.program_id(1)
    k = pl.program_id(2)
    n_conv = pl.num_programs(2) - 1
    tm = dt_ref.shape[1]
    project = lambda: jnp.dot(h_ref[0, 0], w_ref[...], preferred_element_type=F32)

    def project_into(dst):
        dst[...] = project()
        dst[0:HALO, :] = jnp.where(i > 0, dst[0:HALO, :], 0.0)
        dst[HALO + tm:, :] = jnp.where(i < pl.num_programs(1) - 1, dst[HALO + tm:, :], 0.0)

    def conv_from(src):
        pad = M_CONV // 2
        rows = min(tm, CONV_ROWS)
        w_half = 0.5 * cw_ref[...]
        b_half = 0.5 * cb_ref[...]
        for r0 in range(0, tm, rows):
            ext = src[r0:r0 + rows + 2 * HALO, :]
            acc = None
            for tap in range(M_CONV):
                shifted = ext if tap == pad else pltpu.roll(ext, (pad - tap) % (rows + 2 * HALO), axis=0)
                term = shifted[HALO:HALO + rows] * w_half[tap:tap + 1, :]
                acc = term if acc is None else acc + term
            xbc_ref[0, r0:r0 + rows, :] = _silu_of_twice(acc + b_half).astype(xbc_ref.dtype)

    @pl.when(k < n_conv)
    def _():
        project_into(acc_scr)
        conv_from(acc_scr)

    @pl.when(k == n_conv)
    def _():
        dt_ref[0] = project()[HALO:HALO + tm, 0:LANES]


def _proj_kernel(h_ref, w_ref, o_ref):
    tm = o_ref.shape[1]
    acc = jnp.dot(h_ref[0, 0], w_ref[...], preferred_element_type=F32)
    o_ref[0] = acc[HALO:HALO + tm].astype(o_ref.dtype)


def _proj_plain(h, w, ncols):
    b, nt, tmh, d = h.shape
    tm = tmh - 2 * HALO
    tn = 2 * COL_TILE if ncols % (2 * COL_TILE) == 0 else COL_TILE
    return pl.pallas_call(
        _proj_kernel,
        grid=(b, nt, ncols // tn),
        in_specs=[pl.BlockSpec((1, 1, tmh, d), lambda bb, i, j: (bb, i, 0, 0)),
                  pl.BlockSpec((d, tn), lambda bb, i, j: (0, j))],
        out_specs=pl.BlockSpec((1, tm, tn), lambda bb, i, j: (bb, i, j)),
        out_shape=jax.ShapeDtypeStruct((b, nt * tm, ncols), BF16),
        compiler_params=_cparams("parallel", "parallel", "arbitrary"),
        name="proj_plain",
    )(h, w)


def _inproj(h, w, col0, conv_w, conv_b):
    b, nt, tmh, d = h.shape
    tm = tmh - 2 * HALO
    l = nt * tm
    tn = COL_TILE
    j0 = col0 // tn
    n_conv = conv_w.shape[1] // tn
    conv_idx = lambda j: jnp.minimum(j, n_conv - 1)
    return pl.pallas_call(
        _inproj_kernel,
        grid=(b, nt, n_conv + 1),
        in_specs=[pl.BlockSpec((1, 1, tmh, d), lambda bb, i, j: (bb, i, 0, 0)),
                  pl.BlockSpec((d, tn), lambda bb, i, j: (0, j + j0)),
                  pl.BlockSpec((M_CONV, tn), lambda bb, i, j: (0, conv_idx(j))),
                  pl.BlockSpec((1, tn), lambda bb, i, j: (0, conv_idx(j)))],
        out_specs=[pl.BlockSpec((1, tm, tn), lambda bb, i, j: (bb, i, conv_idx(j))),
                   pl.BlockSpec((1, tm, LANES), lambda bb, i, j: (bb, i, 0))],
        out_shape=[jax.ShapeDtypeStruct((b, l, n_conv * tn), BF16), jax.ShapeDtypeStruct((b, l, LANES), F32)],
        scratch_shapes=[pltpu.VMEM((tmh, tn), F32)],
        compiler_params=_cparams("parallel", "parallel", "arbitrary"),
        name="inproj",
    )(h, w, conv_w, conv_b.reshape(1, -1))


def _conv_taps(prev_ref, main_ref, next_ref, w_ref, b_ref, ktaps):
    i = pl.program_id(2)
    nblk = pl.num_programs(2)
    tl = main_ref.shape[1]
    pad = ktaps // 2
    prev = jnp.where(i > 0, prev_ref[0].astype(F32), 0.0)
    nxt = jnp.where(i < nblk - 1, next_ref[0].astype(F32), 0.0)
    ext = jnp.concatenate([prev, main_ref[0].astype(F32), nxt], axis=0)
    rows = tl + 2 * HALO
    acc = None
    for j in range(ktaps):
        shifted = ext if j == pad else pltpu.roll(ext, (pad - j) % rows, axis=0)
        term = shifted[HALO:HALO + tl] * w_ref[j:j + 1, :]
        acc = term if acc is None else acc + term
    return acc + b_ref[...]


def _conv_specs(tl, cw, l, col_block0):
    rpb = tl // HALO
    last = l // HALO - 1
    return [
        pl.BlockSpec((1, HALO, cw), lambda bb, c, i: (bb, jnp.maximum(i * rpb - 1, 0), c + col_block0)),
        pl.BlockSpec((1, tl, cw), lambda bb, c, i: (bb, i, c + col_block0)),
        pl.BlockSpec((1, HALO, cw), lambda bb, c, i: (bb, jnp.minimum((i + 1) * rpb, last), c + col_block0)),
    ]


def _hconv_kernel(p0, m0, n0, p1, m1, n1, p2, m2, n2, w0, b0, w1, b1, w2, b2, x0_ref, u_ref):
    x0_ref[0] = _conv_taps(p0, m0, n0, w0, b0, H_SHORT).astype(x0_ref.dtype)
    x1 = _conv_taps(p1, m1, n1, w1, b1, H_SHORT)
    v = _conv_taps(p2, m2, n2, w2, b2, H_SHORT)
    u_ref[0] = (x1 * v).astype(u_ref.dtype)


def _hconv(p, col0, w, bias, d):
    b, l, _ = p.shape
    tl = min(l, 1024)
    cw = min(d, COL_TILE)
    nb = d // cw
    specs, wspecs = [], []
    for s in range(3):
        specs += _conv_specs(tl, cw, l, col0 // cw + s * nb)
        wspecs += [pl.BlockSpec((H_SHORT, cw), functools.partial(lambda bb, c, i, s: (0, c + s * nb), s=s)),
                   pl.BlockSpec((1, cw), functools.partial(lambda bb, c, i, s: (0, c + s * nb), s=s))]
    bias2 = bias.reshape(1, 3 * d)
    out_spec = pl.BlockSpec((1, tl, cw), lambda bb, c, i: (bb, i, c))
    return pl.pallas_call(
        _hconv_kernel,
        grid=(b, nb, l // tl),
        in_specs=specs + wspecs,
        out_specs=[out_spec, out_spec],
        out_shape=[jax.ShapeDtypeStruct((b, l, d), BF16)] * 2,
        compiler_params=_cparams("parallel", "parallel", "arbitrary"),
        name="hconv",
    )(p, p, p, p, p, p, p, p, p, w, bias2, w, bias2, w, bias2)


def _ssd_kernel(*refs, reverse, with_output, epilogue, heads):
    if epilogue:
        (xbc_ref, dt_ref, dtb_ref, alog_ref, init_ref, sel_ref, yf_ref, z_ref, dsk_ref, ng_ref,
         y_ref, fin_ref, s_scr) = refs
    elif with_output:
        xbc_ref, dt_ref, dtb_ref, alog_ref, init_ref, sel_ref, y_ref, fin_ref, s_scr = refs
    else:
        xbc_ref, dt_ref, dtb_ref, alog_ref, init_ref, fin_ref, s_scr = refs
    t = M_CHUNK
    n = M_STATE
    assert n == t
    di = heads * M_HEADDIM
    gw = di // M_GROUPS
    rpg = heads // M_GROUPS
    lane0 = heads if reverse else 0
    edge = 0 if reverse else t - 1
    c = pl.program_id(1)

    nb = xbc_ref.shape[0]

    @pl.when(c == 0)
    def _():
        s_scr[...] = init_ref[...]

    row = lax.broadcasted_iota(jnp.int32, (t, t), 0)
    col = lax.broadcasted_iota(jnp.int32, (t, t), 1)
    mask = (row <= col) if reverse else (row >= col)
    lane = lax.broadcasted_iota(jnp.int32, (1, LANES), 1)
    lo = lane < M_HEADDIM
    mlo = jnp.where(lo, 1.0, 0.0).astype(BF16)
    mhi = jnp.where(lo, 0.0, 1.0).astype(BF16)

    def chunk_decays(bi):
        x = dt_ref[bi] + dtb_ref[...]
        dt = jnp.maximum(x, 0.0) + jnp.log1p(jnp.exp(-jnp.abs(x)))
        dta = dt * (-jnp.exp(alog_ref[...]))
        acs = jnp.dot(mask.astype(F32), dta, preferred_element_type=F32, precision=HIGHEST)
        acs_t = acs.T
        dt_t = dt.T
        acs2 = acs * LOG2_E
        arow_dt2 = (acs_t - jnp.log(dt_t)) * LOG2_E
        w_t = jnp.exp(acs_t[:, edge:edge + 1] - acs_t) * dt_t
        dec = jnp.exp(acs[edge:edge + 1, :])
        eacs = jnp.exp(acs).astype(BF16) if with_output else None
        return acs2, arow_dt2, w_t, dec, eacs

    def group(bi, g, acs2, arow_dt2, w_t, dec, eacs):
        bt = xbc_ref[bi, :, di + g * n:di + (g + 1) * n].astype(F32).T
        if with_output:
            cm = xbc_ref[bi, :, di + M_GROUPS * n + g * n:di + M_GROUPS * n + (g + 1) * n]
            cb = jnp.dot(cm, bt.astype(BF16), preferred_element_type=F32)
            y_off = jnp.dot(cm, s_scr[bi, g].astype(BF16), preferred_element_type=F32)
        y_parts = []
        for q in range(rpg // 2):
            c0 = g * gw + q * LANES
            xp = xbc_ref[bi, :, c0:c0 + LANES]
            sp = s_scr[bi, g, :, q * LANES:(q + 1) * LANES]
            x2 = jnp.concatenate([xp * mlo, xp * mhi], axis=0)
            hl0 = lane0 + g * rpg + 2 * q
            lhs, btw, decs = [], [], []
            for r in range(2):
                hl = hl0 + r
                if with_output:
                    acol = jnp.broadcast_to(acs2[:, hl:hl + 1], (t, t))
                    dk = jnp.exp2(jnp.where(mask, acol - arow_dt2[hl:hl + 1, :], -jnp.inf))
                    lhs.append((cb * dk).astype(BF16))
                btw.append((bt * w_t[hl:hl + 1, :]).astype(BF16))
                decs.append(jnp.broadcast_to(dec[:, hl:hl + 1], (1, LANES)))
            if with_output:
                e_pair = jnp.dot(eacs, sel_ref[hl0 // 2], preferred_element_type=F32)
                y_diag = jnp.dot(jnp.concatenate(lhs, axis=1), x2, preferred_element_type=F32)
                y_parts.append(y_diag + e_pair * y_off[:, q * LANES:(q + 1) * LANES])
            ds = jnp.dot(jnp.concatenate(btw, axis=1), x2, preferred_element_type=F32)
            s_scr[bi, g, :, q * LANES:(q + 1) * LANES] = jnp.where(lo, decs[0], decs[1]) * sp + ds
        if with_output:
            yg = jnp.concatenate(y_parts, axis=1) if len(y_parts) > 1 else y_parts[0]
            sl = slice(g * gw, (g + 1) * gw)
            if epilogue:
                yg = yg + yf_ref[bi, :, sl].astype(F32)
                v = (yg + xbc_ref[bi, :, sl].astype(F32) * dsk_ref[:, sl]) * _silu(z_ref[bi, :, sl].astype(F32))
                ms = jnp.mean(v * v, axis=-1, keepdims=True)
                y_ref[bi, :, sl] = (v * lax.rsqrt(ms + RMS_EPS) * ng_ref[:, sl]).astype(y_ref.dtype)
            else:
                y_ref[bi, :, sl] = yg.astype(y_ref.dtype)

    decays = [chunk_decays(bi) for bi in range(nb)]
    for g in range(M_GROUPS):
        for bi in range(nb):
            group(bi, g, *decays[bi])

    @pl.when(c == pl.num_programs(1) - 1)
    def _():
        fin_ref[...] = s_scr[...]


def _pair_selectors():
    p = jnp.arange(LANES // 2)[:, None, None]
    h = jnp.arange(LANES)[None, :, None]
    lane = jnp.arange(LANES)[None, None, :]
    return (h == 2 * p + (lane >= M_HEADDIM)).astype(BF16)


def _ssd(xbc, dtp, dtb, alog, init, heads, reverse, with_output, extra=None):
    b, l, xw = xbc.shape
    t = M_CHUNK
    nc = l // t
    di = heads * M_HEADDIM
    gw = di // M_GROUPS
    epilogue = extra is not None
    cidx = (lambda c: nc - 1 - c) if reverse else (lambda c: c)
    nb = SSD_BATCH_ROWS if b % SSD_BATCH_ROWS == 0 else 1
    state_spec = pl.BlockSpec((nb, M_GROUPS, M_STATE, gw), lambda bb, c: (bb, 0, 0, 0))
    in_specs = [pl.BlockSpec((nb, t, xw), lambda bb, c: (bb, cidx(c), 0)),
                pl.BlockSpec((nb, t, LANES), lambda bb, c: (bb, cidx(c), 0)),
                pl.BlockSpec((1, LANES), lambda bb, c: (0, 0)),
                pl.BlockSpec((1, LANES), lambda bb, c: (0, 0)),
                state_spec]
    args = [xbc, dtp, dtb, alog, init]
    out_specs, out_shape = [], []
    if with_output:
        in_specs.append(pl.BlockSpec((LANES // 2, LANES, LANES), lambda bb, c: (0, 0, 0)))
        args.append(_pair_selectors())
    if epilogue:
        yf, p, z_col0, dsk, ng = extra
        in_specs += [pl.BlockSpec((nb, t, di), lambda bb, c: (bb, cidx(c), 0)),
                     pl.BlockSpec((nb, t, di), lambda bb, c: (bb, cidx(c), z_col0 // di)),
                     pl.BlockSpec((1, di), lambda bb, c: (0, 0)),
                     pl.BlockSpec((1, di), lambda bb, c: (0, 0))]
        args += [yf, p, dsk, ng]
    if with_output:
        out_specs.append(pl.BlockSpec((nb, t, di), lambda bb, c: (bb, cidx(c), 0)))
        out_shape.append(jax.ShapeDtypeStruct((b, l, di), BF16))
    out_specs.append(state_spec)
    out_shape.append(jax.ShapeDtypeStruct((b, M_GROUPS, M_STATE, gw), F32))
    outs = pl.pallas_call(
        functools.partial(_ssd_kernel, reverse=reverse, with_output=with_output, epilogue=epilogue, heads=heads),
        grid=(b // nb, nc),
        in_specs=in_specs,
        out_specs=out_specs,
        out_shape=out_shape,
        scratch_shapes=[pltpu.VMEM((nb, M_GROUPS, M_STATE, gw), F32)],
        compiler_params=_cparams("parallel", "arbitrary"),
        name="ssd_bwd" if reverse else "ssd_fwd",
    )(*args)
    return (outs[0], outs[1]) if with_output else (None, outs[0])


def _filt_kernel(f_ref, w1_ref, b1_ref, w2_ref, b2_ref, fr_ref, w3_ref, dl_ref, o_ref, ss_ref):
    f = f_ref[...]
    fr = fr_ref[...]
    h = jnp.sin(fr * (jnp.dot(f, w1_ref[...], preferred_element_type=F32, precision=HIGHEST) + b1_ref[...]))
    h = jnp.sin(fr * (jnp.dot(h, w2_ref[...], preferred_element_type=F32, precision=HIGHEST) + b2_ref[...]))
    filt = _dot_split(h, w3_ref[...])
    tcol = f[:, H_EMB:H_EMB + 1]
    mcol = f[:, H_EMB + 1:H_EMB + 2]
    out = filt * jnp.exp(-tcol * dl_ref[...]) * mcol
    o_ref[...] = out

    @pl.when(pl.program_id(1) == 0)
    def _():
        ss_ref[...] = jnp.zeros_like(ss_ref)

    ss_ref[...] += jnp.sum(out * out, axis=0, keepdims=True)


def _filter_features(l):
    t = jnp.linspace(0.0, 1.0, l, dtype=F32)[:, None]
    ang = (2.0 * math.pi / l) * jnp.arange(l, dtype=F32)[:, None]
    nb = (H_EMB - 1) // 2
    bands = jnp.linspace(1e-4, nb - 1, nb, dtype=F32)[None, :]
    feats = jnp.concatenate([t, jnp.cos(bands * ang), -jnp.sin(bands * ang), t, jnp.ones_like(t)], axis=-1)
    rev = jnp.concatenate([feats[0:1] * 0.0, jnp.flip(feats[1:], axis=0)], axis=0)
    full = jnp.concatenate([feats, rev], axis=0)
    return jnp.pad(full, ((0, 0), (0, FEAT_COLS - full.shape[1])))


def _hyena_filter(l, w1, b1, w2, b2, freq, w3):
    hid = w2.shape[0]
    d = w3.shape[1] // 2
    feats = _filter_features(l)
    w1p = jnp.pad(w1, ((0, FEAT_COLS - w1.shape[0]), (0, 0)))
    deltas = jnp.abs(jnp.linspace(math.log(H_DECAY_TARGET) / H_SLOW_PCT, math.log(H_DECAY_TARGET) / H_FAST_PCT,
                                  d, dtype=F32))[None, :]
    tl = min(l, 512)
    cw = d
    nb = d // cw
    nrow = l // tl
    small = lambda shape: pl.BlockSpec(shape, lambda c, i: (0, 0))
    return pl.pallas_call(
        _filt_kernel,
        grid=(nb, 2 * nrow),
        in_specs=[pl.BlockSpec((tl, FEAT_COLS), lambda c, i: (i, 0)),
                  small((FEAT_COLS, hid)), small((1, hid)), small((hid, hid)), small((1, hid)), small((1, hid)),
                  pl.BlockSpec((hid, cw), lambda c, i: (0, c + (i // nrow) * nb)),
                  pl.BlockSpec((1, cw), lambda c, i: (0, c))],
        out_specs=[pl.BlockSpec((tl, cw), lambda c, i: (i, c)),
                   pl.BlockSpec((1, cw), lambda c, i: (0, c))],
        out_shape=[jax.ShapeDtypeStruct((2 * l, d), F32), jax.ShapeDtypeStruct((1, d), F32)],
        compiler_params=_cparams("parallel", "arbitrary"),
        name="hyena_filter",
    )(feats, w1p, b1.reshape(1, hid), w2, b2.reshape(1, hid), freq.reshape(1, hid), w3, deltas)


def _fft_split(n):
    n1 = 1 << ((n.bit_length() - 1 + 1) // 2)
    return n1, n // n1


def _dft_consts(n1, n2):
    n = n1 * n2
    h = n1 // 2
    k1 = jnp.arange(n1, dtype=jnp.int32)
    t1 = jnp.arange(n1, dtype=jnp.int32)
    t2 = jnp.arange(n2, dtype=jnp.int32)
    idx = (k1[None, :, None] * (n2 * t1[None, None, :] + t2[:, None, None])) % n
    ang = idx.astype(F32) * (2.0 * math.pi / n)
    cs, sn = jnp.cos(ang), jnp.sin(ang)
    ch, sh = cs[..., :h], sn[..., :h]
    t_data = jnp.concatenate([jnp.concatenate([ch, sh], -1), jnp.concatenate([-sh, ch], -1)], axis=1)
    t_filt = jnp.concatenate([cs, -sn], axis=1)
    ct, st = jnp.swapaxes(ch, 1, 2), jnp.swapaxes(sh, 1, 2)
    t_inv = jnp.concatenate([jnp.concatenate([ct, -st], -1), jnp.concatenate([st, ct], -1)], axis=1) / n
    k2 = jnp.arange(n2, dtype=jnp.int32)
    ang2 = ((k2[:, None] * k2[None, :]) % n2).astype(F32) * (2.0 * math.pi / n2)
    c2, s2 = jnp.cos(ang2), jnp.sin(ang2)
    m_fwd = jnp.concatenate([jnp.concatenate([c2, s2], -1), jnp.concatenate([-s2, c2], -1)], axis=0)
    m_inv = jnp.concatenate([jnp.concatenate([c2, -s2], -1), jnp.concatenate([s2, c2], -1)], axis=0)
    return dict(t_data=t_data.astype(BF16), t_filt=t_filt.astype(BF16), t_inv=t_inv.astype(BF16),
                m_fwd=m_fwd.astype(BF16), m_inv=m_inv.astype(BF16))


def _fft1_kernel(x_ref, t_ref, o_ref):
    a = x_ref.shape[1]
    xs = [pltpu.einshape("rjc->jrc", x_ref[0, s].astype(F32)) for s in range(a)]
    n1 = o_ref.shape[2]
    for j in range(SUBLANES):
        x = jnp.concatenate([xs[s][j] for s in range(a)], axis=0).astype(BF16)
        o = jnp.dot(t_ref[j], x, preferred_element_type=F32)
        o_ref[0, j] = _pack_complex(o[:n1], o[n1:])


def _pack_complex(re, im):
    r = lax.bitcast_convert_type(re.astype(BF16).astype(F32), jnp.uint32)
    i = lax.bitcast_convert_type(im.astype(BF16).astype(F32), jnp.uint32)
    return (r >> 16) | i


def _unpack_complex(w):
    re = lax.bitcast_convert_type(w << 16, F32)
    im = lax.bitcast_convert_type(w & jnp.uint32(0xFFFF0000), F32)
    return re, im


def _fft_stage1(xv, tmat, n1, n2, d):
    p, a, r, _, _ = xv.shape
    cw = min(d, 2 * COL_TILE)
    return pl.pallas_call(
        _fft1_kernel,
        grid=(n2 // SUBLANES, d // cw, p),
        in_specs=[pl.BlockSpec((1, a, r, SUBLANES, cw), lambda j, c, pp: (pp, 0, 0, j, c)),
                  pl.BlockSpec((SUBLANES, 2 * n1, n1), lambda j, c, pp: (j, 0, 0))],
        out_specs=pl.BlockSpec((1, SUBLANES, n1, cw), lambda j, c, pp: (pp, j, 0, c)),
        out_shape=jax.ShapeDtypeStruct((p, n2, n1, d), jnp.uint32),
        compiler_params=_cparams("parallel", "parallel", "parallel"),
        name="fft_stage1",
    )(xv, tmat)


def _fft2_kernel(a_ref, af_ref, ss_ref, mf_ref, mi_ref, o_ref, h_scr):
    n2 = a_ref.shape[1]

    def stage(w):
        re, im = _unpack_complex(w)
        return jnp.dot(mf_ref[...], jnp.concatenate([re, im], axis=0).astype(BF16), preferred_element_type=F32)

    @pl.when(pl.program_id(2) == 0)
    def _():
        af = pltpu.einshape("tkc->ktc", af_ref[0])
        for k in range(SUBLANES):
            h_scr[k] = stage(af[k]) * lax.rsqrt(ss_ref[...] + RMS_EPS)

    cw = a_ref.shape[3]
    half = max(cw // 2, LANES)
    for c0 in range(0, cw, half):
        a = pltpu.einshape("tkc->ktc", a_ref[0, :, :, c0:c0 + half])
        outs = []
        for k in range(SUBLANES):
            xk = stage(a[k])
            xr, xi = xk[:n2], xk[n2:]
            hr, hi = h_scr[k, 0:n2, c0:c0 + half], h_scr[k, n2:, c0:c0 + half]
            z = jnp.concatenate([xr * hr - xi * hi, xr * hi + xi * hr], axis=0).astype(BF16)
            bk = jnp.dot(mi_ref[...], z, preferred_element_type=F32)
            outs.append(_pack_complex(bk[:n2], bk[n2:]))
        o_ref[0, :, :, c0:c0 + half] = pltpu.einshape("ktc->tkc", jnp.stack(outs, axis=0))


def _fft_stage2(av, afv, ss, m_fwd, m_inv, n1, n2, d):
    p = av.shape[0]
    cw = min(d, COL_TILE)
    kb = SUBLANES
    return pl.pallas_call(
        _fft2_kernel,
        grid=(n1 // kb, d // cw, p),
        in_specs=[pl.BlockSpec((1, n2, kb, cw), lambda k, c, pp: (pp, 0, k, c)),
                  pl.BlockSpec((1, n2, kb, cw), lambda k, c, pp: (0, 0, k, c)),
                  pl.BlockSpec((1, cw), lambda k, c, pp: (0, c)),
                  pl.BlockSpec((2 * n2, 2 * n2), lambda k, c, pp: (0, 0)),
                  pl.BlockSpec((2 * n2, 2 * n2), lambda k, c, pp: (0, 0))],
        out_specs=pl.BlockSpec((1, n2, kb, cw), lambda k, c, pp: (pp, 0, k, c)),
        out_shape=jax.ShapeDtypeStruct(av.shape, jnp.uint32),
        scratch_shapes=[pltpu.VMEM((kb, 2 * n2, cw), F32)],
        compiler_params=_cparams("parallel", "parallel", "arbitrary"),
        name="fft_stage2",
    )(av, afv, ss, m_fwd, m_inv)


def _fft3_kernel(b_ref, t_ref, u_ref, x0_ref, bias_ref, o_ref):
    n1 = b_ref.shape[2]
    h = n1 // 2
    ys = []
    for j in range(SUBLANES):
        re, im = _unpack_complex(b_ref[0, j])
        bv = jnp.concatenate([re, im], axis=0).astype(BF16)
        ys.append(jnp.dot(t_ref[j], bv, preferred_element_type=F32))
    y = pltpu.einshape("jtc->tjc", jnp.stack(ys, axis=0))
    for s in range(2):
        conv = y[s * h:(s + 1) * h] + u_ref[0, s].astype(F32) * bias_ref[...]
        o_ref[0, s] = (x0_ref[0, s].astype(F32) * conv).astype(o_ref.dtype)


def _fft_stage3(bv, t_inv, uv, x0v, bias, n1, n2, d):
    p = bv.shape[0]
    h = n1 // 2
    cw = min(d, 2 * COL_TILE)
    io_spec = pl.BlockSpec((1, 2, h, SUBLANES, cw), lambda j, c, pp: (pp, 0, 0, j, c))
    return pl.pallas_call(
        _fft3_kernel,
        grid=(n2 // SUBLANES, d // cw, p),
        in_specs=[pl.BlockSpec((1, SUBLANES, n1, cw), lambda j, c, pp: (pp, j, 0, c)),
                  pl.BlockSpec((SUBLANES, n1, 2 * n1), lambda j, c, pp: (j, 0, 0)),
                  io_spec, io_spec,
                  pl.BlockSpec((1, cw), lambda j, c, pp: (0, c))],
        out_specs=io_spec,
        out_shape=jax.ShapeDtypeStruct(uv.shape, BF16),
        compiler_params=_cparams("parallel", "parallel", "parallel"),
        name="fft_stage3",
    )(bv, t_inv, uv, x0v, bias)


def _filter_spectrum(l, consts, lp):
    n1, n2 = _fft_split(2 * l)
    d = lp['h_bias'].shape[0]
    full, ss = _hyena_filter(l, lp['hf_w1'], lp['hf_b1'], lp['hf_w2'], lp['hf_b2'], lp['hf_freq'], lp['hf_w3'])
    return _fft_stage1(full.reshape(1, 1, n1, n2, d), consts['t_filt'], n1, n2, d), ss


def _long_conv(u, x0, hf, consts, bias):
    b, l, d = u.shape
    n1, n2 = _fft_split(2 * l)
    p = b // 2
    h = n1 // 2
    uv = u.reshape(p, 2, h, n2, d)
    a = _fft_stage1(uv, consts['t_data'], n1, n2, d)
    bv = _fft_stage2(a, hf[0], hf[1], consts['m_fwd'], consts['m_inv'], n1, n2, d)
    y = _fft_stage3(bv, consts['t_inv'], uv, x0.reshape(p, 2, h, n2, d), bias.reshape(1, d), n1, n2, d)
    return y.reshape(b, l, d)


def _merge_kernel(ym_ref, yh_ref, ga_ref, gb_ref, wm_ref, wh_ref, o_ref):
    a = jnp.dot(ym_ref[0], wm_ref[...], preferred_element_type=F32)
    bb = jnp.dot(yh_ref[0], wh_ref[...], preferred_element_type=F32)
    ga = jax.nn.sigmoid(ga_ref[0].astype(F32))
    gb = jax.nn.sigmoid(gb_ref[0].astype(F32))
    o_ref[0] = (ga * a + gb * bb).astype(o_ref.dtype)


def _merge(ym, yh, p, gate_col0, wm, wh):
    b, l, di = ym.shape
    d = yh.shape[2]
    tm = min(l, 1024)
    tn = min(d, COL_TILE)
    g0 = gate_col0 // tn
    nb = d // tn
    return pl.pallas_call(
        _merge_kernel,
        grid=(b, l // tm, nb),
        in_specs=[pl.BlockSpec((1, tm, di), lambda bb, i, j: (bb, i, 0)),
                  pl.BlockSpec((1, tm, d), lambda bb, i, j: (bb, i, 0)),
                  pl.BlockSpec((1, tm, tn), lambda bb, i, j: (bb, i, g0 + j)),
                  pl.BlockSpec((1, tm, tn), lambda bb, i, j: (bb, i, g0 + nb + j)),
                  pl.BlockSpec((di, tn), lambda bb, i, j: (0, j)),
                  pl.BlockSpec((d, tn), lambda bb, i, j: (0, j))],
        out_specs=pl.BlockSpec((1, tm, tn), lambda bb, i, j: (bb, i, j)),
        out_shape=jax.ShapeDtypeStruct((b, l, d), BF16),
        compiler_params=_cparams("parallel", "parallel", "arbitrary"),
        name="merge",
    )(ym, yh, p, p, wm, wh)


def _resid_mm_kernel(a_ref, w_ref, x_ref, g_ref, o_ref):
    o_ref[0] = x_ref[0] + g_ref[0] * jnp.dot(a_ref[0], w_ref[...], preferred_element_type=F32)


def _resid_mm(a, w, x, gate):
    b, l, k = a.shape
    d = w.shape[1]
    tm = min(l, 1024)
    tn = min(d, 2 * COL_TILE)
    return pl.pallas_call(
        _resid_mm_kernel,
        grid=(b, l // tm, d // tn),
        in_specs=[pl.BlockSpec((1, tm, k), lambda bb, i, j: (bb, i, 0)),
                  pl.BlockSpec((k, tn), lambda bb, i, j: (0, j)),
                  pl.BlockSpec((1, tm, tn), lambda bb, i, j: (bb, i, j)),
                  pl.BlockSpec((1, 1, tn), lambda bb, i, j: (bb, 0, j))],
        out_specs=pl.BlockSpec((1, tm, tn), lambda bb, i, j: (bb, i, j)),
        out_shape=jax.ShapeDtypeStruct((b, l, d), F32),
        compiler_params=_cparams("parallel", "parallel", "arbitrary"),
        name="resid_mm",
    )(a, w, x, gate)


def _ffn_kernel(x_ref, g_ref, sh_ref, sc_ref, gt_ref, wg_ref, wu_ref, wd_ref, fg_ref, o_ref, h_scr, ms_scr, *,
                final_norm):
    j = pl.program_id(2)

    @pl.when(j == 0)
    def _():
        h_scr[...] = _modnorm_two_pass(x_ref, ms_scr, g_ref[...], sh_ref[0], sc_ref[0]).astype(BF16)
        o_ref[...] = jnp.zeros_like(o_ref)

    tm = h_scr.shape[0]
    rows_per_pass = min(tm, 512)
    for r0 in range(0, tm, rows_per_pass):
        h = h_scr[r0:r0 + rows_per_pass, :]
        gg = jnp.dot(h, wg_ref[...], preferred_element_type=F32)
        uu = jnp.dot(h, wu_ref[...], preferred_element_type=F32)
        o_ref[0, r0:r0 + rows_per_pass, :] += jnp.dot((_silu(gg) * uu).astype(BF16), wd_ref[...],
                                                      preferred_element_type=F32)

    @pl.when(j == pl.num_programs(2) - 1)
    def _():
        y = x_ref[0] + gt_ref[0] * o_ref[0]
        if final_norm:
            ms = jnp.mean(y * y, axis=-1, keepdims=True)
            y = y * lax.rsqrt(ms + RMS_EPS) * fg_ref[...]
        o_ref[0] = y


def _ffn(x, g, shift, scale, gate, w_gu, w_down, final_g=None):
    b, l, d = x.shape
    f = w_down.shape[0]
    tm = min(l, 1024)
    fg = jnp.ones((1, d), F32) if final_g is None else final_g.reshape(1, d)
    tf = 256
    nf = f // tf
    vec = pl.BlockSpec((1, 1, d), lambda bb, i, j: (bb, 0, 0))
    return pl.pallas_call(
        functools.partial(_ffn_kernel, final_norm=final_g is not None),
        grid=(b, l // tm, nf),
        in_specs=[pl.BlockSpec((1, tm, d), lambda bb, i, j: (bb, i, 0)),
                  pl.BlockSpec((1, d), lambda bb, i, j: (0, 0)),
                  vec, vec, vec,
                  pl.BlockSpec((d, tf), lambda bb, i, j: (0, j)),
                  pl.BlockSpec((d, tf), lambda bb, i, j: (0, j + nf)),
                  pl.BlockSpec((tf, d), lambda bb, i, j: (j, 0)),
                  pl.BlockSpec((1, d), lambda bb, i, j: (0, 0))],
        out_specs=pl.BlockSpec((1, tm, d), lambda bb, i, j: (bb, i, 0)),
        out_shape=jax.ShapeDtypeStruct((b, l, d), F32),
        scratch_shapes=[pltpu.VMEM((tm, d), BF16), pltpu.VMEM((tm, 1), F32)],
        compiler_params=_cparams("parallel", "parallel", "arbitrary"),
        name="ffn",
    )(x, g.reshape(1, d), shift, scale, gate, w_gu, w_gu, w_down, fg)


def _grid_pos_embed(rows, d):
    r, col = jnp.meshgrid(jnp.arange(rows), jnp.arange(GRID_W), indexing='ij')
    quarter = d // 4
    omega = 1.0 / (POS_BASE ** (jnp.arange(quarter, dtype=F32) / quarter))

    def axis_embed(pos):
        ang = pos.reshape(-1)[:, None].astype(F32) * omega[None, :]
        return jnp.concatenate([jnp.sin(ang), jnp.cos(ang)], axis=-1)

    return jnp.concatenate([axis_embed(r), axis_embed(col)], axis=-1).astype(F32)


class _Cols:
    def __init__(self, d):
        self.d = d
        self.di = 2 * d
        self.heads = self.di // M_HEADDIM
        self.xbc = self.di + 2 * M_GROUPS * M_STATE
        self.z0 = 0
        self.gate0 = self.z0 + self.di
        self.hy0 = self.gate0 + 2 * d
        self.xbc0 = self.hy0 + 3 * d
        self.dt0 = self.xbc0 + self.xbc
        self.total = self.dt0 + COL_TILE
        assert 2 * self.heads <= LANES and self.heads % (2 * M_GROUPS) == 0
        for off in (self.gate0, self.hy0, self.xbc0, self.dt0):
            assert off % COL_TILE == 0


def _pack_w_in(w_in, cols):
    o_dt = cols.xbc
    o_z = o_dt + 2 * cols.heads
    o_hy = o_z + cols.di
    o_gate = o_hy + 3 * cols.d
    parts = [w_in[:, o_z:o_hy], w_in[:, o_gate:], w_in[:, o_hy:o_gate], w_in[:, :o_dt], w_in[:, o_dt:o_z]]
    w = jnp.concatenate(parts, axis=1)
    return jnp.pad(w, ((0, 0), (0, cols.total - w.shape[1]))).astype(BF16)


def _pad_lanes(v):
    v = v.reshape(1, -1)
    return jnp.pad(v, ((0, 0), (0, LANES - v.shape[1])))


def _mix_sublayer(x, mods, lp, cols, hf, consts, init_f, init_b):
    h = _prenorm(x, lp['norm1_g'], mods[0], mods[1])
    p = _proj_plain(h, lp['w_in'], cols.xbc0)
    xbc, dtp = _inproj(h, lp['w_in'], cols.xbc0, lp['m_conv_w'], lp['m_conv_b'])
    y_f, fin_f = _ssd(xbc, dtp, lp['dtb'], lp['alog'], init_f, cols.heads, False, True)
    ym, fin_b = _ssd(xbc, dtp, lp['dtb'], lp['alog'], init_b, cols.heads, True, True,
                     extra=(y_f, p, cols.z0, lp['dskip'], lp['m_norm_g']))
    x0, u = _hconv(p, cols.hy0, lp['h_conv_w'], lp['h_conv_b'], cols.d)
    yh = _long_conv(u, x0, hf, consts, lp['h_bias'])
    merged = _merge(ym, yh, p, cols.gate0, lp['m_w_out'], lp['h_w_out'])
    return _resid_mm(merged, lp['w_merge_out'], x, mods[2]), fin_f, fin_b


def _ctx_final_states(x, mods, lp, cols, init_f, init_b):
    h = _prenorm(x, lp['norm1_g'], mods[0], mods[1])
    xbc, dtp = _inproj(h, lp['w_in'], cols.xbc0, lp['m_conv_w'], lp['m_conv_b'])
    _, fin_f = _ssd(xbc, dtp, lp['dtb'], lp['alog'], init_f, cols.heads, False, False)
    _, fin_b = _ssd(xbc, dtp, lp['dtb'], lp['alog'], init_b, cols.heads, True, False)
    return fin_f, fin_b


def kernel(x, c, ctx, c_ctx, ada_w, ada_b, norm1_g, w_in, m_conv_w, m_conv_b, m_dt_bias, m_a_log, m_d, m_norm_g,
           m_w_out, h_conv_w, h_conv_b, hf_w1, hf_b1, hf_w2, hf_b2, hf_freq, hf_w3, h_bias, h_w_out, w_merge_out,
           norm2_g, ffn_w_gu, ffn_w_down, final_g):
    b, l, d = x.shape
    lc = ctx.shape[1]
    depth = ada_w.shape[0]
    cols = _Cols(d)
    assert b % 2 == 0 and b + 1 <= SUBLANES and l % M_CHUNK == 0 and lc % M_CHUNK == 0

    cvecs = jnp.concatenate([c, c_ctx[None], jnp.zeros((SUBLANES - b - 1, d), F32)], axis=0)
    mods_all = _mods(cvecs, ada_w, ada_b)
    consts_l = _dft_consts(*_fft_split(2 * l))
    consts_c = _dft_consts(*_fft_split(2 * lc))

    x_l = _add_pos(x, _grid_pos_embed(l // GRID_W, d))
    x_c = ctx
    zero_state = jnp.zeros((b, M_GROUPS, M_STATE, cols.di // M_GROUPS), F32)
    for i in range(depth):
        lp = dict(norm1_g=norm1_g[i], w_in=_pack_w_in(w_in[i], cols), m_conv_w=m_conv_w[i], m_conv_b=m_conv_b[i],
                  dtb=_pad_lanes(m_dt_bias[i]), alog=_pad_lanes(m_a_log[i]),
                  dskip=jnp.repeat(m_d[i], M_HEADDIM).reshape(1, cols.di), m_norm_g=m_norm_g[i].reshape(1, cols.di),
                  m_w_out=m_w_out[i].astype(BF16), h_conv_w=h_conv_w[i], h_conv_b=h_conv_b[i],
                  hf_w1=hf_w1[i], hf_b1=hf_b1[i], hf_w2=hf_w2[i], hf_b2=hf_b2[i], hf_freq=hf_freq[i], hf_w3=hf_w3[i],
                  h_bias=h_bias[i], h_w_out=h_w_out[i].astype(BF16), w_merge_out=w_merge_out[i].astype(BF16),
                  norm2_g=norm2_g[i], ffn_w_gu=ffn_w_gu[i].astype(BF16), ffn_w_down=ffn_w_down[i].astype(BF16))
        m = mods_all[i].reshape(SUBLANES, ADA_CHUNKS, d)
        mods_l = [m[:b, k][:, None, :] for k in range(ADA_CHUNKS)]
        mods_c = [jnp.broadcast_to(m[b, k][None, None, :], (b, 1, d)) for k in range(ADA_CHUNKS)]
        if i < depth - 1:
            hf_c = _filter_spectrum(lc, consts_c, lp)
            x_c_mixed, fin_f, fin_b = _mix_sublayer(x_c, mods_c, lp, cols, hf_c, consts_c, zero_state, zero_state)
        else:
            fin_f, fin_b = _ctx_final_states(x_c, mods_c, lp, cols, zero_state, zero_state)
        hf_l = _filter_spectrum(l, consts_l, lp)
        x_l, _, _ = _mix_sublayer(x_l, mods_l, lp, cols, hf_l, consts_l, fin_f, fin_b)
        x_l = _ffn(x_l, lp['norm2_g'], mods_l[3], mods_l[4], mods_l[5], lp['ffn_w_gu'], lp['ffn_w_down'],
                   final_g=final_g if i == depth - 1 else None)
        if i < depth - 1:
            x_c = _ffn(x_c_mixed, lp['norm2_g'], mods_c[3], mods_c[4], mods_c[5], lp['ffn_w_gu'], lp['ffn_w_down'])
    return x_l
```

```python
import functools
import math

import jax
import jax.numpy as jnp
from jax import lax
from jax.experimental import pallas as pl
from jax.experimental.pallas import tpu as pltpu

F32 = jnp.float32
BF16 = jnp.bfloat16
HIGHEST = lax.Precision.HIGHEST

RMS_EPS = 1e-6
LOG2_E = 1.4426950408889634
GRID_W = 64
POS_BASE = 10000.0
ADA_CHUNKS = 6
M_HEADDIM = 64
M_GROUPS = 8
M_STATE = 128
M_CONV = 5
M_CHUNK = 128
H_SHORT = 3
H_EMB = 33
H_DECAY_TARGET = 1e-2
H_FAST_PCT = 0.3
H_SLOW_PCT = 1.5

LANES = 128
SUBLANES = 8
HALO = 16
VMEM_LIMIT_BYTES = 56 * 1024 * 1024

COL_TILE = 512
FEAT_COLS = 40
SSD_BATCH_ROWS = 2


def _cparams(*sem):
    return pltpu.CompilerParams(dimension_semantics=sem, vmem_limit_bytes=VMEM_LIMIT_BYTES)


def _silu_of_twice(h):
    return h + h * jnp.tanh(h)


def _silu(v):
    return _silu_of_twice(0.5 * v)


def _round_up(a, m):
    return -(-a // m) * m


def _mods_kernel(c_ref, w_ref, b_ref, o_ref):
    s = _silu(c_ref[...])
    o_ref[0] = jnp.dot(s, w_ref[0], preferred_element_type=F32, precision=HIGHEST) + b_ref[0]


def _mods(cvecs, ada_w, ada_b):
    depth, d, n = ada_w.shape
    tn = min(n, 1024)
    return pl.pallas_call(
        _mods_kernel,
        grid=(depth, n // tn),
        in_specs=[pl.BlockSpec((SUBLANES, d), lambda i, j: (0, 0)),
                  pl.BlockSpec((1, d, tn), lambda i, j: (i, 0, j)),
                  pl.BlockSpec((1, 1, tn), lambda i, j: (i, 0, j))],
        out_specs=pl.BlockSpec((1, SUBLANES, tn), lambda i, j: (i, 0, j)),
        out_shape=jax.ShapeDtypeStruct((depth, SUBLANES, n), F32),
        compiler_params=_cparams("parallel", "parallel"),
        name="mods",
    )(cvecs, ada_w, ada_b.reshape(depth, 1, n))


def _add_kernel(x_ref, p_ref, o_ref):
    o_ref[0] = x_ref[0] + p_ref[...]


def _add_pos(x, pos):
    b, l, d = x.shape
    tl = min(l, 1024)
    return pl.pallas_call(
        _add_kernel,
        grid=(l // tl, b),
        in_specs=[pl.BlockSpec((1, tl, d), lambda i, bb: (bb, i, 0)),
                  pl.BlockSpec((tl, d), lambda i, bb: (i, 0))],
        out_specs=pl.BlockSpec((1, tl, d), lambda i, bb: (bb, i, 0)),
        out_shape=jax.ShapeDtypeStruct(x.shape, F32),
        compiler_params=_cparams("parallel", "parallel"),
        name="add_pos",
    )(x, pos)


def _modnorm(x, g, shift, scale):
    ms = jnp.mean(x * x, axis=-1, keepdims=True)
    y = x * lax.rsqrt(ms + RMS_EPS) * g
    return y * (1.0 + scale) + shift


INPROJ_ROWS = 2048
CONV_ROWS = 256


def _modnorm_two_pass(x_ref, ms_scr, g, shift, scale):
    x = x_ref[0]
    ms_scr[...] = jnp.mean(x * x, axis=-1, keepdims=True)
    y = x_ref[0] * lax.rsqrt(ms_scr[...] + RMS_EPS) * g
    return y * (1.0 + scale) + shift


def _prenorm_kernel(prev_ref, x_ref, next_ref, g_ref, sh_ref, sc_ref, o_ref, ms_scr):
    k = pl.program_id(2)
    rows = x_ref.shape[1]
    tl = o_ref.shape[2] - 2 * HALO
    norm = lambda v: _modnorm(v, g_ref[...], sh_ref[0], sc_ref[0]).astype(o_ref.dtype)
    o_ref[0, 0, pl.ds(pl.multiple_of(HALO + k * rows, HALO), rows), :] = _modnorm_two_pass(
        x_ref, ms_scr, g_ref[...], sh_ref[0], sc_ref[0]).astype(o_ref.dtype)

    @pl.when(k == 0)
    def _():
        o_ref[0, 0, 0:HALO, :] = norm(prev_ref[0])

    @pl.when(k == pl.num_programs(2) - 1)
    def _():
        o_ref[0, 0, HALO + tl:, :] = norm(next_ref[0])


def _prenorm(x, g, shift, scale):
    b, l, d = x.shape
    tl = min(l, INPROJ_ROWS)
    rows = min(tl, 512)
    nk = tl // rows
    rpb = tl // HALO
    last = l // HALO - 1
    vec = pl.BlockSpec((1, 1, d), lambda bb, i, k: (bb, 0, 0))
    return pl.pallas_call(
        _prenorm_kernel,
        grid=(b, l // tl, nk),
        in_specs=[pl.BlockSpec((1, HALO, d), lambda bb, i, k: (bb, jnp.maximum(i * rpb - 1, 0), 0)),
                  pl.BlockSpec((1, rows, d), lambda bb, i, k: (bb, i * nk + k, 0)),
                  pl.BlockSpec((1, HALO, d), lambda bb, i, k: (bb, jnp.minimum((i + 1) * rpb, last), 0)),
                  pl.BlockSpec((1, d), lambda bb, i, k: (0, 0)), vec, vec],
        out_specs=pl.BlockSpec((1, 1, tl + 2 * HALO, d), lambda bb, i, k: (bb, i, 0, 0)),
        out_shape=jax.ShapeDtypeStruct((b, l // tl, tl + 2 * HALO, d), BF16),
        scratch_shapes=[pltpu.VMEM((rows, 1), F32)],
        compiler_params=_cparams("parallel", "parallel", "arbitrary"),
        name="prenorm",
    )(x, x, x, g.reshape(1, d), shift, scale)


def _inproj_kernel(h_ref, w_ref, cw_ref, cb_ref, xbc_ref, dt_ref, acc_scr):
    i = pl.program_id(1)
    k = pl.program_id(2)
    n_conv = pl.num_programs(2) - 1
    tm = dt_ref.shape[1]
    project = lambda: jnp.dot(h_ref[0, 0], w_ref[...], preferred_element_type=F32)

    def project_into(dst):
        dst[...] = project()
        dst[0:HALO, :] = jnp.where(i > 0, dst[0:HALO, :], 0.0)
        dst[HALO + tm:, :] = jnp.where(i < pl.num_programs(1) - 1, dst[HALO + tm:, :], 0.0)

    def conv_from(src):
        pad = M_CONV // 2
        rows = min(tm, CONV_ROWS)
        w_half = 0.5 * cw_ref[...]
        b_half = 0.5 * cb_ref[...]
        for r0 in range(0, tm, rows):
            ext = src[r0:r0 + rows + 2 * HALO, :]
            acc = None
            for tap in range(M_CONV):
                shifted = ext if tap == pad else pltpu.roll(ext, (pad - tap) % (rows + 2 * HALO), axis=0)
                term = shifted[HALO:HALO + rows] * w_half[tap:tap + 1, :]
                acc = term if acc is None else acc + term
            xbc_ref[0, r0:r0 + rows, :] = _silu_of_twice(acc + b_half).astype(xbc_ref.dtype)

    @pl.when(k < n_conv)
    def _():
        project_into(acc_scr)
        conv_from(acc_scr)

    @pl.when(k == n_conv)
    def _():
        dt_ref[0] = project()[HALO:HALO + tm, 0:LANES]


def _proj_kernel(h_ref, w_ref, o_ref):
    tm = o_ref.shape[1]
    acc = jnp.dot(h_ref[0, 0], w_ref[...], preferred_element_type=F32)
    o_ref[0] = acc[HALO:HALO + tm].astype(o_ref.dtype)


def _proj_plain(h, w, ncols):
    b, nt, tmh, d = h.shape
    tm = tmh - 2 * HALO
    tn = 2 * COL_TILE if ncols % (2 * COL_TILE) == 0 else COL_TILE
    return pl.pallas_call(
        _proj_kernel,
        grid=(b, nt, ncols // tn),
        in_specs=[pl.BlockSpec((1, 1, tmh, d), lambda bb, i, j: (bb, i, 0, 0)),
                  pl.BlockSpec((d, tn), lambda bb, i, j: (0, j))],
        out_specs=pl.BlockSpec((1, tm, tn), lambda bb, i, j: (bb, i, j)),
        out_shape=jax.ShapeDtypeStruct((b, nt * tm, ncols), BF16),
        compiler_params=_cparams("parallel", "parallel", "arbitrary"),
        name="proj_plain",
    )(h, w)


def _inproj(h, w, col0, conv_w, conv_b):
    b, nt, tmh, d = h.shape
    tm = tmh - 2 * HALO
    l = nt * tm
    tn = COL_TILE
    j0 = col0 // tn
    n_conv = conv_w.shape[1] // tn
    conv_idx = lambda j: jnp.minimum(j, n_conv - 1)
    return pl.pallas_call(
        _inproj_kernel,
        grid=(b, nt, n_conv + 1),
        in_specs=[pl.BlockSpec((1, 1, tmh, d), lambda bb, i, j: (bb, i, 0, 0)),
                  pl.BlockSpec((d, tn), lambda bb, i, j: (0, j + j0)),
                  pl.BlockSpec((M_CONV, tn), lambda bb, i, j: (0, conv_idx(j))),
                  pl.BlockSpec((1, tn), lambda bb, i, j: (0, conv_idx(j)))],
        out_specs=[pl.BlockSpec((1, tm, tn), lambda bb, i, j: (bb, i, conv_idx(j))),
                   pl.BlockSpec((1, tm, LANES), lambda bb, i, j: (bb, i, 0))],
        out_shape=[jax.ShapeDtypeStruct((b, l, n_conv * tn), BF16), jax.ShapeDtypeStruct((b, l, LANES), F32)],
        scratch_shapes=[pltpu.VMEM((tmh, tn), F32)],
        compiler_params=_cparams("parallel", "parallel", "arbitrary"),
        name="inproj",
    )(h, w, conv_w, conv_b.reshape(1, -1))


def _conv_taps(prev_ref, main_ref, next_ref, w_ref, b_ref, ktaps):
    i = pl.program_id(2)
    nblk = pl.num_programs(2)
    tl = main_ref.shape[1]
    pad = ktaps // 2
    prev = jnp.where(i > 0, prev_ref[0].astype(F32), 0.0)
    nxt = jnp.where(i < nblk - 1, next_ref[0].astype(F32), 0.0)
    ext = jnp.concatenate([prev, main_ref[0].astype(F32), nxt], axis=0)
    rows = tl + 2 * HALO
    acc = None
    for j in range(ktaps):
        shifted = ext if j == pad else pltpu.roll(ext, (pad - j) % rows, axis=0)
        term = shifted[HALO:HALO + tl] * w_ref[j:j + 1, :]
        acc = term if acc is None else acc + term
    return acc + b_ref[...]


def _conv_specs(tl, cw, l, col_block0):
    rpb = tl // HALO
    last = l // HALO - 1
    return [
        pl.BlockSpec((1, HALO, cw), lambda bb, c, i: (bb, jnp.maximum(i * rpb - 1, 0), c + col_block0)),
        pl.BlockSpec((1, tl, cw), lambda bb, c, i: (bb, i, c + col_block0)),
        pl.BlockSpec((1, HALO, cw), lambda bb, c, i: (bb, jnp.minimum((i + 1) * rpb, last), c + col_block0)),
    ]


def _hconv_kernel(p0, m0, n0, p1, m1, n1, p2, m2, n2, w0, b0, w1, b1, w2, b2, x0_ref, u_ref):
    x0_ref[0] = _conv_taps(p0, m0, n0, w0, b0, H_SHORT).astype(x0_ref.dtype)
    x1 = _conv_taps(p1, m1, n1, w1, b1, H_SHORT)
    v = _conv_taps(p2, m2, n2, w2, b2, H_SHORT)
    u_ref[0] = (x1 * v).astype(u_ref.dtype)


def _hconv(p, col0, w, bias, d):
    b, l, _ = p.shape
    tl = min(l, 1024)
    cw = min(d, COL_TILE)
    nb = d // cw
    specs, wspecs = [], []
    for s in range(3):
        specs += _conv_specs(tl, cw, l, col0 // cw + s * nb)
        wspecs += [pl.BlockSpec((H_SHORT, cw), functools.partial(lambda bb, c, i, s: (0, c + s * nb), s=s)),
                   pl.BlockSpec((1, cw), functools.partial(lambda bb, c, i, s: (0, c + s * nb), s=s))]
    bias2 = bias.reshape(1, 3 * d)
    out_spec = pl.BlockSpec((1, tl, cw), lambda bb, c, i: (bb, i, c))
    return pl.pallas_call(
        _hconv_kernel,
        grid=(b, nb, l // tl),
        in_specs=specs + wspecs,
        out_specs=[out_spec, out_spec],
        out_shape=[jax.ShapeDtypeStruct((b, l, d), BF16)] * 2,
        compiler_params=_cparams("parallel", "parallel", "arbitrary"),
        name="hconv",
    )(p, p, p, p, p, p, p, p, p, w, bias2, w, bias2, w, bias2)


def _ssd_kernel(*refs, reverse, with_output, epilogue, heads):
    if epilogue:
        (xbc_ref, dt_ref, dtb_ref, alog_ref, init_ref, sel_ref, yf_ref, z_ref, dsk_ref, ng_ref,
         y_ref, fin_ref, s_scr) = refs
    elif with_output:
        xbc_ref, dt_ref, dtb_ref, alog_ref, init_ref, sel_ref, y_ref, fin_ref, s_scr = refs
    else:
        xbc_ref, dt_ref, dtb_ref, alog_ref, init_ref, fin_ref, s_scr = refs
    t = M_CHUNK
    n = M_STATE
    assert n == t
    di = heads * M_HEADDIM
    gw = di // M_GROUPS
    rpg = heads // M_GROUPS
    lane0 = heads if reverse else 0
    edge = 0 if reverse else t - 1
    c = pl.program_id(1)

    nb = xbc_ref.shape[0]

    @pl.when(c == 0)
    def _():
        s_scr[...] = init_ref[...]

    row = lax.broadcasted_iota(jnp.int32, (t, t), 0)
    col = lax.broadcasted_iota(jnp.int32, (t, t), 1)
    mask = (row <= col) if reverse else (row >= col)
    lane = lax.broadcasted_iota(jnp.int32, (1, LANES), 1)
    lo = lane < M_HEADDIM
    mlo = jnp.where(lo, 1.0, 0.0).astype(BF16)
    mhi = jnp.where(lo, 0.0, 1.0).astype(BF16)

    def chunk_decays(bi):
        x = dt_ref[bi] + dtb_ref[...]
        dt = jnp.maximum(x, 0.0) + jnp.log1p(jnp.exp(-jnp.abs(x)))
        dta = dt * (-jnp.exp(alog_ref[...]))
        acs = jnp.dot(mask.astype(F32), dta, preferred_element_type=F32, precision=HIGHEST)
        acs_t = acs.T
        dt_t = dt.T
        acs2 = acs * LOG2_E
        arow_dt2 = (acs_t - jnp.log(dt_t)) * LOG2_E
        w_t = jnp.exp(acs_t[:, edge:edge + 1] - acs_t) * dt_t
        dec = jnp.exp(acs[edge:edge + 1, :])
        eacs = jnp.exp(acs).astype(BF16) if with_output else None
        return acs2, arow_dt2, w_t, dec, eacs

    def group(bi, g, acs2, arow_dt2, w_t, dec, eacs):
        bt = xbc_ref[bi, :, di + g * n:di + (g + 1) * n].astype(F32).T
        if with_output:
            cm = xbc_ref[bi, :, di + M_GROUPS * n + g * n:di + M_GROUPS * n + (g + 1) * n]
            cb = jnp.dot(cm, bt.astype(BF16), preferred_element_type=F32)
            y_off = jnp.dot(cm, s_scr[bi, g].astype(BF16), preferred_element_type=F32)
        y_parts = []
        for q in range(rpg // 2):
            c0 = g * gw + q * LANES
            xp = xbc_ref[bi, :, c0:c0 + LANES]
            sp = s_scr[bi, g, :, q * LANES:(q + 1) * LANES]
            x2 = jnp.concatenate([xp * mlo, xp * mhi], axis=0)
            hl0 = lane0 + g * rpg + 2 * q
            lhs, btw, decs = [], [], []
            for r in range(2):
                hl = hl0 + r
                if with_output:
                    acol = jnp.broadcast_to(acs2[:, hl:hl + 1], (t, t))
                    dk = jnp.exp2(jnp.where(mask, acol - arow_dt2[hl:hl + 1, :], -jnp.inf))
                    lhs.append((cb * dk).astype(BF16))
                btw.append((bt * w_t[hl:hl + 1, :]).astype(BF16))
                decs.append(jnp.broadcast_to(dec[:, hl:hl + 1], (1, LANES)))
            if with_output:
                e_pair = jnp.dot(eacs, sel_ref[hl0 // 2], preferred_element_type=F32)
                y_diag = jnp.dot(jnp.concatenate(lhs, axis=1), x2, preferred_element_type=F32)
                y_parts.append(y_diag + e_pair * y_off[:, q * LANES:(q + 1) * LANES])
            ds = jnp.dot(jnp.concatenate(btw, axis=1), x2, preferred_element_type=F32)
            s_scr[bi, g, :, q * LANES:(q + 1) * LANES] = jnp.where(lo, decs[0], decs[1]) * sp + ds
        if with_output:
            yg = jnp.concatenate(y_parts, axis=1) if len(y_parts) > 1 else y_parts[0]
            sl = slice(g * gw, (g + 1) * gw)
            if epilogue:
                yg = yg + yf_ref[bi, :, sl].astype(F32)
                v = (yg + xbc_ref[bi, :, sl].astype(F32) * dsk_ref[:, sl]) * _silu(z_ref[bi, :, sl].astype(F32))
                ms = jnp.mean(v * v, axis=-1, keepdims=True)
                y_ref[bi, :, sl] = (v * lax.rsqrt(ms + RMS_EPS) * ng_ref[:, sl]).astype(y_ref.dtype)
            else:
                y_ref[bi, :, sl] = yg.astype(y_ref.dtype)

    decays = [chunk_decays(bi) for bi in range(nb)]
    for g in range(M_GROUPS):
        for bi in range(nb):
            group(bi, g, *decays[bi])

    @pl.when(c == pl.num_programs(1) - 1)
    def _():
        fin_ref[...] = s_scr[...]


def _pair_selectors():
    p = jnp.arange(LANES // 2)[:, None, None]
    h = jnp.arange(LANES)[None, :, None]
    lane = jnp.arange(LANES)[None, None, :]
    return (h == 2 * p + (lane >= M_HEADDIM)).astype(BF16)


def _ssd(xbc, dtp, dtb, alog, init, heads, reverse, with_output, extra=None):
    b, l, xw = xbc.shape
    t = M_CHUNK
    nc = l // t
    di = heads * M_HEADDIM
    gw = di // M_GROUPS
    epilogue = extra is not None
    cidx = (lambda c: nc - 1 - c) if reverse else (lambda c: c)
    nb = SSD_BATCH_ROWS if b % SSD_BATCH_ROWS == 0 else 1
    state_spec = pl.BlockSpec((nb, M_GROUPS, M_STATE, gw), lambda bb, c: (bb, 0, 0, 0))
    in_specs = [pl.BlockSpec((nb, t, xw), lambda bb, c: (bb, cidx(c), 0)),
                pl.BlockSpec((nb, t, LANES), lambda bb, c: (bb, cidx(c), 0)),
                pl.BlockSpec((1, LANES), lambda bb, c: (0, 0)),
                pl.BlockSpec((1, LANES), lambda bb, c: (0, 0)),
                state_spec]
    args = [xbc, dtp, dtb, alog, init]
    out_specs, out_shape = [], []
    if with_output:
        in_specs.append(pl.BlockSpec((LANES // 2, LANES, LANES), lambda bb, c: (0, 0, 0)))
        args.append(_pair_selectors())
    if epilogue:
        yf, p, z_col0, dsk, ng = extra
        in_specs += [pl.BlockSpec((nb, t, di), lambda bb, c: (bb, cidx(c), 0)),
                     pl.BlockSpec((nb, t, di), lambda bb, c: (bb, cidx(c), z_col0 // di)),
                     pl.BlockSpec((1, di), lambda bb, c: (0, 0)),
                     pl.BlockSpec((1, di), lambda bb, c: (0, 0))]
        args += [yf, p, dsk, ng]
    if with_output:
        out_specs.append(pl.BlockSpec((nb, t, di), lambda bb, c: (bb, cidx(c), 0)))
        out_shape.append(jax.ShapeDtypeStruct((b, l, di), BF16))
    out_specs.append(state_spec)
    out_shape.append(jax.ShapeDtypeStruct((b, M_GROUPS, M_STATE, gw), F32))
    outs = pl.pallas_call(
        functools.partial(_ssd_kernel, reverse=reverse, with_output=with_output, epilogue=epilogue, heads=heads),
        grid=(b // nb, nc),
        in_specs=in_specs,
        out_specs=out_specs,
        out_shape=out_shape,
        scratch_shapes=[pltpu.VMEM((nb, M_GROUPS, M_STATE, gw), F32)],
        compiler_params=_cparams("parallel", "arbitrary"),
        name="ssd_bwd" if reverse else "ssd_fwd",
    )(*args)
    return (outs[0], outs[1]) if with_output else (None, outs[0])


def _filt_kernel(f_ref, w1_ref, b1_ref, w2_ref, b2_ref, fr_ref, w3_ref, dl_ref, o_ref, ss_ref):
    f = f_ref[...]
    fr = fr_ref[...]
    h = jnp.sin(fr * (jnp.dot(f, w1_ref[...], preferred_element_type=F32, precision=HIGHEST) + b1_ref[...]))
    h = jnp.sin(fr * (jnp.dot(h, w2_ref[...], preferred_element_type=F32, precision=HIGHEST) + b2_ref[...]))
    filt = jnp.dot(h, w3_ref[...], preferred_element_type=F32, precision=HIGHEST)
    tcol = f[:, H_EMB:H_EMB + 1]
    mcol = f[:, H_EMB + 1:H_EMB + 2]
    out = filt * jnp.exp(-tcol * dl_ref[...]) * mcol
    o_ref[...] = out

    @pl.when(pl.program_id(1) == 0)
    def _():
        ss_ref[...] = jnp.zeros_like(ss_ref)

    ss_ref[...] += jnp.sum(out * out, axis=0, keepdims=True)


def _filter_features(l):
    t = jnp.linspace(0.0, 1.0, l, dtype=F32)[:, None]
    ang = (2.0 * math.pi / l) * jnp.arange(l, dtype=F32)[:, None]
    nb = (H_EMB - 1) // 2
    bands = jnp.linspace(1e-4, nb - 1, nb, dtype=F32)[None, :]
    feats = jnp.concatenate([t, jnp.cos(bands * ang), -jnp.sin(bands * ang), t, jnp.ones_like(t)], axis=-1)
    rev = jnp.concatenate([feats[0:1] * 0.0, jnp.flip(feats[1:], axis=0)], axis=0)
    full = jnp.concatenate([feats, rev], axis=0)
    return jnp.pad(full, ((0, 0), (0, FEAT_COLS - full.shape[1])))


def _hyena_filter(l, w1, b1, w2, b2, freq, w3):
    hid = w2.shape[0]
    d = w3.shape[1] // 2
    feats = _filter_features(l)
    w1p = jnp.pad(w1, ((0, FEAT_COLS - w1.shape[0]), (0, 0)))
    deltas = jnp.abs(jnp.linspace(math.log(H_DECAY_TARGET) / H_SLOW_PCT, math.log(H_DECAY_TARGET) / H_FAST_PCT,
                                  d, dtype=F32))[None, :]
    tl = min(l, 512)
    cw = d
    nb = d // cw
    nrow = l // tl
    small = lambda shape: pl.BlockSpec(shape, lambda c, i: (0, 0))
    return pl.pallas_call(
        _filt_kernel,
        grid=(nb, 2 * nrow),
        in_specs=[pl.BlockSpec((tl, FEAT_COLS), lambda c, i: (i, 0)),
                  small((FEAT_COLS, hid)), small((1, hid)), small((hid, hid)), small((1, hid)), small((1, hid)),
                  pl.BlockSpec((hid, cw), lambda c, i: (0, c + (i // nrow) * nb)),
                  pl.BlockSpec((1, cw), lambda c, i: (0, c))],
        out_specs=[pl.BlockSpec((tl, cw), lambda c, i: (i, c)),
                   pl.BlockSpec((1, cw), lambda c, i: (0, c))],
        out_shape=[jax.ShapeDtypeStruct((2 * l, d), F32), jax.ShapeDtypeStruct((1, d), F32)],
        compiler_params=_cparams("parallel", "arbitrary"),
        name="hyena_filter",
    )(feats, w1p, b1.reshape(1, hid), w2, b2.reshape(1, hid), freq.reshape(1, hid), w3, deltas)


def _fft_split(n):
    n1 = 1 << ((n.bit_length() - 1 + 1) // 2)
    return n1, n // n1


def _dft_consts(n1, n2):
    n = n1 * n2
    h = n1 // 2
    k1 = jnp.arange(n1, dtype=jnp.int32)
    t1 = jnp.arange(n1, dtype=jnp.int32)
    t2 = jnp.arange(n2, dtype=jnp.int32)
    idx = (k1[None, :, None] * (n2 * t1[None, None, :] + t2[:, None, None])) % n
    ang = idx.astype(F32) * (2.0 * math.pi / n)
    cs, sn = jnp.cos(ang), jnp.sin(ang)
    ch, sh = cs[..., :h], sn[..., :h]
    t_data = jnp.concatenate([jnp.concatenate([ch, sh], -1), jnp.concatenate([-sh, ch], -1)], axis=1)
    t_filt = jnp.concatenate([cs, -sn], axis=1)
    ct, st = jnp.swapaxes(ch, 1, 2), jnp.swapaxes(sh, 1, 2)
    t_inv = jnp.concatenate([jnp.concatenate([ct, -st], -1), jnp.concatenate([st, ct], -1)], axis=1) / n
    k2 = jnp.arange(n2, dtype=jnp.int32)
    ang2 = ((k2[:, None] * k2[None, :]) % n2).astype(F32) * (2.0 * math.pi / n2)
    c2, s2 = jnp.cos(ang2), jnp.sin(ang2)
    m_fwd = jnp.concatenate([jnp.concatenate([c2, s2], -1), jnp.concatenate([-s2, c2], -1)], axis=0)
    m_inv = jnp.concatenate([jnp.concatenate([c2, -s2], -1), jnp.concatenate([s2, c2], -1)], axis=0)
    return dict(t_data=t_data.astype(BF16), t_filt=t_filt.astype(BF16), t_inv=t_inv.astype(BF16),
                m_fwd=m_fwd.astype(BF16), m_inv=m_inv.astype(BF16))


def _fft1_kernel(x_ref, t_ref, o_ref):
    a = x_ref.shape[1]
    xs = [pltpu.einshape("rjc->jrc", x_ref[0, s].astype(F32)) for s in range(a)]
    n1 = o_ref.shape[2]
    for j in range(SUBLANES):
        x = jnp.concatenate([xs[s][j] for s in range(a)], axis=0).astype(BF16)
        o = jnp.dot(t_ref[j], x, preferred_element_type=F32)
        o_ref[0, j] = _pack_complex(o[:n1], o[n1:])


def _pack_complex(re, im):
    r = lax.bitcast_convert_type(re.astype(BF16).astype(F32), jnp.uint32)
    i = lax.bitcast_convert_type(im.astype(BF16).astype(F32), jnp.uint32)
    return (r >> 16) | i


def _unpack_complex(w):
    re = lax.bitcast_convert_type(w << 16, F32)
    im = lax.bitcast_convert_type(w & jnp.uint32(0xFFFF0000), F32)
    return re, im


def _fft_stage1(xv, tmat, n1, n2, d):
    p, a, r, _, _ = xv.shape
    cw = min(d, 2 * COL_TILE)
    return pl.pallas_call(
        _fft1_kernel,
        grid=(n2 // SUBLANES, d // cw, p),
        in_specs=[pl.BlockSpec((1, a, r, SUBLANES, cw), lambda j, c, pp: (pp, 0, 0, j, c)),
                  pl.BlockSpec((SUBLANES, 2 * n1, n1), lambda j, c, pp: (j, 0, 0))],
        out_specs=pl.BlockSpec((1, SUBLANES, n1, cw), lambda j, c, pp: (pp, j, 0, c)),
        out_shape=jax.ShapeDtypeStruct((p, n2, n1, d), jnp.uint32),
        compiler_params=_cparams("parallel", "parallel", "parallel"),
        name="fft_stage1",
    )(xv, tmat)


def _fft2_kernel(a_ref, af_ref, ss_ref, mf_ref, mi_ref, o_ref, h_scr):
    n2 = a_ref.shape[1]

    def stage(w):
        re, im = _unpack_complex(w)
        return jnp.dot(mf_ref[...], jnp.concatenate([re, im], axis=0).astype(BF16), preferred_element_type=F32)

    @pl.when(pl.program_id(2) == 0)
    def _():
        af = pltpu.einshape("tkc->ktc", af_ref[0])
        for k in range(SUBLANES):
            h_scr[k] = stage(af[k]) * lax.rsqrt(ss_ref[...] + RMS_EPS)

    cw = a_ref.shape[3]
    half = max(cw // 2, LANES)
    for c0 in range(0, cw, half):
        a = pltpu.einshape("tkc->ktc", a_ref[0, :, :, c0:c0 + half])
        outs = []
        for k in range(SUBLANES):
            xk = stage(a[k])
            xr, xi = xk[:n2], xk[n2:]
            hr, hi = h_scr[k, 0:n2, c0:c0 + half], h_scr[k, n2:, c0:c0 + half]
            z = jnp.concatenate([xr * hr - xi * hi, xr * hi + xi * hr], axis=0).astype(BF16)
            bk = jnp.dot(mi_ref[...], z, preferred_element_type=F32)
            outs.append(_pack_complex(bk[:n2], bk[n2:]))
        o_ref[0, :, :, c0:c0 + half] = pltpu.einshape("ktc->tkc", jnp.stack(outs, axis=0))


def _fft_stage2(av, afv, ss, m_fwd, m_inv, n1, n2, d):
    p = av.shape[0]
    cw = min(d, COL_TILE)
    kb = SUBLANES
    return pl.pallas_call(
        _fft2_kernel,
        grid=(n1 // kb, d // cw, p),
        in_specs=[pl.BlockSpec((1, n2, kb, cw), lambda k, c, pp: (pp, 0, k, c)),
                  pl.BlockSpec((1, n2, kb, cw), lambda k, c, pp: (0, 0, k, c)),
                  pl.BlockSpec((1, cw), lambda k, c, pp: (0, c)),
                  pl.BlockSpec((2 * n2, 2 * n2), lambda k, c, pp: (0, 0)),
                  pl.BlockSpec((2 * n2, 2 * n2), lambda k, c, pp: (0, 0))],
        out_specs=pl.BlockSpec((1, n2, kb, cw), lambda k, c, pp: (pp, 0, k, c)),
        out_shape=jax.ShapeDtypeStruct(av.shape, jnp.uint32),
        scratch_shapes=[pltpu.VMEM((kb, 2 * n2, cw), F32)],
        compiler_params=_cparams("parallel", "parallel", "arbitrary"),
        name="fft_stage2",
    )(av, afv, ss, m_fwd, m_inv)


def _fft3_kernel(b_ref, t_ref, u_ref, x0_ref, bias_ref, o_ref):
    n1 = b_ref.shape[2]
    h = n1 // 2
    ys = []
    for j in range(SUBLANES):
        re, im = _unpack_complex(b_ref[0, j])
        bv = jnp.concatenate([re, im], axis=0).astype(BF16)
        ys.append(jnp.dot(t_ref[j], bv, preferred_element_type=F32))
    y = pltpu.einshape("jtc->tjc", jnp.stack(ys, axis=0))
    for s in range(2):
        conv = y[s * h:(s + 1) * h] + u_ref[0, s].astype(F32) * bias_ref[...]
        o_ref[0, s] = (x0_ref[0, s].astype(F32) * conv).astype(o_ref.dtype)


def _fft_stage3(bv, t_inv, uv, x0v, bias, n1, n2, d):
    p = bv.shape[0]
    h = n1 // 2
    cw = min(d, 2 * COL_TILE)
    io_spec = pl.BlockSpec((1, 2, h, SUBLANES, cw), lambda j, c, pp: (pp, 0, 0, j, c))
    return pl.pallas_call(
        _fft3_kernel,
        grid=(n2 // SUBLANES, d // cw, p),
        in_specs=[pl.BlockSpec((1, SUBLANES, n1, cw), lambda j, c, pp: (pp, j, 0, c)),
                  pl.BlockSpec((SUBLANES, n1, 2 * n1), lambda j, c, pp: (j, 0, 0)),
                  io_spec, io_spec,
                  pl.BlockSpec((1, cw), lambda j, c, pp: (0, c))],
        out_specs=io_spec,
        out_shape=jax.ShapeDtypeStruct(uv.shape, BF16),
        compiler_params=_cparams("parallel", "parallel", "parallel"),
        name="fft_stage3",
    )(bv, t_inv, uv, x0v, bias)


def _filter_spectrum(l, consts, lp):
    n1, n2 = _fft_split(2 * l)
    d = lp['h_bias'].shape[0]
    full, ss = _hyena_filter(l, lp['hf_w1'], lp['hf_b1'], lp['hf_w2'], lp['hf_b2'], lp['hf_freq'], lp['hf_w3'])
    return _fft_stage1(full.reshape(1, 1, n1, n2, d), consts['t_filt'], n1, n2, d), ss


def _long_conv(u, x0, hf, consts, bias):
    b, l, d = u.shape
    n1, n2 = _fft_split(2 * l)
    p = b // 2
    h = n1 // 2
    uv = u.reshape(p, 2, h, n2, d)
    a = _fft_stage1(uv, consts['t_data'], n1, n2, d)
    bv = _fft_stage2(a, hf[0], hf[1], consts['m_fwd'], consts['m_inv'], n1, n2, d)
    y = _fft_stage3(bv, consts['t_inv'], uv, x0.reshape(p, 2, h, n2, d), bias.reshape(1, d), n1, n2, d)
    return y.reshape(b, l, d)


def _merge_kernel(ym_ref, yh_ref, ga_ref, gb_ref, wm_ref, wh_ref, o_ref):
    a = jnp.dot(ym_ref[0], wm_ref[...], preferred_element_type=F32)
    bb = jnp.dot(yh_ref[0], wh_ref[...], preferred_element_type=F32)
    ga = jax.nn.sigmoid(ga_ref[0].astype(F32))
    gb = jax.nn.sigmoid(gb_ref[0].astype(F32))
    o_ref[0] = (ga * a + gb * bb).astype(o_ref.dtype)


def _merge(ym, yh, p, gate_col0, wm, wh):
    b, l, di = ym.shape
    d = yh.shape[2]
    tm = min(l, 1024)
    tn = min(d, COL_TILE)
    g0 = gate_col0 // tn
    nb = d // tn
    return pl.pallas_call(
        _merge_kernel,
        grid=(b, l // tm, nb),
        in_specs=[pl.BlockSpec((1, tm, di), lambda bb, i, j: (bb, i, 0)),
                  pl.BlockSpec((1, tm, d), lambda bb, i, j: (bb, i, 0)),
                  pl.BlockSpec((1, tm, tn), lambda bb, i, j: (bb, i, g0 + j)),
                  pl.BlockSpec((1, tm, tn), lambda bb, i, j: (bb, i, g0 + nb + j)),
                  pl.BlockSpec((di, tn), lambda bb, i, j: (0, j)),
                  pl.BlockSpec((d, tn), lambda bb, i, j: (0, j))],
        out_specs=pl.BlockSpec((1, tm, tn), lambda bb, i, j: (bb, i, j)),
        out_shape=jax.ShapeDtypeStruct((b, l, d), BF16),
        compiler_params=_cparams("parallel", "parallel", "arbitrary"),
        name="merge",
    )(ym, yh, p, p, wm, wh)


def _resid_mm_kernel(a_ref, w_ref, x_ref, g_ref, o_ref):
    o_ref[0] = x_ref[0] + g_ref[0] * jnp.dot(a_ref[0], w_ref[...], preferred_element_type=F32)


def _resid_mm(a, w, x, gate):
    b, l, k = a.shape
    d = w.shape[1]
    tm = min(l, 1024)
    tn = min(d, 2 * COL_TILE)
    return pl.pallas_call(
        _resid_mm_kernel,
        grid=(b, l // tm, d // tn),
        in_specs=[pl.BlockSpec((1, tm, k), lambda bb, i, j: (bb, i, 0)),
                  pl.BlockSpec((k, tn), lambda bb, i, j: (0, j)),
                  pl.BlockSpec((1, tm, tn), lambda bb, i, j: (bb, i, j)),
                  pl.BlockSpec((1, 1, tn), lambda bb, i, j: (bb, 0, j))],
        out_specs=pl.BlockSpec((1, tm, tn), lambda bb, i, j: (bb, i, j)),
        out_shape=jax.ShapeDtypeStruct((b, l, d), F32),
        compiler_params=_cparams("parallel", "parallel", "arbitrary"),
        name="resid_mm",
    )(a, w, x, gate)


def _ffn_kernel(x_ref, g_ref, sh_ref, sc_ref, gt_ref, wg_ref, wu_ref, wd_ref, fg_ref, o_ref, h_scr, ms_scr, *,
                final_norm):
    j = pl.program_id(2)

    @pl.when(j == 0)
    def _():
        h_scr[...] = _modnorm_two_pass(x_ref, ms_scr, g_ref[...], sh_ref[0], sc_ref[0]).astype(BF16)
        o_ref[...] = jnp.zeros_like(o_ref)

    tm = h_scr.shape[0]
    rows_per_pass = min(tm, 512)
    for r0 in range(0, tm, rows_per_pass):
        h = h_scr[r0:r0 + rows_per_pass, :]
        gg = jnp.dot(h, wg_ref[...], preferred_element_type=F32)
        uu = jnp.dot(h, wu_ref[...], preferred_element_type=F32)
        o_ref[0, r0:r0 + rows_per_pass, :] += jnp.dot((_silu(gg) * uu).astype(BF16), wd_ref[...],
                                                      preferred_element_type=F32)

    @pl.when(j == pl.num_programs(2) - 1)
    def _():
        y = x_ref[0] + gt_ref[0] * o_ref[0]
        if final_norm:
            ms = jnp.mean(y * y, axis=-1, keepdims=True)
            y = y * lax.rsqrt(ms + RMS_EPS) * fg_ref[...]
        o_ref[0] = y


def _ffn(x, g, shift, scale, gate, w_gu, w_down, final_g=None):
    b, l, d = x.shape
    f = w_down.shape[0]
    tm = min(l, 1024)
    fg = jnp.ones((1, d), F32) if final_g is None else final_g.reshape(1, d)
    tf = 256
    nf = f // tf
    vec = pl.BlockSpec((1, 1, d), lambda bb, i, j: (bb, 0, 0))
    return pl.pallas_call(
        functools.partial(_ffn_kernel, final_norm=final_g is not None),
        grid=(b, l // tm, nf),
        in_specs=[pl.BlockSpec((1, tm, d), lambda bb, i, j: (bb, i, 0)),
                  pl.BlockSpec((1, d), lambda bb, i, j: (0, 0)),
                  vec, vec, vec,
                  pl.BlockSpec((d, tf), lambda bb, i, j: (0, j)),
                  pl.BlockSpec((d, tf), lambda bb, i, j: (0, j + nf)),
                  pl.BlockSpec((tf, d), lambda bb, i, j: (j, 0)),
                  pl.BlockSpec((1, d), lambda bb, i, j: (0, 0))],
        out_specs=pl.BlockSpec((1, tm, d), lambda bb, i, j: (bb, i, 0)),
        out_shape=jax.ShapeDtypeStruct((b, l, d), F32),
        scratch_shapes=[pltpu.VMEM((tm, d), BF16), pltpu.VMEM((tm, 1), F32)],
        compiler_params=_cparams("parallel", "parallel", "arbitrary"),
        name="ffn",
    )(x, g.reshape(1, d), shift, scale, gate, w_gu, w_gu, w_down, fg)


def _grid_pos_embed(rows, d):
    r, col = jnp.meshgrid(jnp.arange(rows), jnp.arange(GRID_W), indexing='ij')
    quarter = d // 4
    omega = 1.0 / (POS_BASE ** (jnp.arange(quarter, dtype=F32) / quarter))

    def axis_embed(pos):
        ang = pos.reshape(-1)[:, None].astype(F32) * omega[None, :]
        return jnp.concatenate([jnp.sin(ang), jnp.cos(ang)], axis=-1)

    return jnp.concatenate([axis_embed(r), axis_embed(col)], axis=-1).astype(F32)


class _Cols:
    def __init__(self, d):
        self.d = d
        self.di = 2 * d
        self.heads = self.di // M_HEADDIM
        self.xbc = self.di + 2 * M_GROUPS * M_STATE
        self.z0 = 0
        self.gate0 = self.z0 + self.di
        self.hy0 = self.gate0 + 2 * d
        self.xbc0 = self.hy0 + 3 * d
        self.dt0 = self.xbc0 + self.xbc
        self.total = self.dt0 + COL_TILE
        assert 2 * self.heads <= LANES and self.heads % (2 * M_GROUPS) == 0
        for off in (self.gate0, self.hy0, self.xbc0, self.dt0):
            assert off % COL_TILE == 0


def _pack_w_in(w_in, cols):
    o_dt = cols.xbc
    o_z = o_dt + 2 * cols.heads
    o_hy = o_z + cols.di
    o_gate = o_hy + 3 * cols.d
    parts = [w_in[:, o_z:o_hy], w_in[:, o_gate:], w_in[:, o_hy:o_gate], w_in[:, :o_dt], w_in[:, o_dt:o_z]]
    w = jnp.concatenate(parts, axis=1)
    return jnp.pad(w, ((0, 0), (0, cols.total - w.shape[1]))).astype(BF16)


def _pad_lanes(v):
    v = v.reshape(1, -1)
    return jnp.pad(v, ((0, 0), (0, LANES - v.shape[1])))


def _fold(t):
    return t.reshape(1, -1, t.shape[-1])


def _mix_sublayer(x, mods, lp, cols, hf, consts, init_f, init_b, fold_batch=False):
    h = _prenorm(x, lp['norm1_g'], mods[0], mods[1])
    p = _proj_plain(h, lp['w_in'], cols.xbc0)
    xbc, dtp = _inproj(h, lp['w_in'], cols.xbc0, lp['m_conv_w'], lp['m_conv_b'])
    y_f, fin_f = _ssd(xbc, dtp, lp['dtb'], lp['alog'], init_f, cols.heads, False, True)
    ym, fin_b = _ssd(xbc, dtp, lp['dtb'], lp['alog'], init_b, cols.heads, True, True,
                     extra=(y_f, p, cols.z0, lp['dskip'], lp['m_norm_g']))
    x0, u = _hconv(p, cols.hy0, lp['h_conv_w'], lp['h_conv_b'], cols.d)
    yh = _long_conv(u, x0, hf, consts, lp['h_bias'])
    if fold_batch:
        merged = _merge(_fold(ym), _fold(yh), _fold(p), cols.gate0, lp['m_w_out'], lp['h_w_out'])
        out = _resid_mm(merged, lp['w_merge_out'], _fold(x), mods[2][:1]).reshape(x.shape)
    else:
        merged = _merge(ym, yh, p, cols.gate0, lp['m_w_out'], lp['h_w_out'])
        out = _resid_mm(merged, lp['w_merge_out'], x, mods[2])
    return out, fin_f, fin_b


def _ctx_final_states(x, mods, lp, cols, init_f, init_b):
    h = _prenorm(x, lp['norm1_g'], mods[0], mods[1])
    xbc, dtp = _inproj(h, lp['w_in'], cols.xbc0, lp['m_conv_w'], lp['m_conv_b'])
    _, fin_f = _ssd(xbc, dtp, lp['dtb'], lp['alog'], init_f, cols.heads, False, False)
    _, fin_b = _ssd(xbc, dtp, lp['dtb'], lp['alog'], init_b, cols.heads, True, False)
    return fin_f, fin_b


def kernel(x, c, ctx, c_ctx, ada_w, ada_b, norm1_g, w_in, m_conv_w, m_conv_b, m_dt_bias, m_a_log, m_d, m_norm_g,
           m_w_out, h_conv_w, h_conv_b, hf_w1, hf_b1, hf_w2, hf_b2, hf_freq, hf_w3, h_bias, h_w_out, w_merge_out,
           norm2_g, ffn_w_gu, ffn_w_down, final_g):
    b, l, d = x.shape
    lc = ctx.shape[1]
    depth = ada_w.shape[0]
    cols = _Cols(d)
    assert b % 2 == 0 and b + 1 <= SUBLANES and l % M_CHUNK == 0 and lc % M_CHUNK == 0

    cvecs = jnp.concatenate([c, c_ctx[None], jnp.zeros((SUBLANES - b - 1, d), F32)], axis=0)
    mods_all = _mods(cvecs, ada_w, ada_b)
    consts_l = _dft_consts(*_fft_split(2 * l))
    consts_c = _dft_consts(*_fft_split(2 * lc))

    x_l = _add_pos(x, _grid_pos_embed(l // GRID_W, d))
    x_c = ctx
    zero_state = jnp.zeros((b, M_GROUPS, M_STATE, cols.di // M_GROUPS), F32)
    for i in range(depth):
        lp = dict(norm1_g=norm1_g[i], w_in=_pack_w_in(w_in[i], cols), m_conv_w=m_conv_w[i], m_conv_b=m_conv_b[i],
                  dtb=_pad_lanes(m_dt_bias[i]), alog=_pad_lanes(m_a_log[i]),
                  dskip=jnp.repeat(m_d[i], M_HEADDIM).reshape(1, cols.di), m_norm_g=m_norm_g[i].reshape(1, cols.di),
                  m_w_out=m_w_out[i].astype(BF16), h_conv_w=h_conv_w[i], h_conv_b=h_conv_b[i],
                  hf_w1=hf_w1[i], hf_b1=hf_b1[i], hf_w2=hf_w2[i], hf_b2=hf_b2[i], hf_freq=hf_freq[i], hf_w3=hf_w3[i],
                  h_bias=h_bias[i], h_w_out=h_w_out[i].astype(BF16), w_merge_out=w_merge_out[i].astype(BF16),
                  norm2_g=norm2_g[i], ffn_w_gu=ffn_w_gu[i].astype(BF16), ffn_w_down=ffn_w_down[i].astype(BF16))
        m = mods_all[i].reshape(SUBLANES, ADA_CHUNKS, d)
        mods_l = [m[:b, k][:, None, :] for k in range(ADA_CHUNKS)]
        mods_c = [jnp.broadcast_to(m[b, k][None, None, :], (b, 1, d)) for k in range(ADA_CHUNKS)]
        if i < depth - 1:
            hf_c = _filter_spectrum(lc, consts_c, lp)
            x_c_mixed, fin_f, fin_b = _mix_sublayer(x_c, mods_c, lp, cols, hf_c, consts_c, zero_state, zero_state,
                                                    fold_batch=True)
        else:
            fin_f, fin_b = _ctx_final_states(x_c, mods_c, lp, cols, zero_state, zero_state)
        hf_l = _filter_spectrum(l, consts_l, lp)
        x_l, _, _ = _mix_sublayer(x_l, mods_l, lp, cols, hf_l, consts_l, fin_f, fin_b)
        x_l = _ffn(x_l, lp['norm2_g'], mods_l[3], mods_l[4], mods_l[5], lp['ffn_w_gu'], lp['ffn_w_down'],
                   final_g=final_g if i == depth - 1 else None)
        if i < depth - 1:
            x_c = _ffn(_fold(x_c_mixed), lp['norm2_g'], mods_c[3][:1], mods_c[4][:1], mods_c[5][:1], lp['ffn_w_gu'],
                       lp['ffn_w_down']).reshape(x_c_mixed.shape)
    return x_l
```
